```python
import math
import jax
import jax.numpy as jnp
from jax import lax
import numpy as np


D_MODEL = 1024
BATCH = 8
SEQ = 2048
DEPTH = 2

CHUNK = 64
Q_BLOCK = 128
BRANCH_WIDTH = D_MODEL // 2
SB_HEADS = 8
SB_HEAD_DIM = BRANCH_WIDTH // SB_HEADS
POOL_WINDOWS = (2, 4, 8, 16)
POOL_GROUPS = len(POOL_WINDOWS)
POOL_GROUP_DIM = BRANCH_WIDTH // POOL_GROUPS
HGRN_EXPAND = 128
HGRN_HEADS = BRANCH_WIDTH // HGRN_EXPAND
HGRN_HEAD_DIM = BRANCH_WIDTH // HGRN_HEADS
N_BRANCH = 3
EPS = 1e-6
IN_SIZES = (BRANCH_WIDTH,) * 10 + (D_MODEL,) * N_BRANCH
IN_COLS = sum(IN_SIZES)

kernel_name = "hybrid_stickbreak_pool_hgrn2_block"


def _rmsnorm(x, g):
    xf = x.astype(jnp.float32)
    return xf * lax.rsqrt(jnp.mean(xf * xf, axis=-1, keepdims=True) + EPS) * g.astype(jnp.float32)


def _stick_breaking(q, k, v):
    seq = q.shape[1]
    scale = 1.0 / math.sqrt(SB_HEAD_DIM)
    outs = []
    for blk in range(seq // Q_BLOCK):
        qs, qe = blk * Q_BLOCK, (blk + 1) * Q_BLOCK
        z = jnp.einsum('bqhd,bkhd->bhqk', q[:, qs:qe], k[:, :qe]) * scale
        qpos = jnp.arange(qs, qe)[:, None]
        kpos = jnp.arange(qe)[None, :]
        mask = kpos < qpos
        log_1mb = jnp.where(mask, jax.nn.log_sigmoid(-z), 0.0)
        cum = jnp.cumsum(log_1mb, axis=-1)
        rem = cum[..., -1:] - cum
        a = jnp.where(mask, jnp.exp(jax.nn.log_sigmoid(z) + rem), 0.0)
        outs.append(jnp.einsum('bhqk,bkhd->bqhd', a, v[:, :qe]))
    return jnp.concatenate(outs, axis=1)


def _multiscale_pool(u, w_grp, scale):
    bsz, seq, _ = u.shape
    c = jnp.cumsum(u, axis=1)
    c = jnp.concatenate([jnp.zeros_like(c[:, :1]), c], axis=1)
    cg = c.reshape(bsz, seq + 1, POOL_GROUPS, POOL_GROUP_DIM)
    ug = u.reshape(bsz, seq, POOL_GROUPS, POOL_GROUP_DIM)
    pos = jnp.arange(seq)
    diffs = []
    for g, w in enumerate(POOL_WINDOWS):
        start = jnp.maximum(pos + 1 - w, 0)
        cnt = (pos + 1 - start).astype(jnp.float32)
        mean = (cg[:, 1:, g] - cg[:, start, g]) / cnt[None, :, None]
        diffs.append(mean - ug[:, :, g])
    d = jnp.stack(diffs, axis=2)
    y = jnp.einsum('bsgc,gcd->bsgd', d, w_grp.astype(jnp.float32))
    return y.reshape(bsz, seq, BRANCH_WIDTH) * scale.astype(jnp.float32)


def _hgrn2(q, f_logit, i, lb):
    bsz, seq, _ = q.shape
    n_chunks = seq // CHUNK
    f = lb + (1.0 - lb) * jax.nn.sigmoid(f_logit)
    log_f = jnp.log(f)
    k = 1.0 - f

    def to_chunks(t):
        return t.reshape(bsz, n_chunks, CHUNK, HGRN_HEADS, HGRN_HEAD_DIM).transpose(1, 0, 3, 2, 4)

    causal = jnp.tril(jnp.ones((CHUNK, CHUNK), dtype=bool))[None, None, :, :, None]

    def step(state, inp):
        qc, kc, vc, lfc = inp
        b = jnp.cumsum(lfc, axis=2)
        diff = b[:, :, :, None, :] - b[:, :, None, :, :]
        decay = jnp.exp(jnp.where(causal, diff, -jnp.inf))
        attn = jnp.einsum('bhrsd,bhsd->bhrs', qc[:, :, :, None, :] * decay, kc)
        o = (jnp.einsum('bhrs,bhsv->bhrv', attn, vc)
             + jnp.einsum('bhrd,bhdv->bhrv', qc * jnp.exp(b), state))
        b_last = b[:, :, -1:, :]
        new_state = (jnp.exp(b_last[:, :, 0, :])[..., None] * state
                     + jnp.einsum('bhsd,bhsv->bhdv', kc * jnp.exp(b_last - b), vc))
        return new_state, o

    state0 = jnp.zeros((bsz, HGRN_HEADS, HGRN_HEAD_DIM, HGRN_HEAD_DIM), jnp.float32)
    _, o = lax.scan(step, state0, (to_chunks(q), to_chunks(k), to_chunks(i), to_chunks(log_f)))
    return o.transpose(1, 0, 3, 2, 4).reshape(bsz, seq, HGRN_HEADS, HGRN_HEAD_DIM)


def setup_inputs(seed: int = 0) -> dict:
    key = jax.random.key(seed)
    ks = jax.random.split(key, 12)
    f32 = jnp.float32
    x = jax.random.normal(ks[0], (BATCH, SEQ, D_MODEL), f32)
    norm_g = 1.0 + 0.02 * jax.random.normal(ks[1], (DEPTH, D_MODEL), f32)
    w_in = jax.random.normal(ks[2], (DEPTH, D_MODEL, IN_COLS), f32) * D_MODEL ** -0.5
    pool_w = jax.random.normal(ks[3], (DEPTH, POOL_GROUPS, POOL_GROUP_DIM, POOL_GROUP_DIM), f32) * POOL_GROUP_DIM ** -0.5
    pool_scale = 1.0 + 0.02 * jax.random.normal(ks[4], (DEPTH, BRANCH_WIDTH), f32)
    hgrn_lb = 1.0 + 0.1 * jax.random.normal(ks[5], (DEPTH, BRANCH_WIDTH), f32)
    hgrn_norm_g = 1.0 + 0.02 * jax.random.normal(ks[6], (DEPTH, BRANCH_WIDTH), f32)
    w_branch = jax.random.normal(ks[7], (DEPTH, N_BRANCH, BRANCH_WIDTH, D_MODEL), f32) * BRANCH_WIDTH ** -0.5
    w_out = jax.random.normal(ks[8], (DEPTH, D_MODEL, D_MODEL), f32) * D_MODEL ** -0.5
    final_g = 1.0 + 0.02 * jax.random.normal(ks[9], (D_MODEL,), f32)
    return {"x": x, "norm_g": norm_g, "w_in": w_in, "pool_w": pool_w,
            "pool_scale": pool_scale, "hgrn_lb": hgrn_lb, "hgrn_norm_g": hgrn_norm_g,
            "w_branch": w_branch, "w_out": w_out, "final_g": final_g}


def reference(x, norm_g, w_in, pool_w, pool_scale, hgrn_lb, hgrn_norm_g, w_branch, w_out, final_g):
    bsz, seq, _ = x.shape
    h_res = x.astype(jnp.float32)
    lb_all = jnp.cumsum(jax.nn.softmax(hgrn_lb.astype(jnp.float32), axis=0), axis=0)
    lb_all = lb_all - lb_all[:1]
    split_idx = list(np.cumsum(IN_SIZES)[:-1])
    for layer in range(DEPTH):
        h = _rmsnorm(h_res, norm_g[layer])
        proj = h @ w_in[layer].astype(jnp.float32)
        (sb_q, sb_k, sb_v, sb_z, pool_u, pool_z, hg_q, hg_f, hg_i, hg_z,
         gate_a, gate_b, gate_c) = jnp.split(proj, split_idx, axis=-1)

        hs = (bsz, seq, SB_HEADS, SB_HEAD_DIM)
        o_a = _stick_breaking(sb_q.reshape(hs), sb_k.reshape(hs), sb_v.reshape(hs))
        o_a = o_a.reshape(bsz, seq, BRANCH_WIDTH) * jax.nn.silu(sb_z)

        o_b = _multiscale_pool(pool_u, pool_w[layer], pool_scale[layer]) * jax.nn.silu(pool_z)

        o_c = _hgrn2(hg_q, hg_f, hg_i, lb_all[layer])
        o_c = _rmsnorm(o_c, hgrn_norm_g[layer].reshape(HGRN_HEADS, HGRN_HEAD_DIM))
        o_c = o_c.reshape(bsz, seq, BRANCH_WIDTH) * jax.nn.silu(hg_z)

        wb = w_branch[layer].astype(jnp.float32)
        merged = (jax.nn.sigmoid(gate_a) * (o_a @ wb[0])
                  + jax.nn.sigmoid(gate_b) * (o_b @ wb[1])
                  + jax.nn.sigmoid(gate_c) * (o_c @ wb[2]))
        h_res = h_res + merged @ w_out[layer].astype(jnp.float32)
    return _rmsnorm(h_res, final_g).astype(x.dtype)
```

```python
import functools
import math

import jax
import jax.numpy as jnp
from jax import lax
from jax.experimental import pallas as pl
from jax.experimental.pallas import tpu as pltpu

D_MODEL = 1024
DEPTH = 2
BRANCH_WIDTH = D_MODEL // 2
SB_HEAD_DIM = 64
POOL_WINDOWS = (2, 4, 8, 16)
POOL_HALO = 16
HGRN_HEADS = 4
EPS = 1e-6
IN_COLS = 10 * BRANCH_WIDTH + 3 * D_MODEL

LANES = 128
SUBLANES = 8
VMEM_LIMIT = 48 * 1024 * 1024

CB_SB_Q, CB_SB_K, CB_SB_V, CB_SB_Z = 0, 4, 8, 12
CB_POOL_U, CB_POOL_Z = 16, 20
CB_HG_Q, CB_HG_F, CB_HG_I, CB_HG_Z = 24, 28, 32, 36
GATE_BLOCK0 = 5

PROJ_TM, PROJ_TN = 1024, 1024
SB_T = 256
POOL_T = 256
HG_T = 128
MERGE_TM = 512

F32 = jnp.float32
BF16 = jnp.bfloat16


def _params(*sem):
    return pltpu.CompilerParams(dimension_semantics=sem, vmem_limit_bytes=VMEM_LIMIT)


def _dot(a, b):
    return jnp.dot(a, b, preferred_element_type=F32)


def _dot_nt(a, b):
    return lax.dot_general(a, b, (((1,), (1,)), ((), ())), preferred_element_type=F32)


def _dot_tn(a, b):
    return lax.dot_general(a, b, (((0,), (0,)), ((), ())), preferred_element_type=F32)


def _split3(x):
    hi = x.astype(BF16)
    r = x - hi.astype(F32)
    mid = r.astype(BF16)
    lo = (r - mid.astype(F32)).astype(BF16)
    return hi, mid, lo


def _silu(x):
    return x * jax.nn.sigmoid(x)


def _proj_kernel(x_ref, g_ref, w_ref, o_ref, h_scr):
    @pl.when(pl.program_id(1) == 0)
    def _():
        x = x_ref[...]
        ms = jnp.mean(x * x, axis=-1, keepdims=True)
        h_scr[...] = (x * lax.rsqrt(ms + EPS) * g_ref[...]).astype(BF16)

    o_ref[...] = _dot(h_scr[...], w_ref[...]).astype(o_ref.dtype)


def _proj(h_res, g, w_bf):
    n = h_res.shape[0]
    return pl.pallas_call(
        _proj_kernel,
        grid=(n // PROJ_TM, IN_COLS // PROJ_TN),
        in_specs=[
            pl.BlockSpec((PROJ_TM, D_MODEL), lambda i, j: (i, 0)),
            pl.BlockSpec((1, D_MODEL), lambda i, j: (0, 0)),
            pl.BlockSpec((D_MODEL, PROJ_TN), lambda i, j: (0, j)),
        ],
        out_specs=pl.BlockSpec((PROJ_TM, PROJ_TN), lambda i, j: (i, j)),
        out_shape=jax.ShapeDtypeStruct((n, IN_COLS), BF16),
        scratch_shapes=[pltpu.VMEM((PROJ_TM, D_MODEL), BF16)],
        compiler_params=_params("parallel", "arbitrary"),
        name="proj",
    )(h_res, g, w_bf)


def _sb_kernel(q_ref, k_ref, v_ref, z_ref, o_ref):
    i = pl.program_id(2)
    t = SB_T
    lane = lax.broadcasted_iota(jnp.int32, (1, LANES), 1)
    rows = lax.broadcasted_iota(jnp.int32, (t, t), 0)
    cols = lax.broadcasted_iota(jnp.int32, (t, t), 1)
    causal = cols < rows
    neg_upper = jnp.where(rows > cols, -1.0, 0.0).astype(BF16)

    q = q_ref[...] * jnp.asarray(1.0 / math.sqrt(SB_HEAD_DIM), BF16)

    def tile(qh, j, carry, masked):
        acc, c = carry
        start = pl.multiple_of(j * t, t)
        k = k_ref[pl.ds(start, t), :]
        v = v_ref[pl.ds(start, t), :]
        z = _dot_nt(qh, k)
        tt = jnp.log(1.0 + jnp.exp(-jnp.abs(z)))
        nl = jnp.maximum(z, 0.0) + tt
        lz = jnp.minimum(z, 0.0) - tt
        if masked:
            nl = jnp.where(causal, nl, 0.0)
        hi = nl.astype(BF16)
        lo = (nl - hi.astype(F32)).astype(BF16)
        rem = _dot(hi, neg_upper) + _dot(lo, neg_upper)
        a = jnp.exp(lz + rem + c)
        if masked:
            a = jnp.where(causal, a, 0.0)
        acc = acc + _dot(a.astype(BF16), v)
        c = c - jnp.sum(nl, axis=-1, keepdims=True)
        return acc, c

    outs = []
    for h in range(2):
        in_head = (lane >= h * SB_HEAD_DIM) & (lane < (h + 1) * SB_HEAD_DIM)
        qh = jnp.where(in_head, q, jnp.zeros_like(q))
        carry = (jnp.zeros((t, LANES), F32), jnp.zeros((t, 1), F32))
        carry = tile(qh, i, carry, True)
        carry = lax.fori_loop(
            0, i, lambda jj, cr: tile(qh, i - 1 - jj, cr, False), carry)
        outs.append(carry[0])
    o = jnp.where(lane < SB_HEAD_DIM, outs[0], outs[1])
    o_ref[...] = (o * _silu(z_ref[...].astype(F32))).astype(o_ref.dtype)


def _sb(proj, bsz, seq):
    nq = seq // SB_T
    return pl.pallas_call(
        _sb_kernel,
        grid=(bsz, BRANCH_WIDTH // LANES, nq),
        in_specs=[
            pl.BlockSpec((SB_T, LANES), lambda b, p, i: (b * nq + i, CB_SB_Q + p)),
            pl.BlockSpec((seq, LANES), lambda b, p, i: (b, CB_SB_K + p)),
            pl.BlockSpec((seq, LANES), lambda b, p, i: (b, CB_SB_V + p)),
            pl.BlockSpec((SB_T, LANES), lambda b, p, i: (b * nq + i, CB_SB_Z + p)),
        ],
        out_specs=pl.BlockSpec((SB_T, LANES), lambda b, p, i: (b * nq + i, p)),
        out_shape=jax.ShapeDtypeStruct((bsz * seq, BRANCH_WIDTH), BF16),
        compiler_params=_params("parallel", "parallel", "arbitrary"),
        name="stickbreak",
    )(proj, proj, proj, proj)


def _pool_kernel(u_ref, z_ref, w_ref, sc_ref, o_ref, halo_scr):
    g = pl.program_id(1)
    ti = pl.program_id(2)
    t = POOL_T

    @pl.when(ti == 0)
    def _():
        halo_scr[...] = jnp.zeros_like(halo_scr)

    u = u_ref[...]
    halo = halo_scr[...]
    rows = lax.broadcasted_iota(jnp.int32, (t, t), 0)
    cols = lax.broadcasted_iota(jnp.int32, (t, t), 1)
    hrows = lax.broadcasted_iota(jnp.int32, (t, POOL_HALO), 0)
    hcols = lax.broadcasted_iota(jnp.int32, (t, POOL_HALO), 1) - POOL_HALO
    pos = ti * t + lax.broadcasted_iota(jnp.int32, (t, 1), 0)

    win = jnp.int32(POOL_WINDOWS[0])
    for gi, w in enumerate(POOL_WINDOWS):
        win = jnp.where(g == gi, jnp.int32(w), win)
    band = ((cols <= rows) & (cols > rows - win)).astype(BF16)
    hband = (hcols > hrows - win).astype(BF16)
    wsum = _dot(band, u) + _dot(hband, halo)
    cnt = jnp.minimum(pos + 1, win).astype(F32)
    d = wsum / cnt - u.astype(F32)
    y = _dot(d.astype(BF16), w_ref[0])
    o_ref[...] = (y * sc_ref[0] * _silu(z_ref[...].astype(F32))).astype(o_ref.dtype)
    halo_scr[...] = u[t - POOL_HALO:, :]


def _pool(proj, pool_w_bf, pool_scale, bsz, seq):
    nt = seq // POOL_T
    groups = len(POOL_WINDOWS)
    return pl.pallas_call(
        _pool_kernel,
        grid=(bsz, groups, nt),
        in_specs=[
            pl.BlockSpec((POOL_T, LANES), lambda b, g, i: (b * nt + i, CB_POOL_U + g)),
            pl.BlockSpec((POOL_T, LANES), lambda b, g, i: (b * nt + i, CB_POOL_Z + g)),
            pl.BlockSpec((1, LANES, LANES), lambda b, g, i: (g, 0, 0)),
            pl.BlockSpec((1, 1, LANES), lambda b, g, i: (g, 0, 0)),
        ],
        out_specs=pl.BlockSpec((POOL_T, LANES), lambda b, g, i: (b * nt + i, g)),
        out_shape=jax.ShapeDtypeStruct((bsz * seq, BRANCH_WIDTH), BF16),
        scratch_shapes=[pltpu.VMEM((POOL_HALO, LANES), BF16)],
        compiler_params=_params("parallel", "parallel", "arbitrary"),
        name="pool",
    )(proj, proj, pool_w_bf, pool_scale.reshape(groups, 1, LANES))


def _hgrn_kernel(q_ref, f_ref, i_ref, z_ref, lb_ref, g_ref, o_ref, st_scr):
    t = HG_T

    @pl.when(pl.program_id(2) == 0)
    def _():
        st_scr[...] = jnp.zeros_like(st_scr)

    lb = lb_ref[0]
    sig = jax.nn.sigmoid(f_ref[...].astype(F32))
    f = lb + (1.0 - lb) * sig
    kk = (1.0 - lb) * (1.0 - sig)
    logf = jnp.log(f)
    q = q_ref[...].astype(F32)
    v_bf = i_ref[...]
    v = v_bf.astype(F32)

    rows = lax.broadcasted_iota(jnp.int32, (t, t), 0)
    cols = lax.broadcasted_iota(jnp.int32, (t, t), 1)
    lower = (cols <= rows).astype(BF16)
    hi, mid, lo = _split3(logf)
    b = _dot(lower, hi) + _dot(lower, mid) + _dot(lower, lo)

    st = st_scr[...]
    o = _dot_nt((q * jnp.exp(b)).astype(BF16), st.astype(BF16))

    nb = t // SUBLANES
    q3 = q.reshape(nb, SUBLANES, LANES)
    b3 = b.reshape(nb, SUBLANES, LANES)
    k3 = kk.reshape(nb, SUBLANES, LANES)
    v3 = v.reshape(nb, SUBLANES, LANES)
    sub = lax.broadcasted_iota(jnp.int32, (1, SUBLANES, 1), 1)
    o3 = jnp.zeros((nb, SUBLANES, LANES), F32)
    for s in range(SUBLANES):
        e = jnp.exp(jnp.where(sub >= s, b3 - b3[:, s:s + 1, :], -1e30))
        a = jnp.sum(q3 * e * k3[:, s:s + 1, :], axis=-1, keepdims=True)
        o3 = o3 + a * v3[:, s:s + 1, :]
    o = o + o3.reshape(t, LANES)

    row1 = lax.broadcasted_iota(jnp.int32, (t, 1), 0)
    amat = jnp.zeros((t, t), F32)
    m = SUBLANES
    while m < t:
        blk = 2 * m
        edge = b.reshape(t // blk, blk, LANES)[:, m - 1:m, :]
        edge = jnp.broadcast_to(edge, (t // blk, blk, LANES)).reshape(t, LANES)
        dec = jnp.exp(-jnp.abs(b - edge))
        second = (row1 // m) % 2 == 1
        qt = jnp.where(second, q * dec, 0.0).astype(BF16)
        kt = jnp.where(second, 0.0, kk * dec).astype(BF16)
        p = _dot_nt(qt, kt)
        if blk < t:
            p = jnp.where((rows // blk) == (cols // blk), p, 0.0)
        amat = amat + p
        m = blk
    o = o + _dot(amat.astype(BF16), v_bf)

    b_last = b[t - 1:t, :]
    kd = (kk * jnp.exp(b_last - b)).astype(BF16)
    st_scr[...] = st * jnp.exp(b_last) + _dot_tn(v_bf, kd)

    ms = jnp.mean(o * o, axis=-1, keepdims=True)
    o = o * lax.rsqrt(ms + EPS) * g_ref[0]
    o_ref[...] = (o * _silu(z_ref[...].astype(F32))).astype(o_ref.dtype)


def _hgrn(proj, lb, norm_g, bsz, seq):
    nt = seq // HG_T
    blk = lambda cb: pl.BlockSpec((HG_T, LANES), lambda b, h, i: (b * nt + i, cb + h))
    vec = pl.BlockSpec((1, 1, LANES), lambda b, h, i: (h, 0, 0))
    return pl.pallas_call(
        _hgrn_kernel,
        grid=(bsz, HGRN_HEADS, nt),
        in_specs=[blk(CB_HG_Q), blk(CB_HG_F), blk(CB_HG_I), blk(CB_HG_Z), vec, vec],
        out_specs=pl.BlockSpec((HG_T, LANES), lambda b, h, i: (b * nt + i, h)),
        out_shape=jax.ShapeDtypeStruct((bsz * seq, BRANCH_WIDTH), BF16),
        scratch_shapes=[pltpu.VMEM((LANES, LANES), F32)],
        compiler_params=_params("parallel", "parallel", "arbitrary"),
        name="hgrn2",
    )(proj, proj, proj, proj,
      lb.reshape(HGRN_HEADS, 1, LANES), norm_g.reshape(HGRN_HEADS, 1, LANES))


def _merge_kernel(oa_ref, ob_ref, oc_ref, ga_ref, gb_ref, gc_ref, wb_ref, wo_ref,
                  res_ref, fg_ref, *out_refs, final):
    merged = (jax.nn.sigmoid(ga_ref[...].astype(F32)) * _dot(oa_ref[...], wb_ref[0])
              + jax.nn.sigmoid(gb_ref[...].astype(F32)) * _dot(ob_ref[...], wb_ref[1])
              + jax.nn.sigmoid(gc_ref[...].astype(F32)) * _dot(oc_ref[...], wb_ref[2]))
    h = res_ref[...] + _dot(merged.astype(BF16), wo_ref[...])
    if final:
        ms = jnp.mean(h * h, axis=-1, keepdims=True)
        h = h * lax.rsqrt(ms + EPS) * fg_ref[...]
    out_refs[0][...] = h


def _merge(o_a, o_b, o_c, proj, wb_bf, wo_bf, h_res, final_g, final):
    n = h_res.shape[0]
    row = lambda width: pl.BlockSpec((MERGE_TM, width), lambda i: (i, 0))
    gate = lambda k: pl.BlockSpec((MERGE_TM, D_MODEL), lambda i: (i, GATE_BLOCK0 + k))
    return pl.pallas_call(
        functools.partial(_merge_kernel, final=final),
        grid=(n // MERGE_TM,),
        in_specs=[
            row(BRANCH_WIDTH), row(BRANCH_WIDTH), row(BRANCH_WIDTH),
            gate(0), gate(1), gate(2),
            pl.BlockSpec((3, BRANCH_WIDTH, D_MODEL), lambda i: (0, 0, 0)),
            pl.BlockSpec((D_MODEL, D_MODEL), lambda i: (0, 0)),
            row(D_MODEL),
            pl.BlockSpec((1, D_MODEL), lambda i: (0, 0)),
        ],
        out_specs=row(D_MODEL),
        out_shape=jax.ShapeDtypeStruct((n, D_MODEL), F32),
        compiler_params=_params("parallel"),
        name="merge_final" if final else "merge",
    )(o_a, o_b, o_c, proj, proj, proj, wb_bf, wo_bf, h_res, final_g)


def kernel(x, norm_g, w_in, pool_w, pool_scale, hgrn_lb, hgrn_norm_g, w_branch, w_out, final_g):
    bsz, seq, _ = x.shape
    h_res = x.astype(F32).reshape(bsz * seq, D_MODEL)
    lb_all = jnp.cumsum(jax.nn.softmax(hgrn_lb.astype(F32), axis=0), axis=0)
    lb_all = lb_all - lb_all[:1]
    w_in_bf = w_in.astype(BF16)
    pool_w_bf = pool_w.astype(BF16)
    wb_bf = w_branch.astype(BF16)
    wo_bf = w_out.astype(BF16)
    fg = final_g.astype(F32).reshape(1, D_MODEL)
    for layer in range(DEPTH):
        proj = _proj(h_res, norm_g[layer].astype(F32).reshape(1, D_MODEL), w_in_bf[layer])
        o_a = _sb(proj, bsz, seq)
        o_b = _pool(proj, pool_w_bf[layer], pool_scale[layer].astype(F32), bsz, seq)
        o_c = _hgrn(proj, lb_all[layer], hgrn_norm_g[layer].astype(F32), bsz, seq)
        h_res = _merge(o_a, o_b, o_c, proj, wb_bf[layer], wo_bf[layer], h_res, fg,
                       final=(layer == DEPTH - 1))
    return h_res.reshape(bsz, seq, D_MODEL).astype(x.dtype)
```

```python
import functools
import math

import jax
import jax.numpy as jnp
from jax import lax
from jax.experimental import pallas as pl
from jax.experimental.pallas import tpu as pltpu

D_MODEL = 1024
DEPTH = 2
BRANCH_WIDTH = D_MODEL // 2
SB_HEAD_DIM = 64
POOL_WINDOWS = (2, 4, 8, 16)
POOL_HALO = 16
HGRN_HEADS = 4
EPS = 1e-6
IN_COLS = 10 * BRANCH_WIDTH + 3 * D_MODEL

LANES = 128
SUBLANES = 8
VMEM_LIMIT = 48 * 1024 * 1024

CB_SB_Q, CB_SB_K, CB_SB_V, CB_SB_Z = 0, 4, 8, 12
CB_POOL_U, CB_POOL_Z = 16, 20
CB_HG_Q, CB_HG_F, CB_HG_I, CB_HG_Z = 24, 28, 32, 36
GATE_BLOCK0 = 5

PROJ_TM, PROJ_TN = 1024, 1024
SB_T = 256
HG_T = 128
MERGE_TM = 512

F32 = jnp.float32
BF16 = jnp.bfloat16


def _params(*sem):
    return pltpu.CompilerParams(dimension_semantics=sem, vmem_limit_bytes=VMEM_LIMIT)


def _dot(a, b):
    return jnp.dot(a, b, preferred_element_type=F32)


def _dot_nt(a, b):
    return lax.dot_general(a, b, (((1,), (1,)), ((), ())), preferred_element_type=F32)


def _dot_tn(a, b):
    return lax.dot_general(a, b, (((0,), (0,)), ((), ())), preferred_element_type=F32)


def _split3(x):
    hi = x.astype(BF16)
    r = x - hi.astype(F32)
    mid = r.astype(BF16)
    lo = (r - mid.astype(F32)).astype(BF16)
    return hi, mid, lo


def _silu(x):
    return x * jax.nn.sigmoid(x)


def _proj_kernel(x_ref, g_ref, w_ref, cs_ref, o_ref, h_scr):
    @pl.when(pl.program_id(1) == 0)
    def _():
        x = x_ref[...]
        ms = jnp.mean(x * x, axis=-1, keepdims=True)
        h_scr[...] = (x * lax.rsqrt(ms + EPS) * g_ref[...]).astype(BF16)

    o_ref[...] = (_dot(h_scr[...], w_ref[...]) * cs_ref[...]).astype(o_ref.dtype)


def _proj(h_res, g, w_bf):
    n = h_res.shape[0]
    col = lax.broadcasted_iota(jnp.int32, (1, IN_COLS), 1)
    col_scale = jnp.where(col < BRANCH_WIDTH, math.log2(math.e) / math.sqrt(SB_HEAD_DIM), 1.0)
    return pl.pallas_call(
        _proj_kernel,
        grid=(n // PROJ_TM, IN_COLS // PROJ_TN),
        in_specs=[
            pl.BlockSpec((PROJ_TM, D_MODEL), lambda i, j: (i, 0)),
            pl.BlockSpec((1, D_MODEL), lambda i, j: (0, 0)),
            pl.BlockSpec((D_MODEL, PROJ_TN), lambda i, j: (0, j)),
            pl.BlockSpec((1, PROJ_TN), lambda i, j: (0, j)),
        ],
        out_specs=pl.BlockSpec((PROJ_TM, PROJ_TN), lambda i, j: (i, j)),
        out_shape=jax.ShapeDtypeStruct((n, IN_COLS), BF16),
        scratch_shapes=[pltpu.VMEM((PROJ_TM, D_MODEL), BF16)],
        compiler_params=_params("parallel", "arbitrary"),
        name="proj",
    )(h_res, g, w_bf, col_scale.astype(F32))


def _sb_kernel(q_ref, k_ref, v_ref, z_ref, o_ref):
    i = pl.program_id(2)
    t = SB_T
    lane = lax.broadcasted_iota(jnp.int32, (1, LANES), 1)
    rows = lax.broadcasted_iota(jnp.int32, (t, t), 0)
    cols = lax.broadcasted_iota(jnp.int32, (t, t), 1)
    causal = cols < rows
    neg_upper = jnp.where(rows > cols, -1.0, 0.0).astype(BF16)

    q = q_ref[...]
    qa = jnp.where(lane < SB_HEAD_DIM, q, jnp.zeros_like(q))
    qb = jnp.where(lane < SB_HEAD_DIM, jnp.zeros_like(q), q)

    def ktile(j):
        return k_ref[pl.ds(pl.multiple_of(j * t, t), t), :]

    def scores(z, masked):
        tt = jnp.log2(1.0 + jnp.exp2(-jnp.abs(z)))
        nl = jnp.maximum(z, 0.0) + tt
        lz = z - nl
        if masked:
            nl = jnp.where(causal, nl, 0.0)
        hi = nl.astype(BF16)
        lo = (nl - hi.astype(F32)).astype(BF16)
        return lz, hi, lo, jnp.sum(nl, axis=-1, keepdims=True)

    def suffix(hi, lo):
        return _dot(hi, neg_upper) + _dot(lo, neg_upper)

    def weights(lz, rem, c, masked):
        a = jnp.exp2(lz + rem + c)
        if masked:
            a = jnp.where(causal, a, 0.0)
        return a.astype(BF16)

    def tile(j, carry, masked):
        za, acc_a, c_a, acc_b, c_b = carry
        k = ktile(j)
        v = v_ref[pl.ds(pl.multiple_of(j * t, t), t), :]
        zb = _dot_nt(qb, k)
        lz_a, hi_a, lo_a, rs_a = scores(za, masked)
        rem_a = suffix(hi_a, lo_a)
        lz_b, hi_b, lo_b, rs_b = scores(zb, masked)
        rem_b = suffix(hi_b, lo_b)
        w_a = weights(lz_a, rem_a, c_a, masked)
        acc_a = acc_a + _dot(w_a, v)
        za_next = _dot_nt(qa, ktile(jnp.maximum(j - 1, 0)))
        w_b = weights(lz_b, rem_b, c_b, masked)
        acc_b = acc_b + _dot(w_b, v)
        return za_next, acc_a, c_a - rs_a, acc_b, c_b - rs_b

    zacc = jnp.zeros((t, LANES), F32)
    zc = jnp.zeros((t, 1), F32)
    carry = tile(i, (_dot_nt(qa, ktile(i)), zacc, zc, zacc, zc), True)
    carry = lax.fori_loop(0, i, lambda jj, cr: tile(i - 1 - jj, cr, False), carry)
    o = jnp.where(lane < SB_HEAD_DIM, carry[1], carry[3])
    o_ref[...] = (o * _silu(z_ref[...].astype(F32))).astype(o_ref.dtype)


def _sb(proj, bsz, seq):
    nq = seq // SB_T
    return pl.pallas_call(
        _sb_kernel,
        grid=(bsz, BRANCH_WIDTH // LANES, nq),
        in_specs=[
            pl.BlockSpec((SB_T, LANES), lambda b, p, i: (b * nq + i, CB_SB_Q + p)),
            pl.BlockSpec((seq, LANES), lambda b, p, i: (b, CB_SB_K + p)),
            pl.BlockSpec((seq, LANES), lambda b, p, i: (b, CB_SB_V + p)),
            pl.BlockSpec((SB_T, LANES), lambda b, p, i: (b * nq + i, CB_SB_Z + p)),
        ],
        out_specs=pl.BlockSpec((SB_T, LANES), lambda b, p, i: (b * nq + i, p)),
        out_shape=jax.ShapeDtypeStruct((bsz * seq, BRANCH_WIDTH), BF16),
        compiler_params=_params("parallel", "parallel", "arbitrary"),
        name="stickbreak",
    )(proj, proj, proj, proj)


def _pool_tile(u_ref, z_ref, w_ref, sc_ref, halo_scr, first_pos):
    tm = u_ref.shape[0]
    pos = first_pos + lax.broadcasted_iota(jnp.int32, (tm, 1), 0)
    outs = []
    for g, w in enumerate(POOL_WINDOWS):
        cols = slice(g * LANES, (g + 1) * LANES)
        u = u_ref[:, cols].astype(F32)
        s = jnp.concatenate([halo_scr[:, cols], u], axis=0)
        k = 1
        while k < w:
            s = s + pltpu.roll(s, k, axis=0)
            k *= 2
        cnt = jnp.minimum(pos + 1, w).astype(F32)
        d = s[POOL_HALO:, :] / cnt - u
        outs.append(_dot(d.astype(BF16), w_ref[g]))
        halo_scr[:, cols] = u[tm - POOL_HALO:, :]
    y = jnp.concatenate(outs, axis=1)
    return (y * sc_ref[...] * _silu(z_ref[...].astype(F32))).astype(BF16)


def _hgrn_kernel(q_ref, f_ref, i_ref, z_ref, lb_ref, g_ref, o_ref, st_scr):
    t = HG_T

    @pl.when(pl.program_id(2) == 0)
    def _():
        st_scr[...] = jnp.zeros_like(st_scr)

    lb = lb_ref[0]
    sig = jax.nn.sigmoid(f_ref[...].astype(F32))
    f = lb + (1.0 - lb) * sig
    kk = (1.0 - lb) * (1.0 - sig)
    logf = jnp.log(f)
    q = q_ref[...].astype(F32)
    v_bf = i_ref[...]
    v = v_bf.astype(F32)

    rows = lax.broadcasted_iota(jnp.int32, (t, t), 0)
    cols = lax.broadcasted_iota(jnp.int32, (t, t), 1)
    lower = (cols <= rows).astype(BF16)
    hi, mid, lo = _split3(logf)
    b = _dot(lower, hi) + _dot(lower, mid) + _dot(lower, lo)

    st = st_scr[...]
    o = _dot_nt((q * jnp.exp(b)).astype(BF16), st.astype(BF16))

    nb = t // SUBLANES
    q3 = q.reshape(nb, SUBLANES, LANES)
    b3 = b.reshape(nb, SUBLANES, LANES)
    k3 = kk.reshape(nb, SUBLANES, LANES)
    v3 = v.reshape(nb, SUBLANES, LANES)
    sub = lax.broadcasted_iota(jnp.int32, (1, SUBLANES, 1), 1)
    o3 = jnp.zeros((nb, SUBLANES, LANES), F32)
    for s in range(SUBLANES):
        e = jnp.exp(jnp.where(sub >= s, b3 - b3[:, s:s + 1, :], -1e30))
        a = jnp.sum(q3 * e * k3[:, s:s + 1, :], axis=-1, keepdims=True)
        o3 = o3 + a * v3[:, s:s + 1, :]
    o = o + o3.reshape(t, LANES)

    row1 = lax.broadcasted_iota(jnp.int32, (t, 1), 0)
    amat = jnp.zeros((t, t), F32)
    m = SUBLANES
    while m < t:
        blk = 2 * m
        edge = b.reshape(t // blk, blk, LANES)[:, m - 1:m, :]
        edge = jnp.broadcast_to(edge, (t // blk, blk, LANES)).reshape(t, LANES)
        dec = jnp.exp(-jnp.abs(b - edge))
        second = (row1 // m) % 2 == 1
        qt = jnp.where(second, q * dec, 0.0).astype(BF16)
        kt = jnp.where(second, 0.0, kk * dec).astype(BF16)
        p = _dot_nt(qt, kt)
        if blk < t:
            p = jnp.where((rows // blk) == (cols // blk), p, 0.0)
        amat = amat + p
        m = blk
    o = o + _dot(amat.astype(BF16), v_bf)

    b_last = b[t - 1:t, :]
    kd = (kk * jnp.exp(b_last - b)).astype(BF16)
    st_scr[...] = st * jnp.exp(b_last) + _dot_tn(v_bf, kd)

    ms = jnp.mean(o * o, axis=-1, keepdims=True)
    o = o * lax.rsqrt(ms + EPS) * g_ref[0]
    o_ref[...] = (o * _silu(z_ref[...].astype(F32))).astype(o_ref.dtype)


def _hgrn(proj, lb, norm_g, bsz, seq):
    nt = seq // HG_T
    blk = lambda cb: pl.BlockSpec((HG_T, LANES), lambda b, h, i: (b * nt + i, cb + h))
    vec = pl.BlockSpec((1, 1, LANES), lambda b, h, i: (h, 0, 0))
    return pl.pallas_call(
        _hgrn_kernel,
        grid=(bsz, HGRN_HEADS, nt),
        in_specs=[blk(CB_HG_Q), blk(CB_HG_F), blk(CB_HG_I), blk(CB_HG_Z), vec, vec],
        out_specs=pl.BlockSpec((HG_T, LANES), lambda b, h, i: (b * nt + i, h)),
        out_shape=jax.ShapeDtypeStruct((bsz * seq, BRANCH_WIDTH), BF16),
        scratch_shapes=[pltpu.VMEM((LANES, LANES), F32)],
        compiler_params=_params("parallel", "parallel", "arbitrary"),
        name="hgrn2",
    )(proj, proj, proj, proj,
      lb.reshape(HGRN_HEADS, 1, LANES), norm_g.reshape(HGRN_HEADS, 1, LANES))


def _merge_kernel(oa_ref, oc_ref, pu_ref, pz_ref, ga_ref, gb_ref, gc_ref, pw_ref, ps_ref,
                  wb_ref, wo_ref, res_ref, fg_ref, out_ref, halo_scr, *, final, tiles_per_seq):
    ti = pl.program_id(0) % tiles_per_seq

    @pl.when(ti == 0)
    def _():
        halo_scr[...] = jnp.zeros_like(halo_scr)

    o_b = _pool_tile(pu_ref, pz_ref, pw_ref, ps_ref, halo_scr, ti * MERGE_TM)
    merged = (jax.nn.sigmoid(ga_ref[...].astype(F32)) * _dot(oa_ref[...], wb_ref[0])
              + jax.nn.sigmoid(gb_ref[...].astype(F32)) * _dot(o_b, wb_ref[1])
              + jax.nn.sigmoid(gc_ref[...].astype(F32)) * _dot(oc_ref[...], wb_ref[2]))
    h = res_ref[...] + _dot(merged.astype(BF16), wo_ref[...])
    if final:
        ms = jnp.mean(h * h, axis=-1, keepdims=True)
        h = h * lax.rsqrt(ms + EPS) * fg_ref[...]
    out_ref[...] = h


def _merge(o_a, o_c, proj, pool_w_bf, pool_scale, wb_bf, wo_bf, h_res, final_g, seq, final):
    n = h_res.shape[0]
    groups = len(POOL_WINDOWS)
    row = lambda width, blk=0: pl.BlockSpec((MERGE_TM, width), lambda i: (i, blk))
    whole = lambda *shape: pl.BlockSpec(shape, lambda i: (0,) * len(shape))
    return pl.pallas_call(
        functools.partial(_merge_kernel, final=final, tiles_per_seq=seq // MERGE_TM),
        grid=(n // MERGE_TM,),
        in_specs=[
            row(BRANCH_WIDTH), row(BRANCH_WIDTH),
            row(BRANCH_WIDTH, CB_POOL_U // groups), row(BRANCH_WIDTH, CB_POOL_Z // groups),
            row(D_MODEL, GATE_BLOCK0), row(D_MODEL, GATE_BLOCK0 + 1), row(D_MODEL, GATE_BLOCK0 + 2),
            whole(groups, LANES, LANES), whole(1, BRANCH_WIDTH),
            whole(3, BRANCH_WIDTH, D_MODEL), whole(D_MODEL, D_MODEL),
            row(D_MODEL), whole(1, D_MODEL),
        ],
        out_specs=row(D_MODEL),
        out_shape=jax.ShapeDtypeStruct((n, D_MODEL), F32),
        scratch_shapes=[pltpu.VMEM((POOL_HALO, BRANCH_WIDTH), F32)],
        compiler_params=_params("arbitrary"),
        name="merge_final" if final else "merge",
    )(o_a, o_c, proj, proj, proj, proj, proj, pool_w_bf, pool_scale.reshape(1, BRANCH_WIDTH),
      wb_bf, wo_bf, h_res, final_g)


def kernel(x, norm_g, w_in, pool_w, pool_scale, hgrn_lb, hgrn_norm_g, w_branch, w_out, final_g):
    bsz, seq, _ = x.shape
    h_res = x.astype(F32).reshape(bsz * seq, D_MODEL)
    lb_all = jnp.cumsum(jax.nn.softmax(hgrn_lb.astype(F32), axis=0), axis=0)
    lb_all = lb_all - lb_all[:1]
    w_in_bf = w_in.astype(BF16)
    pool_w_bf = pool_w.astype(BF16)
    wb_bf = w_branch.astype(BF16)
    wo_bf = w_out.astype(BF16)
    fg = final_g.astype(F32).reshape(1, D_MODEL)
    for layer in range(DEPTH):
        proj = _proj(h_res, norm_g[layer].astype(F32).reshape(1, D_MODEL), w_in_bf[layer])
        o_a = _sb(proj, bsz, seq)
        o_c = _hgrn(proj, lb_all[layer], hgrn_norm_g[layer].astype(F32), bsz, seq)
        h_res = _merge(o_a, o_c, proj, pool_w_bf[layer], pool_scale[layer].astype(F32),
                       wb_bf[layer], wo_bf[layer], h_res, fg, seq, final=(layer == DEPTH - 1))
    return h_res.reshape(bsz, seq, D_MODEL).astype(x.dtype)
```

```python
import functools
import math

import jax
import jax.numpy as jnp
from jax import lax
from jax.experimental import pallas as pl
from jax.experimental.pallas import tpu as pltpu

D_MODEL = 1024
DEPTH = 2
BRANCH_WIDTH = D_MODEL // 2
SB_HEAD_DIM = 64
POOL_WINDOWS = (2, 4, 8, 16)
POOL_HALO = 16
HGRN_HEADS = 4
EPS = 1e-6
IN_COLS = 10 * BRANCH_WIDTH + 3 * D_MODEL

LANES = 128
SUBLANES = 8
VMEM_LIMIT = 48 * 1024 * 1024

CB_SB_Q, CB_SB_K, CB_SB_V, CB_SB_Z = 0, 4, 8, 12
CB_POOL_U, CB_POOL_Z = 16, 20
CB_HG_Q, CB_HG_F, CB_HG_I, CB_HG_Z = 24, 28, 32, 36
GATE_BLOCK0 = 5

PROJ_TM, PROJ_TN = 1024, 1024
SB_T = 256
SB_DEAD_LOG2 = -152.0
HG_T = 128
MERGE_TM = 512

F32 = jnp.float32
BF16 = jnp.bfloat16


def _params(*sem):
    return pltpu.CompilerParams(dimension_semantics=sem, vmem_limit_bytes=VMEM_LIMIT)


def _dot(a, b):
    return jnp.dot(a, b, preferred_element_type=F32)


def _dot_nt(a, b):
    return lax.dot_general(a, b, (((1,), (1,)), ((), ())), preferred_element_type=F32)


def _dot_tn(a, b):
    return lax.dot_general(a, b, (((0,), (0,)), ((), ())), preferred_element_type=F32)


def _split3(x):
    hi = x.astype(BF16)
    r = x - hi.astype(F32)
    mid = r.astype(BF16)
    lo = (r - mid.astype(F32)).astype(BF16)
    return hi, mid, lo


def _silu(x):
    return x * jax.nn.sigmoid(x)


def _proj_kernel(x_ref, g_ref, w_ref, cs_ref, o_ref, h_scr):
    @pl.when(pl.program_id(1) == 0)
    def _():
        x = x_ref[...]
        ms = jnp.mean(x * x, axis=-1, keepdims=True)
        h_scr[...] = (x * lax.rsqrt(ms + EPS) * g_ref[...]).astype(BF16)

    o_ref[...] = (_dot(h_scr[...], w_ref[...]) * cs_ref[...]).astype(o_ref.dtype)


def _proj(h_res, g, w_bf):
    n = h_res.shape[0]
    col = lax.broadcasted_iota(jnp.int32, (1, IN_COLS), 1)
    col_scale = jnp.where(col < BRANCH_WIDTH, math.log2(math.e) / math.sqrt(SB_HEAD_DIM), 1.0)
    return pl.pallas_call(
        _proj_kernel,
        grid=(n // PROJ_TM, IN_COLS // PROJ_TN),
        in_specs=[
            pl.BlockSpec((PROJ_TM, D_MODEL), lambda i, j: (i, 0)),
            pl.BlockSpec((1, D_MODEL), lambda i, j: (0, 0)),
            pl.BlockSpec((D_MODEL, PROJ_TN), lambda i, j: (0, j)),
            pl.BlockSpec((1, PROJ_TN), lambda i, j: (0, j)),
        ],
        out_specs=pl.BlockSpec((PROJ_TM, PROJ_TN), lambda i, j: (i, j)),
        out_shape=jax.ShapeDtypeStruct((n, IN_COLS), BF16),
        scratch_shapes=[pltpu.VMEM((PROJ_TM, D_MODEL), BF16)],
        compiler_params=_params("parallel", "arbitrary"),
        name="proj",
    )(h_res, g, w_bf, col_scale.astype(F32))


def _sb_kernel(q_ref, k_ref, v_ref, z_ref, o_ref):
    i = pl.program_id(2)
    t = SB_T
    lane = lax.broadcasted_iota(jnp.int32, (1, LANES), 1)
    rows = lax.broadcasted_iota(jnp.int32, (t, t), 0)
    cols = lax.broadcasted_iota(jnp.int32, (t, t), 1)
    causal = cols < rows
    neg_upper = jnp.where(rows > cols, -1.0, 0.0).astype(BF16)

    q = q_ref[...]
    qa = jnp.where(lane < SB_HEAD_DIM, q, jnp.zeros_like(q))
    qb = jnp.where(lane < SB_HEAD_DIM, jnp.zeros_like(q), q)

    def ktile(j):
        return k_ref[pl.ds(pl.multiple_of(j * t, t), t), :]

    def scores(z, masked):
        tt = jnp.log2(1.0 + jnp.exp2(-jnp.abs(z)))
        nl = jnp.maximum(z, 0.0) + tt
        lz = z - nl
        if masked:
            nl = jnp.where(causal, nl, 0.0)
        hi = nl.astype(BF16)
        lo = (nl - hi.astype(F32)).astype(BF16)
        return lz, hi, lo, jnp.sum(nl, axis=-1, keepdims=True)

    def suffix(hi, lo):
        return _dot(hi, neg_upper) + _dot(lo, neg_upper)

    def weights(lz, rem, c, masked):
        a = jnp.exp2(lz + rem + c)
        if masked:
            a = jnp.where(causal, a, 0.0)
        return a.astype(BF16)

    def tile(j, carry, masked):
        za, acc_a, c_a, acc_b, c_b = carry
        k = ktile(j)
        v = v_ref[pl.ds(pl.multiple_of(j * t, t), t), :]
        zb = _dot_nt(qb, k)
        lz_a, hi_a, lo_a, rs_a = scores(za, masked)
        rem_a = suffix(hi_a, lo_a)
        lz_b, hi_b, lo_b, rs_b = scores(zb, masked)
        rem_b = suffix(hi_b, lo_b)
        w_a = weights(lz_a, rem_a, c_a, masked)
        acc_a = acc_a + _dot(w_a, v)
        za_next = _dot_nt(qa, ktile(jnp.maximum(j - 1, 0)))
        w_b = weights(lz_b, rem_b, c_b, masked)
        acc_b = acc_b + _dot(w_b, v)
        return za_next, acc_a, c_a - rs_a, acc_b, c_b - rs_b

    def live(carry):
        return jnp.max(jnp.maximum(carry[2], carry[4])) > SB_DEAD_LOG2

    def cond(state):
        j, alive, _ = state
        return (j >= 0) & alive

    def body(state):
        j, _, carry = state
        carry = tile(j, carry, False)
        return j - 1, live(carry), carry

    zacc = jnp.zeros((t, LANES), F32)
    zc = jnp.zeros((t, 1), F32)
    carry = tile(i, (_dot_nt(qa, ktile(i)), zacc, zc, zacc, zc), True)
    _, _, carry = lax.while_loop(cond, body, (i - 1, live(carry), carry))
    o = jnp.where(lane < SB_HEAD_DIM, carry[1], carry[3])
    o_ref[...] = (o * _silu(z_ref[...].astype(F32))).astype(o_ref.dtype)


def _sb(proj, bsz, seq):
    nq = seq // SB_T
    return pl.pallas_call(
        _sb_kernel,
        grid=(bsz, BRANCH_WIDTH // LANES, nq),
        in_specs=[
            pl.BlockSpec((SB_T, LANES), lambda b, p, i: (b * nq + i, CB_SB_Q + p)),
            pl.BlockSpec((seq, LANES), lambda b, p, i: (b, CB_SB_K + p)),
            pl.BlockSpec((seq, LANES), lambda b, p, i: (b, CB_SB_V + p)),
            pl.BlockSpec((SB_T, LANES), lambda b, p, i: (b * nq + i, CB_SB_Z + p)),
        ],
        out_specs=pl.BlockSpec((SB_T, LANES), lambda b, p, i: (b * nq + i, p)),
        out_shape=jax.ShapeDtypeStruct((bsz * seq, BRANCH_WIDTH), BF16),
        compiler_params=_params("parallel", "parallel", "arbitrary"),
        name="stickbreak",
    )(proj, proj, proj, proj)


def _pool_tile(u_ref, z_ref, w_ref, sc_ref, halo_scr, first_pos):
    tm = u_ref.shape[0]
    pos = first_pos + lax.broadcasted_iota(jnp.int32, (tm, 1), 0)
    outs = []
    for g, w in enumerate(POOL_WINDOWS):
        cols = slice(g * LANES, (g + 1) * LANES)
        u = u_ref[:, cols].astype(F32)
        s = jnp.concatenate([halo_scr[:, cols], u], axis=0)
        k = 1
        while k < w:
            s = s + pltpu.roll(s, k, axis=0)
            k *= 2
        cnt = jnp.minimum(pos + 1, w).astype(F32)
        d = s[POOL_HALO:, :] / cnt - u
        outs.append(_dot(d.astype(BF16), w_ref[g]))
        halo_scr[:, cols] = u[tm - POOL_HALO:, :]
    y = jnp.concatenate(outs, axis=1)
    return (y * sc_ref[...] * _silu(z_ref[...].astype(F32))).astype(BF16)


def _hgrn_kernel(q_ref, f_ref, i_ref, z_ref, lb_ref, g_ref, o_ref, st_scr):
    t = HG_T

    @pl.when(pl.program_id(2) == 0)
    def _():
        st_scr[...] = jnp.zeros_like(st_scr)

    lb = lb_ref[0]
    sig = jax.nn.sigmoid(f_ref[...].astype(F32))
    f = lb + (1.0 - lb) * sig
    kk = (1.0 - lb) * (1.0 - sig)
    logf = jnp.log(f)
    q = q_ref[...].astype(F32)
    v_bf = i_ref[...]
    v = v_bf.astype(F32)

    rows = lax.broadcasted_iota(jnp.int32, (t, t), 0)
    cols = lax.broadcasted_iota(jnp.int32, (t, t), 1)
    lower = (cols <= rows).astype(BF16)
    hi, mid, lo = _split3(logf)
    b = _dot(lower, hi) + _dot(lower, mid) + _dot(lower, lo)

    st = st_scr[...]
    o = _dot_nt((q * jnp.exp(b)).astype(BF16), st.astype(BF16))

    nb = t // SUBLANES
    q3 = q.reshape(nb, SUBLANES, LANES)
    b3 = b.reshape(nb, SUBLANES, LANES)
    k3 = kk.reshape(nb, SUBLANES, LANES)
    v3 = v.reshape(nb, SUBLANES, LANES)
    sub = lax.broadcasted_iota(jnp.int32, (1, SUBLANES, 1), 1)
    o3 = jnp.zeros((nb, SUBLANES, LANES), F32)
    for s in range(SUBLANES):
        e = jnp.exp(jnp.where(sub >= s, b3 - b3[:, s:s + 1, :], -1e30))
        a = jnp.sum(q3 * e * k3[:, s:s + 1, :], axis=-1, keepdims=True)
        o3 = o3 + a * v3[:, s:s + 1, :]
    o = o + o3.reshape(t, LANES)

    row1 = lax.broadcasted_iota(jnp.int32, (t, 1), 0)
    amat = jnp.zeros((t, t), F32)
    m = SUBLANES
    while m < t:
        blk = 2 * m
        edge = b.reshape(t // blk, blk, LANES)[:, m - 1:m, :]
        edge = jnp.broadcast_to(edge, (t // blk, blk, LANES)).reshape(t, LANES)
        dec = jnp.exp(-jnp.abs(b - edge))
        second = (row1 // m) % 2 == 1
        qt = jnp.where(second, q * dec, 0.0).astype(BF16)
        kt = jnp.where(second, 0.0, kk * dec).astype(BF16)
        p = _dot_nt(qt, kt)
        if blk < t:
            p = jnp.where((rows // blk) == (cols // blk), p, 0.0)
        amat = amat + p
        m = blk
    o = o + _dot(amat.astype(BF16), v_bf)

    b_last = b[t - 1:t, :]
    kd = (kk * jnp.exp(b_last - b)).astype(BF16)
    st_scr[...] = st * jnp.exp(b_last) + _dot_tn(v_bf, kd)

    ms = jnp.mean(o * o, axis=-1, keepdims=True)
    o = o * lax.rsqrt(ms + EPS) * g_ref[0]
    o_ref[...] = (o * _silu(z_ref[...].astype(F32))).astype(o_ref.dtype)


def _hgrn(proj, lb, norm_g, bsz, seq):
    nt = seq // HG_T
    blk = lambda cb: pl.BlockSpec((HG_T, LANES), lambda b, h, i: (b * nt + i, cb + h))
    vec = pl.BlockSpec((1, 1, LANES), lambda b, h, i: (h, 0, 0))
    return pl.pallas_call(
        _hgrn_kernel,
        grid=(bsz, HGRN_HEADS, nt),
        in_specs=[blk(CB_HG_Q), blk(CB_HG_F), blk(CB_HG_I), blk(CB_HG_Z), vec, vec],
        out_specs=pl.BlockSpec((HG_T, LANES), lambda b, h, i: (b * nt + i, h)),
        out_shape=jax.ShapeDtypeStruct((bsz * seq, BRANCH_WIDTH), BF16),
        scratch_shapes=[pltpu.VMEM((LANES, LANES), F32)],
        compiler_params=_params("parallel", "parallel", "arbitrary"),
        name="hgrn2",
    )(proj, proj, proj, proj,
      lb.reshape(HGRN_HEADS, 1, LANES), norm_g.reshape(HGRN_HEADS, 1, LANES))


def _merge_kernel(oa_ref, oc_ref, pu_ref, pz_ref, ga_ref, gb_ref, gc_ref, pw_ref, ps_ref,
                  wb_ref, wo_ref, res_ref, fg_ref, out_ref, halo_scr, *, final, tiles_per_seq):
    ti = pl.program_id(0) % tiles_per_seq

    @pl.when(ti == 0)
    def _():
        halo_scr[...] = jnp.zeros_like(halo_scr)

    o_b = _pool_tile(pu_ref, pz_ref, pw_ref, ps_ref, halo_scr, ti * MERGE_TM)
    merged = (jax.nn.sigmoid(ga_ref[...].astype(F32)) * _dot(oa_ref[...], wb_ref[0])
              + jax.nn.sigmoid(gb_ref[...].astype(F32)) * _dot(o_b, wb_ref[1])
              + jax.nn.sigmoid(gc_ref[...].astype(F32)) * _dot(oc_ref[...], wb_ref[2]))
    h = res_ref[...] + _dot(merged.astype(BF16), wo_ref[...])
    if final:
        ms = jnp.mean(h * h, axis=-1, keepdims=True)
        h = h * lax.rsqrt(ms + EPS) * fg_ref[...]
    out_ref[...] = h


def _merge(o_a, o_c, proj, pool_w_bf, pool_scale, wb_bf, wo_bf, h_res, final_g, seq, final):
    n = h_res.shape[0]
    groups = len(POOL_WINDOWS)
    row = lambda width, blk=0: pl.BlockSpec((MERGE_TM, width), lambda i: (i, blk))
    whole = lambda *shape: pl.BlockSpec(shape, lambda i: (0,) * len(shape))
    return pl.pallas_call(
        functools.partial(_merge_kernel, final=final, tiles_per_seq=seq // MERGE_TM),
        grid=(n // MERGE_TM,),
        in_specs=[
            row(BRANCH_WIDTH), row(BRANCH_WIDTH),
            row(BRANCH_WIDTH, CB_POOL_U // groups), row(BRANCH_WIDTH, CB_POOL_Z // groups),
            row(D_MODEL, GATE_BLOCK0), row(D_MODEL, GATE_BLOCK0 + 1), row(D_MODEL, GATE_BLOCK0 + 2),
            whole(groups, LANES, LANES), whole(1, BRANCH_WIDTH),
            whole(3, BRANCH_WIDTH, D_MODEL), whole(D_MODEL, D_MODEL),
            row(D_MODEL), whole(1, D_MODEL),
        ],
        out_specs=row(D_MODEL),
        out_shape=jax.ShapeDtypeStruct((n, D_MODEL), F32),
        scratch_shapes=[pltpu.VMEM((POOL_HALO, BRANCH_WIDTH), F32)],
        compiler_params=_params("arbitrary"),
        name="merge_final" if final else "merge",
    )(o_a, o_c, proj, proj, proj, proj, proj, pool_w_bf, pool_scale.reshape(1, BRANCH_WIDTH),
      wb_bf, wo_bf, h_res, final_g)


def kernel(x, norm_g, w_in, pool_w, pool_scale, hgrn_lb, hgrn_norm_g, w_branch, w_out, final_g):
    bsz, seq, _ = x.shape
    h_res = x.astype(F32).reshape(bsz * seq, D_MODEL)
    lb_all = jnp.cumsum(jax.nn.softmax(hgrn_lb.astype(F32), axis=0), axis=0)
    lb_all = lb_all - lb_all[:1]
    w_in_bf = w_in.astype(BF16)
    pool_w_bf = pool_w.astype(BF16)
    wb_bf = w_branch.astype(BF16)
    wo_bf = w_out.astype(BF16)
    fg = final_g.astype(F32).reshape(1, D_MODEL)
    for layer in range(DEPTH):
        proj = _proj(h_res, norm_g[layer].astype(F32).reshape(1, D_MODEL), w_in_bf[layer])
        o_a = _sb(proj, bsz, seq)
        o_c = _hgrn(proj, lb_all[layer], hgrn_norm_g[layer].astype(F32), bsz, seq)
        h_res = _merge(o_a, o_c, proj, pool_w_bf[layer], pool_scale[layer].astype(F32),
                       wb_bf[layer], wo_bf[layer], h_res, fg, seq, final=(layer == DEPTH - 1))
    return h_res.reshape(bsz, seq, D_MODEL).astype(x.dtype)
```

```python
import functools
import math

import jax
import jax.numpy as jnp
import numpy as np
from jax import lax
from jax.experimental import pallas as pl
from jax.experimental.pallas import tpu as pltpu

D_MODEL = 1024
DEPTH = 2
BRANCH_WIDTH = D_MODEL // 2
SB_HEAD_DIM = 64
POOL_WINDOWS = (2, 4, 8, 16)
POOL_HALO = 16
HGRN_HEADS = 4
EPS = 1e-6
IN_COLS = 10 * BRANCH_WIDTH + 3 * D_MODEL

LANES = 128
SUBLANES = 8
VMEM_LIMIT = 48 * 1024 * 1024

CB_SB_Q, CB_SB_K, CB_SB_V, CB_SB_Z = 0, 4, 8, 12
CB_POOL_U, CB_POOL_Z = 16, 20
CB_HG_Q, CB_HG_F, CB_HG_I, CB_HG_Z = 24, 28, 32, 36
GATE_BLOCK0 = 5

PROJ_TM, PROJ_TN = 1024, 1024
SB_T = 256
SB_DEAD_LOG2 = -152.0
HG_T = 128
MERGE_TM = 512

F32 = jnp.float32
BF16 = jnp.bfloat16


def _params(*sem):
    return pltpu.CompilerParams(dimension_semantics=sem, vmem_limit_bytes=VMEM_LIMIT)


def _dot(a, b):
    return jnp.dot(a, b, preferred_element_type=F32)


def _dot_nt(a, b):
    return lax.dot_general(a, b, (((1,), (1,)), ((), ())), preferred_element_type=F32)


def _dot_tn(a, b):
    return lax.dot_general(a, b, (((0,), (0,)), ((), ())), preferred_element_type=F32)


def _split3(x):
    hi = x.astype(BF16)
    r = x - hi.astype(F32)
    mid = r.astype(BF16)
    lo = (r - mid.astype(F32)).astype(BF16)
    return hi, mid, lo


def _silu(x):
    return x * jax.nn.sigmoid(x)


def _proj_kernel(x_ref, g_ref, w_ref, cs_ref, o_ref, h_scr):
    @pl.when(pl.program_id(1) == 0)
    def _():
        x = x_ref[...]
        ms = jnp.mean(x * x, axis=-1, keepdims=True)
        h_scr[...] = (x * lax.rsqrt(ms + EPS) * g_ref[...]).astype(BF16)

    o_ref[...] = (_dot(h_scr[...], w_ref[...]) * cs_ref[...]).astype(o_ref.dtype)


def _proj(h_res, g, w_bf):
    n = h_res.shape[0]
    col = lax.broadcasted_iota(jnp.int32, (1, IN_COLS), 1)
    col_scale = jnp.where(col < BRANCH_WIDTH, math.log2(math.e) / math.sqrt(SB_HEAD_DIM), 1.0)
    return pl.pallas_call(
        _proj_kernel,
        grid=(n // PROJ_TM, IN_COLS // PROJ_TN),
        in_specs=[
            pl.BlockSpec((PROJ_TM, D_MODEL), lambda i, j: (i, 0)),
            pl.BlockSpec((1, D_MODEL), lambda i, j: (0, 0)),
            pl.BlockSpec((D_MODEL, PROJ_TN), lambda i, j: (0, j)),
            pl.BlockSpec((1, PROJ_TN), lambda i, j: (0, j)),
        ],
        out_specs=pl.BlockSpec((PROJ_TM, PROJ_TN), lambda i, j: (i, j)),
        out_shape=jax.ShapeDtypeStruct((n, IN_COLS), BF16),
        scratch_shapes=[pltpu.VMEM((PROJ_TM, D_MODEL), BF16)],
        compiler_params=_params("parallel", "arbitrary"),
        name="proj",
    )(h_res, g, w_bf, col_scale.astype(F32))


def _sb_kernel(q_ref, k_ref, v_ref, z_ref, o_ref):
    i = pl.program_id(2)
    t = SB_T
    lane = lax.broadcasted_iota(jnp.int32, (1, LANES), 1)
    rows = lax.broadcasted_iota(jnp.int32, (t, t), 0)
    cols = lax.broadcasted_iota(jnp.int32, (t, t), 1)
    causal = cols < rows
    neg_upper = jnp.where(rows > cols, -1.0, 0.0).astype(BF16)

    q = q_ref[...]
    qa = jnp.where(lane < SB_HEAD_DIM, q, jnp.zeros_like(q))
    qb = jnp.where(lane < SB_HEAD_DIM, jnp.zeros_like(q), q)

    def ktile(j):
        return k_ref[pl.ds(pl.multiple_of(j * t, t), t), :]

    def scores(z, masked):
        tt = jnp.log2(1.0 + jnp.exp2(-jnp.abs(z)))
        nl = jnp.maximum(z, 0.0) + tt
        lz = z - nl
        if masked:
            nl = jnp.where(causal, nl, 0.0)
        hi = nl.astype(BF16)
        lo = (nl - hi.astype(F32)).astype(BF16)
        return lz, hi, lo, jnp.sum(nl, axis=-1, keepdims=True)

    def suffix(hi, lo):
        return _dot(hi, neg_upper) + _dot(lo, neg_upper)

    def weights(lz, rem, c, masked):
        a = jnp.exp2(lz + rem + c)
        if masked:
            a = jnp.where(causal, a, 0.0)
        return a.astype(BF16)

    def tile(j, carry, masked):
        za, acc_a, c_a, acc_b, c_b = carry
        k = ktile(j)
        v = v_ref[pl.ds(pl.multiple_of(j * t, t), t), :]
        zb = _dot_nt(qb, k)
        lz_a, hi_a, lo_a, rs_a = scores(za, masked)
        rem_a = suffix(hi_a, lo_a)
        lz_b, hi_b, lo_b, rs_b = scores(zb, masked)
        rem_b = suffix(hi_b, lo_b)
        w_a = weights(lz_a, rem_a, c_a, masked)
        acc_a = acc_a + _dot(w_a, v)
        za_next = _dot_nt(qa, ktile(jnp.maximum(j - 1, 0)))
        w_b = weights(lz_b, rem_b, c_b, masked)
        acc_b = acc_b + _dot(w_b, v)
        return za_next, acc_a, c_a - rs_a, acc_b, c_b - rs_b

    def live(carry):
        return jnp.max(jnp.maximum(carry[2], carry[4])) > SB_DEAD_LOG2

    def cond(state):
        j, alive, _ = state
        return (j >= 0) & alive

    def body(state):
        j, _, carry = state
        carry = tile(j, carry, False)
        return j - 1, live(carry), carry

    zacc = jnp.zeros((t, LANES), F32)
    zc = jnp.zeros((t, 1), F32)
    carry = tile(i, (_dot_nt(qa, ktile(i)), zacc, zc, zacc, zc), True)
    _, _, carry = lax.while_loop(cond, body, (i - 1, live(carry), carry))
    o = jnp.where(lane < SB_HEAD_DIM, carry[1], carry[3])
    o_ref[...] = (o * _silu(z_ref[...].astype(F32))).astype(o_ref.dtype)


def _sb(proj, bsz, seq):
    nq = seq // SB_T
    return pl.pallas_call(
        _sb_kernel,
        grid=(bsz, BRANCH_WIDTH // LANES, nq),
        in_specs=[
            pl.BlockSpec((SB_T, LANES), lambda b, p, i: (b * nq + i, CB_SB_Q + p)),
            pl.BlockSpec((seq, LANES), lambda b, p, i: (b, CB_SB_K + p)),
            pl.BlockSpec((seq, LANES), lambda b, p, i: (b, CB_SB_V + p)),
            pl.BlockSpec((SB_T, LANES), lambda b, p, i: (b * nq + i, CB_SB_Z + p)),
        ],
        out_specs=pl.BlockSpec((SB_T, LANES), lambda b, p, i: (b * nq + i, p)),
        out_shape=jax.ShapeDtypeStruct((bsz * seq, BRANCH_WIDTH), BF16),
        compiler_params=_params("parallel", "parallel", "arbitrary"),
        name="stickbreak",
    )(proj, proj, proj, proj)


def _pool_tile(u_ref, z_ref, w_ref, sc_ref, halo_scr, first_pos):
    tm = u_ref.shape[0]
    pos = first_pos + lax.broadcasted_iota(jnp.int32, (tm, 1), 0)
    outs = []
    for g, w in enumerate(POOL_WINDOWS):
        cols = slice(g * LANES, (g + 1) * LANES)
        u = u_ref[:, cols].astype(F32)
        s = jnp.concatenate([halo_scr[:, cols], u], axis=0)
        k = 1
        while k < w:
            s = s + pltpu.roll(s, k, axis=0)
            k *= 2
        cnt = jnp.minimum(pos + 1, w).astype(F32)
        d = s[POOL_HALO:, :] / cnt - u
        outs.append(_dot(d.astype(BF16), w_ref[g]))
        halo_scr[:, cols] = u[tm - POOL_HALO:, :]
    y = jnp.concatenate(outs, axis=1)
    return (y * sc_ref[...] * _silu(z_ref[...].astype(F32))).astype(BF16)


HG_LEVELS = HG_T.bit_length() - 1


def _hgrn_tables():
    r = np.arange(HG_T)
    tri = (r[None, :] <= r[:, None]).astype(np.float32)
    pair = np.zeros((HG_LEVELS, HG_T, HG_T), np.float32)
    sign = np.zeros((HG_LEVELS, HG_T, LANES), np.float32)
    for l in range(HG_LEVELS):
        m = 1 << l
        second = (r % (2 * m)) >= m
        same = (r[:, None] // (2 * m)) == (r[None, :] // (2 * m))
        pair[l] = same & second[:, None] & ~second[None, :]
        sign[l] = np.where(second, 1.0, -1.0)[:, None]
    return jnp.asarray(tri, BF16), jnp.asarray(pair), jnp.asarray(sign)


def _hgrn_edge(b, level):
    t = b.shape[0]
    m = 1 << level
    if m >= SUBLANES:
        blk = 2 * m
        e = b.reshape(t // blk, blk, LANES)[:, m - 1:m, :]
        return jnp.broadcast_to(e, (t // blk, blk, LANES)).reshape(t, LANES)
    b3 = b.reshape(t // SUBLANES, SUBLANES, LANES)
    if m == 4:
        e = jnp.broadcast_to(b3[:, 3:4, :], b3.shape)
    else:
        sub = lax.broadcasted_iota(jnp.int32, (1, SUBLANES, 1), 1)
        e = jnp.where(sub < 4, b3[:, 1:2, :], b3[:, 5:6, :])
    return e.reshape(t, LANES)


def _hgrn_kernel(q_ref, f_ref, i_ref, z_ref, lb_ref, g_ref, tri_ref, pair_ref, sign_ref,
                 o_ref, st_scr):
    t = HG_T

    @pl.when(pl.program_id(2) == 0)
    def _():
        st_scr[...] = jnp.zeros_like(st_scr)

    def front(cols):
        lb = lb_ref[:, cols]
        sig = jax.nn.sigmoid(f_ref[:, cols].astype(F32))
        f = lb + (1.0 - lb) * sig
        kk = (1.0 - lb) * (1.0 - sig)
        hi, mid, lo = _split3(jnp.log2(f))
        tri = tri_ref[...]
        return f, kk, _dot(tri, hi) + _dot(tri, mid) + _dot(tri, lo)

    def middle(h, cols, f, kk, b):
        q = q_ref[:, cols].astype(F32)
        v_bf = i_ref[:, cols]
        st = st_scr[h]
        b_last = b[t - 1:t, :]
        o = _dot_nt((q * jnp.exp2(b)).astype(BF16), st.astype(BF16))
        kd = (kk * jnp.exp2(b_last - b)).astype(BF16)
        st_scr[h] = st * jnp.exp2(b_last) + _dot_tn(v_bf, kd)
        o = o + jnp.sum(q * kk, axis=-1, keepdims=True) * v_bf.astype(F32)
        amat = jnp.zeros((t, t), F32)
        for level in range(HG_LEVELS):
            if level == 0:
                dec = jnp.where(sign_ref[0] > 0.0, f, 1.0)
            else:
                dec = jnp.exp2((b - _hgrn_edge(b, level)) * sign_ref[level])
            p = _dot_nt((q * dec).astype(BF16), (kk * dec).astype(BF16))
            amat = amat + p * pair_ref[level]
        return o, amat.astype(BF16), v_bf

    def back(cols, o, amat, v_bf):
        o = o + _dot(amat, v_bf)
        ms = jnp.mean(o * o, axis=-1, keepdims=True)
        o = o * lax.rsqrt(ms + EPS) * g_ref[:, cols]
        o_ref[:, cols] = (o * _silu(z_ref[:, cols].astype(F32))).astype(o_ref.dtype)

    heads = [slice(h * LANES, (h + 1) * LANES) for h in range(2)]
    fronts = [front(c) for c in heads]
    mids = [middle(h, c, *fr) for h, (c, fr) in enumerate(zip(heads, fronts))]
    for c, md in zip(heads, mids):
        back(c, *md)


def _hgrn(proj, lb, norm_g, bsz, seq):
    nt = seq // HG_T
    width = 2 * LANES
    tri, pair, sign = _hgrn_tables()
    blk = lambda cb: pl.BlockSpec((HG_T, width), lambda b, h, i: (b * nt + i, cb // 2 + h))
    vec = pl.BlockSpec((1, width), lambda b, h, i: (0, h))
    whole = lambda arr: pl.BlockSpec(arr.shape, lambda b, h, i: (0,) * arr.ndim)
    return pl.pallas_call(
        _hgrn_kernel,
        grid=(bsz, HGRN_HEADS // 2, nt),
        in_specs=[blk(CB_HG_Q), blk(CB_HG_F), blk(CB_HG_I), blk(CB_HG_Z), vec, vec,
                  whole(tri), whole(pair), whole(sign)],
        out_specs=pl.BlockSpec((HG_T, width), lambda b, h, i: (b * nt + i, h)),
        out_shape=jax.ShapeDtypeStruct((bsz * seq, BRANCH_WIDTH), BF16),
        scratch_shapes=[pltpu.VMEM((2, LANES, LANES), F32)],
        compiler_params=_params("parallel", "parallel", "arbitrary"),
        name="hgrn2",
    )(proj, proj, proj, proj, lb.reshape(1, BRANCH_WIDTH), norm_g.reshape(1, BRANCH_WIDTH),
      tri, pair, sign)


def _merge_kernel(oa_ref, oc_ref, pu_ref, pz_ref, ga_ref, gb_ref, gc_ref, pw_ref, ps_ref,
                  wb_ref, wo_ref, res_ref, fg_ref, out_ref, halo_scr, *, final, tiles_per_seq):
    ti = pl.program_id(0) % tiles_per_seq

    @pl.when(ti == 0)
    def _():
        halo_scr[...] = jnp.zeros_like(halo_scr)

    o_b = _pool_tile(pu_ref, pz_ref, pw_ref, ps_ref, halo_scr, ti * MERGE_TM)
    merged = (jax.nn.sigmoid(ga_ref[...].astype(F32)) * _dot(oa_ref[...], wb_ref[0])
              + jax.nn.sigmoid(gb_ref[...].astype(F32)) * _dot(o_b, wb_ref[1])
              + jax.nn.sigmoid(gc_ref[...].astype(F32)) * _dot(oc_ref[...], wb_ref[2]))
    h = res_ref[...] + _dot(merged.astype(BF16), wo_ref[...])
    if final:
        ms = jnp.mean(h * h, axis=-1, keepdims=True)
        h = h * lax.rsqrt(ms + EPS) * fg_ref[...]
    out_ref[...] = h


def _merge(o_a, o_c, proj, pool_w_bf, pool_scale, wb_bf, wo_bf, h_res, final_g, seq, final):
    n = h_res.shape[0]
    groups = len(POOL_WINDOWS)
    row = lambda width, blk=0: pl.BlockSpec((MERGE_TM, width), lambda i: (i, blk))
    whole = lambda *shape: pl.BlockSpec(shape, lambda i: (0,) * len(shape))
    return pl.pallas_call(
        functools.partial(_merge_kernel, final=final, tiles_per_seq=seq // MERGE_TM),
        grid=(n // MERGE_TM,),
        in_specs=[
            row(BRANCH_WIDTH), row(BRANCH_WIDTH),
            row(BRANCH_WIDTH, CB_POOL_U // groups), row(BRANCH_WIDTH, CB_POOL_Z // groups),
            row(D_MODEL, GATE_BLOCK0), row(D_MODEL, GATE_BLOCK0 + 1), row(D_MODEL, GATE_BLOCK0 + 2),
            whole(groups, LANES, LANES), whole(1, BRANCH_WIDTH),
            whole(3, BRANCH_WIDTH, D_MODEL), whole(D_MODEL, D_MODEL),
            row(D_MODEL), whole(1, D_MODEL),
        ],
        out_specs=row(D_MODEL),
        out_shape=jax.ShapeDtypeStruct((n, D_MODEL), F32),
        scratch_shapes=[pltpu.VMEM((POOL_HALO, BRANCH_WIDTH), F32)],
        compiler_params=_params("arbitrary"),
        name="merge_final" if final else "merge",
    )(o_a, o_c, proj, proj, proj, proj, proj, pool_w_bf, pool_scale.reshape(1, BRANCH_WIDTH),
      wb_bf, wo_bf, h_res, final_g)


def kernel(x, norm_g, w_in, pool_w, pool_scale, hgrn_lb, hgrn_norm_g, w_branch, w_out, final_g):
    bsz, seq, _ = x.shape
    h_res = x.astype(F32).reshape(bsz * seq, D_MODEL)
    lb_all = jnp.cumsum(jax.nn.softmax(hgrn_lb.astype(F32), axis=0), axis=0)
    lb_all = lb_all - lb_all[:1]
    w_in_bf = w_in.astype(BF16)
    pool_w_bf = pool_w.astype(BF16)
    wb_bf = w_branch.astype(BF16)
    wo_bf = w_out.astype(BF16)
    fg = final_g.astype(F32).reshape(1, D_MODEL)
    for layer in range(DEPTH):
        proj = _proj(h_res, norm_g[layer].astype(F32).reshape(1, D_MODEL), w_in_bf[layer])
        o_a = _sb(proj, bsz, seq)
        o_c = _hgrn(proj, lb_all[layer], hgrn_norm_g[layer].astype(F32), bsz, seq)
        h_res = _merge(o_a, o_c, proj, pool_w_bf[layer], pool_scale[layer].astype(F32),
                       wb_bf[layer], wo_bf[layer], h_res, fg, seq, final=(layer == DEPTH - 1))
    return h_res.reshape(bsz, seq, D_MODEL).astype(x.dtype)
```

```python
import functools
import math

import jax
import jax.numpy as jnp
import numpy as np
from jax import lax
from jax.experimental import pallas as pl
from jax.experimental.pallas import tpu as pltpu

D_MODEL = 1024
DEPTH = 2
BRANCH_WIDTH = D_MODEL // 2
SB_HEAD_DIM = 64
POOL_WINDOWS = (2, 4, 8, 16)
POOL_HALO = 16
HGRN_HEADS = 4
EPS = 1e-6
IN_COLS = 10 * BRANCH_WIDTH + 3 * D_MODEL

LANES = 128
SUBLANES = 8
VMEM_LIMIT = 48 * 1024 * 1024

CB_SB_Q, CB_SB_K, CB_SB_V, CB_SB_Z = 0, 4, 8, 12
CB_POOL_U, CB_POOL_Z = 16, 20
CB_HG_Q, CB_HG_F, CB_HG_I, CB_HG_Z = 24, 28, 32, 36
GATE_BLOCK0 = 5

PROJ_TM, PROJ_TN = 1024, 1024
SB_T = 256
SB_QTILES = 2
SB_DEAD_LOG2 = -152.0
SB_OFF_LOG2 = -1e30
HG_T = 128
MERGE_TM = 512

F32 = jnp.float32
BF16 = jnp.bfloat16


def _params(*sem):
    return pltpu.CompilerParams(dimension_semantics=sem, vmem_limit_bytes=VMEM_LIMIT)


def _dot(a, b):
    return jnp.dot(a, b, preferred_element_type=F32)


def _dot_nt(a, b):
    return lax.dot_general(a, b, (((1,), (1,)), ((), ())), preferred_element_type=F32)


def _dot_tn(a, b):
    return lax.dot_general(a, b, (((0,), (0,)), ((), ())), preferred_element_type=F32)


def _split3(x):
    hi = x.astype(BF16)
    r = x - hi.astype(F32)
    mid = r.astype(BF16)
    lo = (r - mid.astype(F32)).astype(BF16)
    return hi, mid, lo


def _silu(x):
    return x * jax.nn.sigmoid(x)


def _proj_kernel(x_ref, g_ref, w_ref, cs_ref, o_ref, h_scr):
    @pl.when(pl.program_id(1) == 0)
    def _():
        x = x_ref[...]
        ms = jnp.mean(x * x, axis=-1, keepdims=True)
        h_scr[...] = (x * lax.rsqrt(ms + EPS) * g_ref[...]).astype(BF16)

    o_ref[...] = (_dot(h_scr[...], w_ref[...]) * cs_ref[...]).astype(o_ref.dtype)


def _proj(h_res, g, w_bf):
    n = h_res.shape[0]
    col = lax.broadcasted_iota(jnp.int32, (1, IN_COLS), 1)
    col_scale = jnp.where(col < BRANCH_WIDTH, math.log2(math.e) / math.sqrt(SB_HEAD_DIM), 1.0)
    return pl.pallas_call(
        _proj_kernel,
        grid=(n // PROJ_TM, IN_COLS // PROJ_TN),
        in_specs=[
            pl.BlockSpec((PROJ_TM, D_MODEL), lambda i, j: (i, 0)),
            pl.BlockSpec((1, D_MODEL), lambda i, j: (0, 0)),
            pl.BlockSpec((D_MODEL, PROJ_TN), lambda i, j: (0, j)),
            pl.BlockSpec((1, PROJ_TN), lambda i, j: (0, j)),
        ],
        out_specs=pl.BlockSpec((PROJ_TM, PROJ_TN), lambda i, j: (i, j)),
        out_shape=jax.ShapeDtypeStruct((n, IN_COLS), BF16),
        scratch_shapes=[pltpu.VMEM((PROJ_TM, D_MODEL), BF16)],
        compiler_params=_params("parallel", "arbitrary"),
        name="proj",
    )(h_res, g, w_bf, col_scale.astype(F32))


def _sb_kernel(q_ref, k_ref, v_ref, z_ref, o_ref):
    t = SB_T
    lane = lax.broadcasted_iota(jnp.int32, (1, LANES), 1)
    rows = lax.broadcasted_iota(jnp.int32, (t, t), 0)
    cols = lax.broadcasted_iota(jnp.int32, (t, t), 1)
    causal = cols < rows
    neg_upper = jnp.where(rows > cols, -1.0, 0.0).astype(BF16)

    def ktile(j):
        return k_ref[pl.ds(pl.multiple_of(j * t, t), t), :]

    def vtile(j):
        return v_ref[pl.ds(pl.multiple_of(j * t, t), t), :]

    def chain(z, masked):
        tt = jnp.log2(1.0 + jnp.exp2(-jnp.abs(z)))
        nl = jnp.maximum(z, 0.0) + tt
        lz = z - nl
        if masked:
            nl = jnp.where(causal, nl, 0.0)
        return lz, _dot(nl.astype(BF16), neg_upper), jnp.sum(nl, axis=-1, keepdims=True)

    def weights(lz, rem, c, masked):
        a = jnp.exp2(lz + rem if c is None else lz + rem + c)
        if masked:
            a = jnp.where(causal, a, 0.0)
        return a.astype(BF16)

    def live(carry):
        return jnp.max(jnp.maximum(carry[1], carry[3])) > SB_DEAD_LOG2

    def near(i, qa, qb):
        k0, v0 = ktile(i), vtile(i)
        prev = jnp.maximum(i - 1, 0)
        k1, v1 = ktile(prev), vtile(prev)
        z0a, z0b = _dot_nt(qa, k0), _dot_nt(qb, k0)
        z1a, z1b = _dot_nt(qa, k1), _dot_nt(qb, k1)
        lz0a, rem0a, rs0a = chain(z0a, True)
        lz0b, rem0b, rs0b = chain(z0b, True)
        acc_a = _dot(weights(lz0a, rem0a, None, True), v0)
        lz1a, rem1a, rs1a = chain(z1a, False)
        acc_b = _dot(weights(lz0b, rem0b, None, True), v0)
        lz1b, rem1b, rs1b = chain(z1b, False)
        off = jnp.where(i > 0, 0.0, SB_OFF_LOG2)
        c1a = off - rs0a
        c1b = off - rs0b
        acc_a = acc_a + _dot(weights(lz1a, rem1a, c1a, False), v1)
        acc_b = acc_b + _dot(weights(lz1b, rem1b, c1b, False), v1)
        return acc_a, c1a - rs1a, acc_b, c1b - rs1b

    def far(i, qa, qb, carry):
        def body(state):
            j, _, (acc_a, c_a, acc_b, c_b) = state
            k, v = ktile(j), vtile(j)
            lz_a, rem_a, rs_a = chain(_dot_nt(qa, k), False)
            lz_b, rem_b, rs_b = chain(_dot_nt(qb, k), False)
            acc_a = acc_a + _dot(weights(lz_a, rem_a, c_a, False), v)
            acc_b = acc_b + _dot(weights(lz_b, rem_b, c_b, False), v)
            new = (acc_a, c_a - rs_a, acc_b, c_b - rs_b)
            return j - 1, live(new), new

        state = lax.while_loop(lambda st: (st[0] >= 0) & st[1], body, (i - 2, live(carry), carry))
        return state[2]

    tiles = []
    for u in range(SB_QTILES):
        q = q_ref[u * t:(u + 1) * t, :]
        qa = jnp.where(lane < SB_HEAD_DIM, q, jnp.zeros_like(q))
        qb = jnp.where(lane < SB_HEAD_DIM, jnp.zeros_like(q), q)
        i = pl.program_id(2) * SB_QTILES + u
        tiles.append((i, qa, qb, near(i, qa, qb)))
    for u, (i, qa, qb, carry) in enumerate(tiles):
        carry = far(i, qa, qb, carry)
        o = jnp.where(lane < SB_HEAD_DIM, carry[0], carry[2])
        gate = _silu(z_ref[u * t:(u + 1) * t, :].astype(F32))
        o_ref[u * t:(u + 1) * t, :] = (o * gate).astype(o_ref.dtype)


def _sb(proj, bsz, seq):
    tq = SB_T * SB_QTILES
    nq = seq // tq
    return pl.pallas_call(
        _sb_kernel,
        grid=(bsz, BRANCH_WIDTH // LANES, nq),
        in_specs=[
            pl.BlockSpec((tq, LANES), lambda b, p, i: (b * nq + i, CB_SB_Q + p)),
            pl.BlockSpec((seq, LANES), lambda b, p, i: (b, CB_SB_K + p)),
            pl.BlockSpec((seq, LANES), lambda b, p, i: (b, CB_SB_V + p)),
            pl.BlockSpec((tq, LANES), lambda b, p, i: (b * nq + i, CB_SB_Z + p)),
        ],
        out_specs=pl.BlockSpec((tq, LANES), lambda b, p, i: (b * nq + i, p)),
        out_shape=jax.ShapeDtypeStruct((bsz * seq, BRANCH_WIDTH), BF16),
        compiler_params=_params("parallel", "parallel", "arbitrary"),
        name="stickbreak",
    )(proj, proj, proj, proj)


def _pool_tile(u_ref, z_ref, w_ref, sc_ref, halo_scr, first_pos):
    tm = u_ref.shape[0]
    pos = first_pos + lax.broadcasted_iota(jnp.int32, (tm, 1), 0)
    outs = []
    for g, w in enumerate(POOL_WINDOWS):
        cols = slice(g * LANES, (g + 1) * LANES)
        u = u_ref[:, cols].astype(F32)
        s = jnp.concatenate([halo_scr[:, cols], u], axis=0)
        k = 1
        while k < w:
            s = s + pltpu.roll(s, k, axis=0)
            k *= 2
        cnt = jnp.minimum(pos + 1, w).astype(F32)
        d = s[POOL_HALO:, :] / cnt - u
        outs.append(_dot(d.astype(BF16), w_ref[g]))
        halo_scr[:, cols] = u[tm - POOL_HALO:, :]
    y = jnp.concatenate(outs, axis=1)
    return (y * sc_ref[...] * _silu(z_ref[...].astype(F32))).astype(BF16)


HG_LEVELS = HG_T.bit_length() - 1


def _hgrn_tables():
    r = np.arange(HG_T)
    tri = (r[None, :] <= r[:, None]).astype(np.float32)
    pair = np.zeros((HG_LEVELS, HG_T, HG_T), np.float32)
    sign = np.zeros((HG_LEVELS, HG_T, LANES), np.float32)
    for l in range(HG_LEVELS):
        m = 1 << l
        second = (r % (2 * m)) >= m
        same = (r[:, None] // (2 * m)) == (r[None, :] // (2 * m))
        pair[l] = same & second[:, None] & ~second[None, :]
        sign[l] = np.where(second, 1.0, -1.0)[:, None]
    return jnp.asarray(tri, BF16), jnp.asarray(pair), jnp.asarray(sign)


def _hgrn_edge(b, level):
    t = b.shape[0]
    m = 1 << level
    if m >= SUBLANES:
        blk = 2 * m
        e = b.reshape(t // blk, blk, LANES)[:, m - 1:m, :]
        return jnp.broadcast_to(e, (t // blk, blk, LANES)).reshape(t, LANES)
    b3 = b.reshape(t // SUBLANES, SUBLANES, LANES)
    if m == 4:
        e = jnp.broadcast_to(b3[:, 3:4, :], b3.shape)
    else:
        sub = lax.broadcasted_iota(jnp.int32, (1, SUBLANES, 1), 1)
        e = jnp.where(sub < 4, b3[:, 1:2, :], b3[:, 5:6, :])
    return e.reshape(t, LANES)


def _hgrn_kernel(q_ref, f_ref, i_ref, z_ref, lb_ref, g_ref, tri_ref, pair_ref, sign_ref,
                 o_ref, st_scr):
    t = HG_T

    @pl.when(pl.program_id(2) == 0)
    def _():
        st_scr[...] = jnp.zeros_like(st_scr)

    def front(cols):
        lb = lb_ref[:, cols]
        sig = jax.nn.sigmoid(f_ref[:, cols].astype(F32))
        f = lb + (1.0 - lb) * sig
        kk = (1.0 - lb) * (1.0 - sig)
        hi, mid, lo = _split3(jnp.log2(f))
        tri = tri_ref[...]
        return f, kk, _dot(tri, hi) + _dot(tri, mid) + _dot(tri, lo)

    def middle(h, cols, f, kk, b):
        q = q_ref[:, cols].astype(F32)
        v_bf = i_ref[:, cols]
        st = st_scr[h]
        b_last = b[t - 1:t, :]
        o = _dot_nt((q * jnp.exp2(b)).astype(BF16), st.astype(BF16))
        kd = (kk * jnp.exp2(b_last - b)).astype(BF16)
        st_scr[h] = st * jnp.exp2(b_last) + _dot_tn(v_bf, kd)
        o = o + jnp.sum(q * kk, axis=-1, keepdims=True) * v_bf.astype(F32)
        amat = jnp.zeros((t, t), F32)
        for level in range(HG_LEVELS):
            if level == 0:
                dec = jnp.where(sign_ref[0] > 0.0, f, 1.0)
            else:
                dec = jnp.exp2((b - _hgrn_edge(b, level)) * sign_ref[level])
            p = _dot_nt((q * dec).astype(BF16), (kk * dec).astype(BF16))
            amat = amat + p * pair_ref[level]
        return o, amat.astype(BF16), v_bf

    def back(cols, o, amat, v_bf):
        o = o + _dot(amat, v_bf)
        ms = jnp.mean(o * o, axis=-1, keepdims=True)
        o = o * lax.rsqrt(ms + EPS) * g_ref[:, cols]
        o_ref[:, cols] = (o * _silu(z_ref[:, cols].astype(F32))).astype(o_ref.dtype)

    heads = [slice(h * LANES, (h + 1) * LANES) for h in range(2)]
    fronts = [front(c) for c in heads]
    mids = [middle(h, c, *fr) for h, (c, fr) in enumerate(zip(heads, fronts))]
    for c, md in zip(heads, mids):
        back(c, *md)


def _hgrn(proj, lb, norm_g, bsz, seq):
    nt = seq // HG_T
    width = 2 * LANES
    tri, pair, sign = _hgrn_tables()
    blk = lambda cb: pl.BlockSpec((HG_T, width), lambda b, h, i: (b * nt + i, cb // 2 + h))
    vec = pl.BlockSpec((1, width), lambda b, h, i: (0, h))
    whole = lambda arr: pl.BlockSpec(arr.shape, lambda b, h, i: (0,) * arr.ndim)
    return pl.pallas_call(
        _hgrn_kernel,
        grid=(bsz, HGRN_HEADS // 2, nt),
        in_specs=[blk(CB_HG_Q), blk(CB_HG_F), blk(CB_HG_I), blk(CB_HG_Z), vec, vec,
                  whole(tri), whole(pair), whole(sign)],
        out_specs=pl.BlockSpec((HG_T, width), lambda b, h, i: (b * nt + i, h)),
        out_shape=jax.ShapeDtypeStruct((bsz * seq, BRANCH_WIDTH), BF16),
        scratch_shapes=[pltpu.VMEM((2, LANES, LANES), F32)],
        compiler_params=_params("parallel", "parallel", "arbitrary"),
        name="hgrn2",
    )(proj, proj, proj, proj, lb.reshape(1, BRANCH_WIDTH), norm_g.reshape(1, BRANCH_WIDTH),
      tri, pair, sign)


def _merge_kernel(oa_ref, oc_ref, pu_ref, pz_ref, ga_ref, gb_ref, gc_ref, pw_ref, ps_ref,
                  wb_ref, wo_ref, res_ref, fg_ref, out_ref, halo_scr, *, final, tiles_per_seq):
    ti = pl.program_id(0) % tiles_per_seq

    @pl.when(ti == 0)
    def _():
        halo_scr[...] = jnp.zeros_like(halo_scr)

    o_b = _pool_tile(pu_ref, pz_ref, pw_ref, ps_ref, halo_scr, ti * MERGE_TM)
    merged = (jax.nn.sigmoid(ga_ref[...].astype(F32)) * _dot(oa_ref[...], wb_ref[0])
              + jax.nn.sigmoid(gb_ref[...].astype(F32)) * _dot(o_b, wb_ref[1])
              + jax.nn.sigmoid(gc_ref[...].astype(F32)) * _dot(oc_ref[...], wb_ref[2]))
    h = res_ref[...] + _dot(merged.astype(BF16), wo_ref[...])
    if final:
        ms = jnp.mean(h * h, axis=-1, keepdims=True)
        h = h * lax.rsqrt(ms + EPS) * fg_ref[...]
    out_ref[...] = h


def _merge(o_a, o_c, proj, pool_w_bf, pool_scale, wb_bf, wo_bf, h_res, final_g, seq, final):
    n = h_res.shape[0]
    groups = len(POOL_WINDOWS)
    row = lambda width, blk=0: pl.BlockSpec((MERGE_TM, width), lambda i: (i, blk))
    whole = lambda *shape: pl.BlockSpec(shape, lambda i: (0,) * len(shape))
    return pl.pallas_call(
        functools.partial(_merge_kernel, final=final, tiles_per_seq=seq // MERGE_TM),
        grid=(n // MERGE_TM,),
        in_specs=[
            row(BRANCH_WIDTH), row(BRANCH_WIDTH),
            row(BRANCH_WIDTH, CB_POOL_U // groups), row(BRANCH_WIDTH, CB_POOL_Z // groups),
            row(D_MODEL, GATE_BLOCK0), row(D_MODEL, GATE_BLOCK0 + 1), row(D_MODEL, GATE_BLOCK0 + 2),
            whole(groups, LANES, LANES), whole(1, BRANCH_WIDTH),
            whole(3, BRANCH_WIDTH, D_MODEL), whole(D_MODEL, D_MODEL),
            row(D_MODEL), whole(1, D_MODEL),
        ],
        out_specs=row(D_MODEL),
        out_shape=jax.ShapeDtypeStruct((n, D_MODEL), F32),
        scratch_shapes=[pltpu.VMEM((POOL_HALO, BRANCH_WIDTH), F32)],
        compiler_params=_params("arbitrary"),
        name="merge_final" if final else "merge",
    )(o_a, o_c, proj, proj, proj, proj, proj, pool_w_bf, pool_scale.reshape(1, BRANCH_WIDTH),
      wb_bf, wo_bf, h_res, final_g)


def kernel(x, norm_g, w_in, pool_w, pool_scale, hgrn_lb, hgrn_norm_g, w_branch, w_out, final_g):
    bsz, seq, _ = x.shape
    h_res = x.astype(F32).reshape(bsz * seq, D_MODEL)
    lb_all = jnp.cumsum(jax.nn.softmax(hgrn_lb.astype(F32), axis=0), axis=0)
    lb_all = lb_all - lb_all[:1]
    w_in_bf = w_in.astype(BF16)
    pool_w_bf = pool_w.astype(BF16)
    wb_bf = w_branch.astype(BF16)
    wo_bf = w_out.astype(BF16)
    fg = final_g.astype(F32).reshape(1, D_MODEL)
    for layer in range(DEPTH):
        proj = _proj(h_res, norm_g[layer].astype(F32).reshape(1, D_MODEL), w_in_bf[layer])
        o_a = _sb(proj, bsz, seq)
        o_c = _hgrn(proj, lb_all[layer], hgrn_norm_g[layer].astype(F32), bsz, seq)
        h_res = _merge(o_a, o_c, proj, pool_w_bf[layer], pool_scale[layer].astype(F32),
                       wb_bf[layer], wo_bf[layer], h_res, fg, seq, final=(layer == DEPTH - 1))
    return h_res.reshape(bsz, seq, D_MODEL).astype(x.dtype)
```

```python
import functools
import math

import jax
import jax.numpy as jnp
import numpy as np
from jax import lax
from jax.experimental import pallas as pl
from jax.experimental.pallas import tpu as pltpu

D_MODEL = 1024
DEPTH = 2
BRANCH_WIDTH = D_MODEL // 2
SB_HEAD_DIM = 64
POOL_WINDOWS = (2, 4, 8, 16)
POOL_HALO = 16
HGRN_HEADS = 4
EPS = 1e-6
IN_COLS = 10 * BRANCH_WIDTH + 3 * D_MODEL

LANES = 128
SUBLANES = 8
VMEM_LIMIT = 48 * 1024 * 1024

CB_SB_Q, CB_SB_K, CB_SB_V, CB_SB_Z = 0, 4, 8, 12
CB_POOL_U, CB_POOL_Z = 16, 20
CB_HG_Q, CB_HG_F, CB_HG_I, CB_HG_Z = 24, 28, 32, 36
GATE_BLOCK0 = 5

PROJ_TM, PROJ_TN = 1024, 2048
SB_T = 256
SB_QTILES = 4
SB_DEAD_LOG2 = -152.0
SB_OFF_LOG2 = -1e30
HG_T = 128
HG_HEADS_PER_STEP = 4
MERGE_TM = 512

F32 = jnp.float32
BF16 = jnp.bfloat16


def _params(*sem):
    return pltpu.CompilerParams(dimension_semantics=sem, vmem_limit_bytes=VMEM_LIMIT)


def _dot(a, b):
    return jnp.dot(a, b, preferred_element_type=F32)


def _dot_nt(a, b):
    return lax.dot_general(a, b, (((1,), (1,)), ((), ())), preferred_element_type=F32)


def _dot_tn(a, b):
    return lax.dot_general(a, b, (((0,), (0,)), ((), ())), preferred_element_type=F32)


def _split3(x):
    hi = x.astype(BF16)
    r = x - hi.astype(F32)
    mid = r.astype(BF16)
    lo = (r - mid.astype(F32)).astype(BF16)
    return hi, mid, lo


def _silu(x):
    return x * jax.nn.sigmoid(x)


def _proj_kernel(x_ref, g_ref, w_ref, cs_ref, o_ref, h_scr):
    @pl.when(pl.program_id(1) == 0)
    def _():
        x = x_ref[...]
        ms = jnp.mean(x * x, axis=-1, keepdims=True)
        h_scr[...] = (x * lax.rsqrt(ms + EPS) * g_ref[...]).astype(BF16)

    o_ref[...] = (_dot(h_scr[...], w_ref[...]) * cs_ref[...]).astype(o_ref.dtype)


def _proj(h_res, g, w_bf, layer):
    n = h_res.shape[0]
    col = lax.broadcasted_iota(jnp.int32, (1, IN_COLS), 1)
    col_scale = jnp.where(col < BRANCH_WIDTH, math.log2(math.e) / math.sqrt(SB_HEAD_DIM), 1.0)
    return pl.pallas_call(
        _proj_kernel,
        grid=(n // PROJ_TM, IN_COLS // PROJ_TN),
        in_specs=[
            pl.BlockSpec((PROJ_TM, D_MODEL), lambda i, j: (i, 0)),
            pl.BlockSpec((1, D_MODEL), lambda i, j: (0, 0)),
            pl.BlockSpec((None, D_MODEL, PROJ_TN), lambda i, j: (layer, 0, j)),
            pl.BlockSpec((1, PROJ_TN), lambda i, j: (0, j)),
        ],
        out_specs=pl.BlockSpec((PROJ_TM, PROJ_TN), lambda i, j: (i, j)),
        out_shape=jax.ShapeDtypeStruct((n, IN_COLS), BF16),
        scratch_shapes=[pltpu.VMEM((PROJ_TM, D_MODEL), BF16)],
        compiler_params=_params("parallel", "arbitrary"),
        name="proj",
    )(h_res, g, w_bf, col_scale.astype(F32))


def _sb_kernel(q_ref, k_ref, v_ref, z_ref, o_ref):
    t = SB_T
    lane = lax.broadcasted_iota(jnp.int32, (1, LANES), 1)
    rows = lax.broadcasted_iota(jnp.int32, (t, t), 0)
    cols = lax.broadcasted_iota(jnp.int32, (t, t), 1)
    causal = cols < rows
    neg_upper = jnp.where(rows > cols, -1.0, 0.0).astype(BF16)

    def ktile(j):
        return k_ref[pl.ds(pl.multiple_of(j * t, t), t), :]

    def vtile(j):
        return v_ref[pl.ds(pl.multiple_of(j * t, t), t), :]

    def chain(z, masked):
        tt = jnp.log2(1.0 + jnp.exp2(-jnp.abs(z)))
        nl = jnp.maximum(z, 0.0) + tt
        lz = z - nl
        if masked:
            nl = jnp.where(causal, nl, 0.0)
        return lz, _dot(nl.astype(BF16), neg_upper), jnp.sum(nl, axis=-1, keepdims=True)

    def weights(lz, rem, c, masked):
        a = jnp.exp2(lz + rem if c is None else lz + rem + c)
        if masked:
            a = jnp.where(causal, a, 0.0)
        return a.astype(BF16)

    def live(carry):
        return jnp.max(jnp.maximum(carry[1], carry[3])) > SB_DEAD_LOG2

    def near(i, qa, qb):
        k0, v0 = ktile(i), vtile(i)
        prev = jnp.maximum(i - 1, 0)
        k1, v1 = ktile(prev), vtile(prev)
        z0a, z0b = _dot_nt(qa, k0), _dot_nt(qb, k0)
        z1a, z1b = _dot_nt(qa, k1), _dot_nt(qb, k1)
        lz0a, rem0a, rs0a = chain(z0a, True)
        lz0b, rem0b, rs0b = chain(z0b, True)
        acc_a = _dot(weights(lz0a, rem0a, None, True), v0)
        lz1a, rem1a, rs1a = chain(z1a, False)
        acc_b = _dot(weights(lz0b, rem0b, None, True), v0)
        lz1b, rem1b, rs1b = chain(z1b, False)
        off = jnp.where(i > 0, 0.0, SB_OFF_LOG2)
        c1a = off - rs0a
        c1b = off - rs0b
        acc_a = acc_a + _dot(weights(lz1a, rem1a, c1a, False), v1)
        acc_b = acc_b + _dot(weights(lz1b, rem1b, c1b, False), v1)
        return acc_a, c1a - rs1a, acc_b, c1b - rs1b

    def far(i, qa, qb, carry):
        def body(state):
            j, _, (acc_a, c_a, acc_b, c_b) = state
            k, v = ktile(j), vtile(j)
            lz_a, rem_a, rs_a = chain(_dot_nt(qa, k), False)
            lz_b, rem_b, rs_b = chain(_dot_nt(qb, k), False)
            acc_a = acc_a + _dot(weights(lz_a, rem_a, c_a, False), v)
            acc_b = acc_b + _dot(weights(lz_b, rem_b, c_b, False), v)
            new = (acc_a, c_a - rs_a, acc_b, c_b - rs_b)
            return j - 1, live(new), new

        state = lax.while_loop(lambda st: (st[0] >= 0) & st[1], body, (i - 2, live(carry), carry))
        return state[2]

    tiles = []
    for u in range(SB_QTILES):
        q = q_ref[u * t:(u + 1) * t, :]
        qa = jnp.where(lane < SB_HEAD_DIM, q, jnp.zeros_like(q))
        qb = jnp.where(lane < SB_HEAD_DIM, jnp.zeros_like(q), q)
        i = pl.program_id(2) * SB_QTILES + u
        tiles.append((i, qa, qb, near(i, qa, qb)))
    for u, (i, qa, qb, carry) in enumerate(tiles):
        carry = far(i, qa, qb, carry)
        o = jnp.where(lane < SB_HEAD_DIM, carry[0], carry[2])
        gate = _silu(z_ref[u * t:(u + 1) * t, :].astype(F32))
        o_ref[u * t:(u + 1) * t, :] = (o * gate).astype(o_ref.dtype)


def _sb(proj, bsz, seq):
    tq = SB_T * SB_QTILES
    nq = seq // tq
    return pl.pallas_call(
        _sb_kernel,
        grid=(bsz, BRANCH_WIDTH // LANES, nq),
        in_specs=[
            pl.BlockSpec((tq, LANES), lambda b, p, i: (b * nq + i, CB_SB_Q + p)),
            pl.BlockSpec((seq, LANES), lambda b, p, i: (b, CB_SB_K + p)),
            pl.BlockSpec((seq, LANES), lambda b, p, i: (b, CB_SB_V + p)),
            pl.BlockSpec((tq, LANES), lambda b, p, i: (b * nq + i, CB_SB_Z + p)),
        ],
        out_specs=pl.BlockSpec((tq, LANES), lambda b, p, i: (b * nq + i, p)),
        out_shape=jax.ShapeDtypeStruct((bsz * seq, BRANCH_WIDTH), BF16),
        compiler_params=_params("parallel", "parallel", "arbitrary"),
        name="stickbreak",
    )(proj, proj, proj, proj)


def _pool_tile(u_ref, z_ref, w_ref, sc_ref, halo_scr, first_pos):
    tm = u_ref.shape[0]
    pos = first_pos + lax.broadcasted_iota(jnp.int32, (tm, 1), 0)
    outs = []
    for g, w in enumerate(POOL_WINDOWS):
        cols = slice(g * LANES, (g + 1) * LANES)
        u = u_ref[:, cols].astype(F32)
        s = jnp.concatenate([halo_scr[:, cols], u], axis=0)
        k = 1
        while k < w:
            s = s + pltpu.roll(s, k, axis=0)
            k *= 2
        cnt = jnp.minimum(pos + 1, w).astype(F32)
        d = s[POOL_HALO:, :] / cnt - u
        outs.append(_dot(d.astype(BF16), w_ref[g]))
        halo_scr[:, cols] = u[tm - POOL_HALO:, :]
    y = jnp.concatenate(outs, axis=1)
    return (y * sc_ref[...] * _silu(z_ref[...].astype(F32))).astype(BF16)


HG_LEVELS = HG_T.bit_length() - 1


def _hgrn_tables():
    r = np.arange(HG_T)
    tri = (r[None, :] <= r[:, None]).astype(np.float32)
    pair = np.zeros((HG_LEVELS, HG_T, HG_T), np.float32)
    sign = np.zeros((HG_LEVELS, HG_T, LANES), np.float32)
    for l in range(HG_LEVELS):
        m = 1 << l
        second = (r % (2 * m)) >= m
        same = (r[:, None] // (2 * m)) == (r[None, :] // (2 * m))
        pair[l] = same & second[:, None] & ~second[None, :]
        sign[l] = np.where(second, 1.0, -1.0)[:, None]
    return jnp.asarray(tri, BF16), jnp.asarray(pair), jnp.asarray(sign)


def _hgrn_edge(b, level):
    t = b.shape[0]
    m = 1 << level
    if m >= SUBLANES:
        blk = 2 * m
        e = b.reshape(t // blk, blk, LANES)[:, m - 1:m, :]
        return jnp.broadcast_to(e, (t // blk, blk, LANES)).reshape(t, LANES)
    b3 = b.reshape(t // SUBLANES, SUBLANES, LANES)
    if m == 4:
        e = jnp.broadcast_to(b3[:, 3:4, :], b3.shape)
    else:
        sub = lax.broadcasted_iota(jnp.int32, (1, SUBLANES, 1), 1)
        e = jnp.where(sub < 4, b3[:, 1:2, :], b3[:, 5:6, :])
    return e.reshape(t, LANES)


def _hgrn_kernel(q_ref, f_ref, i_ref, z_ref, lb_ref, g_ref, tri_ref, pair_ref, sign_ref,
                 o_ref, st_scr):
    t = HG_T

    @pl.when(pl.program_id(2) == 0)
    def _():
        st_scr[...] = jnp.zeros_like(st_scr)

    def front(cols):
        lb = lb_ref[:, cols]
        sig = jax.nn.sigmoid(f_ref[:, cols].astype(F32))
        f = lb + (1.0 - lb) * sig
        kk = (1.0 - lb) * (1.0 - sig)
        hi, mid, lo = _split3(jnp.log2(f))
        tri = tri_ref[...]
        return f, kk, _dot(tri, hi) + _dot(tri, mid) + _dot(tri, lo)

    def middle(h, cols, f, kk, b):
        q = q_ref[:, cols].astype(F32)
        v_bf = i_ref[:, cols]
        st = st_scr[h]
        b_last = b[t - 1:t, :]
        o = _dot_nt((q * jnp.exp2(b)).astype(BF16), st.astype(BF16))
        kd = (kk * jnp.exp2(b_last - b)).astype(BF16)
        st_scr[h] = st * jnp.exp2(b_last) + _dot_tn(v_bf, kd)
        o = o + jnp.sum(q * kk, axis=-1, keepdims=True) * v_bf.astype(F32)
        amat = jnp.zeros((t, t), F32)
        for level in range(HG_LEVELS):
            if level == 0:
                dec = jnp.where(sign_ref[0] > 0.0, f, 1.0)
            else:
                dec = jnp.exp2((b - _hgrn_edge(b, level)) * sign_ref[level])
            p = _dot_nt((q * dec).astype(BF16), (kk * dec).astype(BF16))
            amat = amat + p * pair_ref[level]
        return o, amat.astype(BF16), v_bf

    def back(cols, o, amat, v_bf):
        o = o + _dot(amat, v_bf)
        ms = jnp.mean(o * o, axis=-1, keepdims=True)
        o = o * lax.rsqrt(ms + EPS) * g_ref[:, cols]
        o_ref[:, cols] = (o * _silu(z_ref[:, cols].astype(F32))).astype(o_ref.dtype)

    heads = [slice(h * LANES, (h + 1) * LANES) for h in range(HG_HEADS_PER_STEP)]
    fronts = [front(c) for c in heads]
    mids = [middle(h, c, *fr) for h, (c, fr) in enumerate(zip(heads, fronts))]
    for c, md in zip(heads, mids):
        back(c, *md)


def _hgrn(proj, lb, norm_g, bsz, seq):
    nt = seq // HG_T
    hps = HG_HEADS_PER_STEP
    width = hps * LANES
    tri, pair, sign = _hgrn_tables()
    blk = lambda cb: pl.BlockSpec((HG_T, width), lambda b, h, i: (b * nt + i, cb // hps + h))
    vec = pl.BlockSpec((1, width), lambda b, h, i: (0, h))
    whole = lambda arr: pl.BlockSpec(arr.shape, lambda b, h, i: (0,) * arr.ndim)
    return pl.pallas_call(
        _hgrn_kernel,
        grid=(bsz, HGRN_HEADS // hps, nt),
        in_specs=[blk(CB_HG_Q), blk(CB_HG_F), blk(CB_HG_I), blk(CB_HG_Z), vec, vec,
                  whole(tri), whole(pair), whole(sign)],
        out_specs=pl.BlockSpec((HG_T, width), lambda b, h, i: (b * nt + i, h)),
        out_shape=jax.ShapeDtypeStruct((bsz * seq, BRANCH_WIDTH), BF16),
        scratch_shapes=[pltpu.VMEM((hps, LANES, LANES), F32)],
        compiler_params=_params("parallel", "parallel", "arbitrary"),
        name="hgrn2",
    )(proj, proj, proj, proj, lb.reshape(1, BRANCH_WIDTH), norm_g.reshape(1, BRANCH_WIDTH),
      tri, pair, sign)


def _merge_kernel(oa_ref, oc_ref, pu_ref, pz_ref, ga_ref, gb_ref, gc_ref, pw_ref, ps_ref,
                  wb_ref, wo_ref, res_ref, fg_ref, out_ref, halo_scr, *, final, tiles_per_seq):
    ti = pl.program_id(0) % tiles_per_seq

    @pl.when(ti == 0)
    def _():
        halo_scr[...] = jnp.zeros_like(halo_scr)

    o_b = _pool_tile(pu_ref, pz_ref, pw_ref, ps_ref, halo_scr, ti * MERGE_TM)
    merged = (jax.nn.sigmoid(ga_ref[...].astype(F32)) * _dot(oa_ref[...], wb_ref[0])
              + jax.nn.sigmoid(gb_ref[...].astype(F32)) * _dot(o_b, wb_ref[1])
              + jax.nn.sigmoid(gc_ref[...].astype(F32)) * _dot(oc_ref[...], wb_ref[2]))
    h = res_ref[...] + _dot(merged.astype(BF16), wo_ref[...])
    if final:
        ms = jnp.mean(h * h, axis=-1, keepdims=True)
        h = h * lax.rsqrt(ms + EPS) * fg_ref[...]
    out_ref[...] = h


def _merge(o_a, o_c, proj, pool_w_bf, pool_scale, wb_bf, wo_bf, h_res, final_g, seq, layer):
    n = h_res.shape[0]
    groups = len(POOL_WINDOWS)
    final = layer == DEPTH - 1
    row = lambda width, blk=0: pl.BlockSpec((MERGE_TM, width), lambda i: (i, blk))
    whole = lambda *shape: pl.BlockSpec((None,) + shape, lambda i: (layer,) + (0,) * len(shape))
    return pl.pallas_call(
        functools.partial(_merge_kernel, final=final, tiles_per_seq=seq // MERGE_TM),
        grid=(n // MERGE_TM,),
        in_specs=[
            row(BRANCH_WIDTH), row(BRANCH_WIDTH),
            row(BRANCH_WIDTH, CB_POOL_U // groups), row(BRANCH_WIDTH, CB_POOL_Z // groups),
            row(D_MODEL, GATE_BLOCK0), row(D_MODEL, GATE_BLOCK0 + 1), row(D_MODEL, GATE_BLOCK0 + 2),
            whole(groups, LANES, LANES), whole(1, BRANCH_WIDTH),
            whole(3, BRANCH_WIDTH, D_MODEL), whole(D_MODEL, D_MODEL),
            row(D_MODEL), pl.BlockSpec((1, D_MODEL), lambda i: (0, 0)),
        ],
        out_specs=row(D_MODEL),
        out_shape=jax.ShapeDtypeStruct((n, D_MODEL), F32),
        scratch_shapes=[pltpu.VMEM((POOL_HALO, BRANCH_WIDTH), F32)],
        compiler_params=_params("arbitrary"),
        name="merge_final" if final else "merge",
    )(o_a, o_c, proj, proj, proj, proj, proj, pool_w_bf,
      pool_scale.reshape(DEPTH, 1, BRANCH_WIDTH), wb_bf, wo_bf, h_res, final_g)


def kernel(x, norm_g, w_in, pool_w, pool_scale, hgrn_lb, hgrn_norm_g, w_branch, w_out, final_g):
    bsz, seq, _ = x.shape
    h_res = x.astype(F32).reshape(bsz * seq, D_MODEL)
    lb_all = jnp.cumsum(jax.nn.softmax(hgrn_lb.astype(F32), axis=0), axis=0)
    lb_all = lb_all - lb_all[:1]
    w_in_bf = w_in.astype(BF16)
    pool_w_bf = pool_w.astype(BF16)
    wb_bf = w_branch.astype(BF16)
    wo_bf = w_out.astype(BF16)
    fg = final_g.astype(F32).reshape(1, D_MODEL)
    for layer in range(DEPTH):
        proj = _proj(h_res, norm_g[layer].astype(F32).reshape(1, D_MODEL), w_in_bf, layer)
        o_a = _sb(proj, bsz, seq)
        o_c = _hgrn(proj, lb_all[layer], hgrn_norm_g[layer].astype(F32), bsz, seq)
        h_res = _merge(o_a, o_c, proj, pool_w_bf, pool_scale.astype(F32),
                       wb_bf, wo_bf, h_res, fg, seq, layer)
    return h_res.reshape(bsz, seq, D_MODEL).astype(x.dtype)
```

```python
import functools
import math

import jax
import jax.numpy as jnp
import numpy as np
from jax import lax
from jax.experimental import pallas as pl
from jax.experimental.pallas import tpu as pltpu

D_MODEL = 1024
DEPTH = 2
BRANCH_WIDTH = D_MODEL // 2
SB_HEAD_DIM = 64
POOL_WINDOWS = (2, 4, 8, 16)
POOL_HALO = 16
HGRN_HEADS = 4
EPS = 1e-6
IN_COLS = 10 * BRANCH_WIDTH + 3 * D_MODEL

LANES = 128
SUBLANES = 8
VMEM_LIMIT = 48 * 1024 * 1024

CB_SB_Q, CB_SB_K, CB_SB_V, CB_SB_Z = 0, 4, 8, 12
CB_POOL_U, CB_POOL_Z = 16, 20
CB_HG_Q, CB_HG_F, CB_HG_I, CB_HG_Z = 24, 28, 32, 36
GATE_BLOCK0 = 5

PH_TM, PH_TN = 512, 1024
SB_T = 256
SB_QTILES = 4
SB_DEAD_LOG2 = -152.0
SB_OFF_LOG2 = -1e30
HG_T = 128
MERGE_TM = 512

F32 = jnp.float32
BF16 = jnp.bfloat16


def _params(*sem):
    return pltpu.CompilerParams(dimension_semantics=sem, vmem_limit_bytes=VMEM_LIMIT)


def _dot(a, b):
    return jnp.dot(a, b, preferred_element_type=F32)


def _dot_nt(a, b):
    return lax.dot_general(a, b, (((1,), (1,)), ((), ())), preferred_element_type=F32)


def _dot_tn(a, b):
    return lax.dot_general(a, b, (((0,), (0,)), ((), ())), preferred_element_type=F32)


def _split3(x):
    hi = x.astype(BF16)
    r = x - hi.astype(F32)
    mid = r.astype(BF16)
    lo = (r - mid.astype(F32)).astype(BF16)
    return hi, mid, lo


def _silu(x):
    return x * jax.nn.sigmoid(x)


def _sb_kernel(q_ref, k_ref, v_ref, z_ref, o_ref):
    t = SB_T
    lane = lax.broadcasted_iota(jnp.int32, (1, LANES), 1)
    rows = lax.broadcasted_iota(jnp.int32, (t, t), 0)
    cols = lax.broadcasted_iota(jnp.int32, (t, t), 1)
    causal = cols < rows
    neg_upper = jnp.where(rows > cols, -1.0, 0.0).astype(BF16)

    def ktile(j):
        return k_ref[pl.ds(pl.multiple_of(j * t, t), t), :]

    def vtile(j):
        return v_ref[pl.ds(pl.multiple_of(j * t, t), t), :]

    def chain(z, masked):
        tt = jnp.log2(1.0 + jnp.exp2(-jnp.abs(z)))
        nl = jnp.maximum(z, 0.0) + tt
        lz = z - nl
        if masked:
            nl = jnp.where(causal, nl, 0.0)
        return lz, _dot(nl.astype(BF16), neg_upper), jnp.sum(nl, axis=-1, keepdims=True)

    def weights(lz, rem, c, masked):
        a = jnp.exp2(lz + rem if c is None else lz + rem + c)
        if masked:
            a = jnp.where(causal, a, 0.0)
        return a.astype(BF16)

    def live(carry):
        return jnp.max(jnp.maximum(carry[1], carry[3])) > SB_DEAD_LOG2

    def near(i, qa, qb):
        k0, v0 = ktile(i), vtile(i)
        prev = jnp.maximum(i - 1, 0)
        k1, v1 = ktile(prev), vtile(prev)
        z0a, z0b = _dot_nt(qa, k0), _dot_nt(qb, k0)
        z1a, z1b = _dot_nt(qa, k1), _dot_nt(qb, k1)
        lz0a, rem0a, rs0a = chain(z0a, True)
        lz0b, rem0b, rs0b = chain(z0b, True)
        acc_a = _dot(weights(lz0a, rem0a, None, True), v0)
        lz1a, rem1a, rs1a = chain(z1a, False)
        acc_b = _dot(weights(lz0b, rem0b, None, True), v0)
        lz1b, rem1b, rs1b = chain(z1b, False)
        off = jnp.where(i > 0, 0.0, SB_OFF_LOG2)
        c1a = off - rs0a
        c1b = off - rs0b
        acc_a = acc_a + _dot(weights(lz1a, rem1a, c1a, False), v1)
        acc_b = acc_b + _dot(weights(lz1b, rem1b, c1b, False), v1)
        return acc_a, c1a - rs1a, acc_b, c1b - rs1b

    def far(i, qa, qb, carry):
        def body(state):
            j, _, (acc_a, c_a, acc_b, c_b) = state
            k, v = ktile(j), vtile(j)
            lz_a, rem_a, rs_a = chain(_dot_nt(qa, k), False)
            lz_b, rem_b, rs_b = chain(_dot_nt(qb, k), False)
            acc_a = acc_a + _dot(weights(lz_a, rem_a, c_a, False), v)
            acc_b = acc_b + _dot(weights(lz_b, rem_b, c_b, False), v)
            new = (acc_a, c_a - rs_a, acc_b, c_b - rs_b)
            return j - 1, live(new), new

        state = lax.while_loop(lambda st: (st[0] >= 0) & st[1], body, (i - 2, live(carry), carry))
        return state[2]

    tiles = []
    for u in range(SB_QTILES):
        q = q_ref[u * t:(u + 1) * t, :]
        qa = jnp.where(lane < SB_HEAD_DIM, q, jnp.zeros_like(q))
        qb = jnp.where(lane < SB_HEAD_DIM, jnp.zeros_like(q), q)
        i = pl.program_id(2) * SB_QTILES + u
        tiles.append((i, qa, qb, near(i, qa, qb)))
    for u, (i, qa, qb, carry) in enumerate(tiles):
        carry = far(i, qa, qb, carry)
        o = jnp.where(lane < SB_HEAD_DIM, carry[0], carry[2])
        gate = _silu(z_ref[u * t:(u + 1) * t, :].astype(F32))
        o_ref[u * t:(u + 1) * t, :] = (o * gate).astype(o_ref.dtype)


def _sb(proj, bsz, seq):
    tq = SB_T * SB_QTILES
    nq = seq // tq
    return pl.pallas_call(
        _sb_kernel,
        grid=(bsz, BRANCH_WIDTH // LANES, nq),
        in_specs=[
            pl.BlockSpec((tq, LANES), lambda b, p, i: (b * nq + i, CB_SB_Q + p)),
            pl.BlockSpec((seq, LANES), lambda b, p, i: (b, CB_SB_K + p)),
            pl.BlockSpec((seq, LANES), lambda b, p, i: (b, CB_SB_V + p)),
            pl.BlockSpec((tq, LANES), lambda b, p, i: (b * nq + i, CB_SB_Z + p)),
        ],
        out_specs=pl.BlockSpec((tq, LANES), lambda b, p, i: (b * nq + i, p)),
        out_shape=jax.ShapeDtypeStruct((bsz * seq, BRANCH_WIDTH), BF16),
        compiler_params=_params("parallel", "parallel", "arbitrary"),
        name="stickbreak",
    )(proj, proj, proj, proj)


def _pool_tile(u_ref, z_ref, w_ref, sc_ref, halo_scr, first_pos):
    tm = u_ref.shape[0]
    pos = first_pos + lax.broadcasted_iota(jnp.int32, (tm, 1), 0)
    outs = []
    for g, w in enumerate(POOL_WINDOWS):
        cols = slice(g * LANES, (g + 1) * LANES)
        u = u_ref[:, cols].astype(F32)
        s = jnp.concatenate([halo_scr[:, cols], u], axis=0)
        k = 1
        while k < w:
            s = s + pltpu.roll(s, k, axis=0)
            k *= 2
        cnt = jnp.minimum(pos + 1, w).astype(F32)
        d = s[POOL_HALO:, :] / cnt - u
        outs.append(_dot(d.astype(BF16), w_ref[g]))
        halo_scr[:, cols] = u[tm - POOL_HALO:, :]
    y = jnp.concatenate(outs, axis=1)
    return (y * sc_ref[...] * _silu(z_ref[...].astype(F32))).astype(BF16)


HG_LEVELS = HG_T.bit_length() - 1


def _hgrn_tables():
    r = np.arange(HG_T)
    tri = (r[None, :] <= r[:, None]).astype(np.float32)
    pair = np.zeros((HG_LEVELS, HG_T, HG_T), np.float32)
    sign = np.zeros((HG_LEVELS, HG_T, LANES), np.float32)
    for l in range(HG_LEVELS):
        m = 1 << l
        second = (r % (2 * m)) >= m
        same = (r[:, None] // (2 * m)) == (r[None, :] // (2 * m))
        pair[l] = same & second[:, None] & ~second[None, :]
        sign[l] = np.where(second, 1.0, -1.0)[:, None]
    return jnp.asarray(tri, BF16), jnp.asarray(pair), jnp.asarray(sign)


def _hgrn_edge(b, level):
    t = b.shape[0]
    m = 1 << level
    if m >= SUBLANES:
        blk = 2 * m
        e = b.reshape(t // blk, blk, LANES)[:, m - 1:m, :]
        return jnp.broadcast_to(e, (t // blk, blk, LANES)).reshape(t, LANES)
    b3 = b.reshape(t // SUBLANES, SUBLANES, LANES)
    if m == 4:
        e = jnp.broadcast_to(b3[:, 3:4, :], b3.shape)
    else:
        sub = lax.broadcasted_iota(jnp.int32, (1, SUBLANES, 1), 1)
        e = jnp.where(sub < 4, b3[:, 1:2, :], b3[:, 5:6, :])
    return e.reshape(t, LANES)


def _hgrn_stages(blk, rows, lb_ref, g_ref, tri_ref, pair_ref, sign_ref, o_ref):
    t = HG_T

    def col(part, cols):
        return blk[rows, part * BRANCH_WIDTH + cols.start:part * BRANCH_WIDTH + cols.stop]

    def front(cols):
        lb = lb_ref[:, cols]
        sig = jax.nn.sigmoid(col(1, cols).astype(F32))
        f = lb + (1.0 - lb) * sig
        kk = (1.0 - lb) * (1.0 - sig)
        hi, mid, lo = _split3(jnp.log2(f))
        tri = tri_ref[...]
        return f, kk, _dot(tri, hi) + _dot(tri, mid) + _dot(tri, lo)

    def middle(cols, st, f, kk, b):
        q = col(0, cols).astype(F32)
        v_bf = col(2, cols)
        b_last = b[t - 1:t, :]
        o = _dot_nt((q * jnp.exp2(b)).astype(BF16), st.astype(BF16))
        kd = (kk * jnp.exp2(b_last - b)).astype(BF16)
        st = st * jnp.exp2(b_last) + _dot_tn(v_bf, kd)
        o = o + jnp.sum(q * kk, axis=-1, keepdims=True) * v_bf.astype(F32)
        amat = jnp.zeros((t, t), F32)
        for level in range(HG_LEVELS):
            if level == 0:
                dec = jnp.where(sign_ref[0] > 0.0, f, 1.0)
            else:
                dec = jnp.exp2((b - _hgrn_edge(b, level)) * sign_ref[level])
            p = _dot_nt((q * dec).astype(BF16), (kk * dec).astype(BF16))
            amat = amat + p * pair_ref[level]
        return st, o, amat.astype(BF16), v_bf

    def back(cols, o, amat, v_bf):
        o = o + _dot(amat, v_bf)
        ms = jnp.mean(o * o, axis=-1, keepdims=True)
        o = o * lax.rsqrt(ms + EPS) * g_ref[:, cols]
        o_ref[rows, cols] = (o * _silu(col(3, cols).astype(F32))).astype(o_ref.dtype)

    return front, middle, back


def _proj_hgrn_kernel(x_ref, g_ref, w_ref, cs_ref, lb_ref, ng_ref, tri_ref, pair_ref, sign_ref,
                      proj_ref, oc_ref, hg_scr, st_scr, *, blocks_per_seq):
    n = pl.program_id(0)
    slot = n % 2

    @pl.when(n == 0)
    def _():
        hg_scr[...] = jnp.zeros_like(hg_scr)
        st_scr[...] = jnp.zeros_like(st_scr)

    prev = hg_scr.at[1 - slot]
    nxt = hg_scr.at[slot]
    fresh = (n + blocks_per_seq - 1) % blocks_per_seq == 0
    heads = [slice(h * LANES, (h + 1) * LANES) for h in range(HGRN_HEADS)]
    states = [jnp.where(fresh, 0.0, st_scr[h]) for h in range(HGRN_HEADS)]

    x = x_ref[...]
    ms = jnp.mean(x * x, axis=-1, keepdims=True)
    hn = (x * lax.rsqrt(ms + EPS) * g_ref[...]).astype(BF16)
    hg0 = CB_HG_Q * LANES

    def project(c):
        cols = slice(c * PH_TN, (c + 1) * PH_TN)
        y = (_dot(hn, w_ref[:, cols]) * cs_ref[:, cols]).astype(BF16)
        proj_ref[:, cols] = y
        if hg0 <= cols.start < hg0 + 4 * BRANCH_WIDTH:
            nxt[:, cols.start - hg0:cols.stop - hg0] = y

    chunks = iter(range(IN_COLS // PH_TN))
    per_tile = (IN_COLS // PH_TN) // (PH_TM // HG_T)
    for k in range(PH_TM // HG_T):
        rows = slice(k * HG_T, (k + 1) * HG_T)
        front, middle, back = _hgrn_stages(prev, rows, lb_ref, ng_ref, tri_ref, pair_ref, sign_ref,
                                           oc_ref)
        fronts = [front(c) for c in heads]
        for _ in range(per_tile // 2):
            project(next(chunks))
        mids = [middle(c, st, *fr) for c, st, fr in zip(heads, states, fronts)]
        states = [md[0] for md in mids]
        for _ in range(per_tile - per_tile // 2):
            project(next(chunks))
        for c, md in zip(heads, mids):
            back(c, *md[1:])
    for c in chunks:
        project(c)
    for h in range(HGRN_HEADS):
        st_scr[h] = states[h]


def _proj_hgrn(h_res, g, w_bf, layer, lb, norm_g, seq):
    n = h_res.shape[0]
    nb = n // PH_TM
    col = lax.broadcasted_iota(jnp.int32, (1, IN_COLS), 1)
    col_scale = jnp.where(col < BRANCH_WIDTH, math.log2(math.e) / math.sqrt(SB_HEAD_DIM), 1.0)
    tri, pair, sign = _hgrn_tables()
    whole = lambda arr: pl.BlockSpec(arr.shape, lambda i: (0,) * arr.ndim)
    cur = lambda i: (jnp.minimum(i, nb - 1), 0)
    return pl.pallas_call(
        functools.partial(_proj_hgrn_kernel, blocks_per_seq=seq // PH_TM),
        grid=(nb + 1,),
        in_specs=[
            pl.BlockSpec((PH_TM, D_MODEL), cur),
            pl.BlockSpec((1, D_MODEL), lambda i: (0, 0)),
            pl.BlockSpec((None, D_MODEL, IN_COLS), lambda i: (layer, 0, 0),
                         pipeline_mode=pl.Buffered(1)),
            pl.BlockSpec((1, IN_COLS), lambda i: (0, 0)),
            pl.BlockSpec((1, BRANCH_WIDTH), lambda i: (0, 0)),
            pl.BlockSpec((1, BRANCH_WIDTH), lambda i: (0, 0)),
            whole(tri), whole(pair), whole(sign),
        ],
        out_specs=[
            pl.BlockSpec((PH_TM, IN_COLS), cur),
            pl.BlockSpec((PH_TM, BRANCH_WIDTH), lambda i: (jnp.maximum(i - 1, 0), 0)),
        ],
        out_shape=[jax.ShapeDtypeStruct((n, IN_COLS), BF16),
                   jax.ShapeDtypeStruct((n, BRANCH_WIDTH), BF16)],
        scratch_shapes=[pltpu.VMEM((2, PH_TM, 4 * BRANCH_WIDTH), BF16),
                        pltpu.VMEM((HGRN_HEADS, LANES, LANES), F32)],
        compiler_params=_params("arbitrary"),
        name="proj_hgrn2",
    )(h_res, g, w_bf, col_scale.astype(F32), lb.reshape(1, BRANCH_WIDTH),
      norm_g.reshape(1, BRANCH_WIDTH), tri, pair, sign)


def _merge_kernel(oa_ref, oc_ref, pu_ref, pz_ref, ga_ref, gb_ref, gc_ref, pw_ref, ps_ref,
                  wb_ref, wo_ref, res_ref, fg_ref, out_ref, halo_scr, *, final, tiles_per_seq):
    ti = pl.program_id(0) % tiles_per_seq

    @pl.when(ti == 0)
    def _():
        halo_scr[...] = jnp.zeros_like(halo_scr)

    o_b = _pool_tile(pu_ref, pz_ref, pw_ref, ps_ref, halo_scr, ti * MERGE_TM)
    merged = (jax.nn.sigmoid(ga_ref[...].astype(F32)) * _dot(oa_ref[...], wb_ref[0])
              + jax.nn.sigmoid(gb_ref[...].astype(F32)) * _dot(o_b, wb_ref[1])
              + jax.nn.sigmoid(gc_ref[...].astype(F32)) * _dot(oc_ref[...], wb_ref[2]))
    h = res_ref[...] + _dot(merged.astype(BF16), wo_ref[...])
    if final:
        ms = jnp.mean(h * h, axis=-1, keepdims=True)
        h = h * lax.rsqrt(ms + EPS) * fg_ref[...]
    out_ref[...] = h


def _merge(o_a, o_c, proj, pool_w_bf, pool_scale, wb_bf, wo_bf, h_res, final_g, seq, layer):
    n = h_res.shape[0]
    groups = len(POOL_WINDOWS)
    final = layer == DEPTH - 1
    row = lambda width, blk=0: pl.BlockSpec((MERGE_TM, width), lambda i: (i, blk))
    whole = lambda *shape: pl.BlockSpec((None,) + shape, lambda i: (layer,) + (0,) * len(shape))
    return pl.pallas_call(
        functools.partial(_merge_kernel, final=final, tiles_per_seq=seq // MERGE_TM),
        grid=(n // MERGE_TM,),
        in_specs=[
            row(BRANCH_WIDTH), row(BRANCH_WIDTH),
            row(BRANCH_WIDTH, CB_POOL_U // groups), row(BRANCH_WIDTH, CB_POOL_Z // groups),
            row(D_MODEL, GATE_BLOCK0), row(D_MODEL, GATE_BLOCK0 + 1), row(D_MODEL, GATE_BLOCK0 + 2),
            whole(groups, LANES, LANES), whole(1, BRANCH_WIDTH),
            whole(3, BRANCH_WIDTH, D_MODEL), whole(D_MODEL, D_MODEL),
            row(D_MODEL), pl.BlockSpec((1, D_MODEL), lambda i: (0, 0)),
        ],
        out_specs=row(D_MODEL),
        out_shape=jax.ShapeDtypeStruct((n, D_MODEL), F32),
        scratch_shapes=[pltpu.VMEM((POOL_HALO, BRANCH_WIDTH), F32)],
        compiler_params=_params("arbitrary"),
        name="merge_final" if final else "merge",
    )(o_a, o_c, proj, proj, proj, proj, proj, pool_w_bf,
      pool_scale.reshape(DEPTH, 1, BRANCH_WIDTH), wb_bf, wo_bf, h_res, final_g)


def kernel(x, norm_g, w_in, pool_w, pool_scale, hgrn_lb, hgrn_norm_g, w_branch, w_out, final_g):
    bsz, seq, _ = x.shape
    h_res = x.astype(F32).reshape(bsz * seq, D_MODEL)
    lb_all = jnp.cumsum(jax.nn.softmax(hgrn_lb.astype(F32), axis=0), axis=0)
    lb_all = lb_all - lb_all[:1]
    w_in_bf = w_in.astype(BF16)
    pool_w_bf = pool_w.astype(BF16)
    wb_bf = w_branch.astype(BF16)
    wo_bf = w_out.astype(BF16)
    fg = final_g.astype(F32).reshape(1, D_MODEL)
    for layer in range(DEPTH):
        proj, o_c = _proj_hgrn(h_res, norm_g[layer].astype(F32).reshape(1, D_MODEL), w_in_bf, layer,
                               lb_all[layer], hgrn_norm_g[layer].astype(F32), seq)
        o_a = _sb(proj, bsz, seq)
        h_res = _merge(o_a, o_c, proj, pool_w_bf, pool_scale.astype(F32),
                       wb_bf, wo_bf, h_res, fg, seq, layer)
    return h_res.reshape(bsz, seq, D_MODEL).astype(x.dtype)
```

```python
import functools
import math

import jax
import jax.numpy as jnp
import numpy as np
from jax import lax
from jax.experimental import pallas as pl
from jax.experimental.pallas import tpu as pltpu

D_MODEL = 1024
DEPTH = 2
BRANCH_WIDTH = D_MODEL // 2
SB_HEAD_DIM = 64
POOL_WINDOWS = (2, 4, 8, 16)
POOL_HALO = 16
HGRN_HEADS = 4
EPS = 1e-6
IN_COLS = 10 * BRANCH_WIDTH + 3 * D_MODEL

LANES = 128
SUBLANES = 8
VMEM_LIMIT = 48 * 1024 * 1024

CB_SB_Q, CB_SB_K, CB_SB_V, CB_SB_Z = 0, 4, 8, 12
CB_POOL_U, CB_POOL_Z = 16, 20
CB_HG_Q, CB_HG_F, CB_HG_I, CB_HG_Z = 24, 28, 32, 36
GATE_BLOCK0 = 5

PH_TM, PH_TN = 512, 256
PH_TICKS_PER_CHUNK = 4
SB_T = 256
SB_QTILES = 4
SB_DEAD_LOG2 = -152.0
SB_OFF_LOG2 = -1e30
SB_Z_CAP = 126.0
HG_T = 128
MERGE_TM = 512

F32 = jnp.float32
BF16 = jnp.bfloat16


def _params(*sem):
    return pltpu.CompilerParams(dimension_semantics=sem, vmem_limit_bytes=VMEM_LIMIT)


def _dot(a, b):
    return jnp.dot(a, b, preferred_element_type=F32)


def _dot_nt(a, b):
    return lax.dot_general(a, b, (((1,), (1,)), ((), ())), preferred_element_type=F32)


def _dot_tn(a, b):
    return lax.dot_general(a, b, (((0,), (0,)), ((), ())), preferred_element_type=F32)


def _split3(x):
    hi = x.astype(BF16)
    r = x - hi.astype(F32)
    mid = r.astype(BF16)
    lo = (r - mid.astype(F32)).astype(BF16)
    return hi, mid, lo


def _silu(x):
    return x * jax.nn.sigmoid(x)


def _sb_kernel(q_ref, k_ref, v_ref, z_ref, o_ref):
    t = SB_T
    lane = lax.broadcasted_iota(jnp.int32, (1, LANES), 1)
    rows = lax.broadcasted_iota(jnp.int32, (t, t), 0)
    cols = lax.broadcasted_iota(jnp.int32, (t, t), 1)
    causal = cols < rows
    neg_upper = jnp.where(rows > cols, -1.0, 0.0).astype(BF16)

    def ktile(j):
        return k_ref[pl.ds(pl.multiple_of(j * t, t), t), :]

    def vtile(j):
        return v_ref[pl.ds(pl.multiple_of(j * t, t), t), :]

    def chain(z, masked):
        nl = jnp.maximum(z, jnp.log2(1.0 + jnp.exp2(jnp.minimum(z, SB_Z_CAP))))
        lz = z - nl
        if masked:
            nl = jnp.where(causal, nl, 0.0)
        return lz, _dot(nl.astype(BF16), neg_upper), jnp.sum(nl, axis=-1, keepdims=True)

    def weights(lz, rem, c, masked):
        a = jnp.exp2(lz + rem if c is None else lz + rem + c)
        if masked:
            a = jnp.where(causal, a, 0.0)
        return a.astype(BF16)

    def live(carry):
        return jnp.max(jnp.maximum(carry[1], carry[3])) > SB_DEAD_LOG2

    def near(i, qa, qb):
        k0, v0 = ktile(i), vtile(i)
        prev = jnp.maximum(i - 1, 0)
        k1, v1 = ktile(prev), vtile(prev)
        z0a, z0b = _dot_nt(qa, k0), _dot_nt(qb, k0)
        z1a, z1b = _dot_nt(qa, k1), _dot_nt(qb, k1)
        lz0a, rem0a, rs0a = chain(z0a, True)
        lz0b, rem0b, rs0b = chain(z0b, True)
        acc_a = _dot(weights(lz0a, rem0a, None, True), v0)
        lz1a, rem1a, rs1a = chain(z1a, False)
        acc_b = _dot(weights(lz0b, rem0b, None, True), v0)
        lz1b, rem1b, rs1b = chain(z1b, False)
        off = jnp.where(i > 0, 0.0, SB_OFF_LOG2)
        c1a = off - rs0a
        c1b = off - rs0b
        acc_a = acc_a + _dot(weights(lz1a, rem1a, c1a, False), v1)
        acc_b = acc_b + _dot(weights(lz1b, rem1b, c1b, False), v1)
        return acc_a, c1a - rs1a, acc_b, c1b - rs1b

    def far(i, qa, qb, carry):
        def body(state):
            j, _, (acc_a, c_a, acc_b, c_b) = state
            k, v = ktile(j), vtile(j)
            lz_a, rem_a, rs_a = chain(_dot_nt(qa, k), False)
            lz_b, rem_b, rs_b = chain(_dot_nt(qb, k), False)
            acc_a = acc_a + _dot(weights(lz_a, rem_a, c_a, False), v)
            acc_b = acc_b + _dot(weights(lz_b, rem_b, c_b, False), v)
            new = (acc_a, c_a - rs_a, acc_b, c_b - rs_b)
            return j - 1, live(new), new

        state = lax.while_loop(lambda st: (st[0] >= 0) & st[1], body, (i - 2, live(carry), carry))
        return state[2]

    tiles = []
    for u in range(SB_QTILES):
        q = q_ref[u * t:(u + 1) * t, :]
        qa = jnp.where(lane < SB_HEAD_DIM, q, jnp.zeros_like(q))
        qb = jnp.where(lane < SB_HEAD_DIM, jnp.zeros_like(q), q)
        i = pl.program_id(2) * SB_QTILES + u
        tiles.append((i, qa, qb, near(i, qa, qb)))
    for u, (i, qa, qb, carry) in enumerate(tiles):
        carry = far(i, qa, qb, carry)
        o = jnp.where(lane < SB_HEAD_DIM, carry[0], carry[2])
        gate = _silu(z_ref[u * t:(u + 1) * t, :].astype(F32))
        o_ref[u * t:(u + 1) * t, :] = (o * gate).astype(o_ref.dtype)


def _sb(proj, bsz, seq):
    tq = SB_T * SB_QTILES
    nq = seq // tq
    return pl.pallas_call(
        _sb_kernel,
        grid=(bsz, BRANCH_WIDTH // LANES, nq),
        in_specs=[
            pl.BlockSpec((tq, LANES), lambda b, p, i: (b * nq + i, CB_SB_Q + p)),
            pl.BlockSpec((seq, LANES), lambda b, p, i: (b, CB_SB_K + p)),
            pl.BlockSpec((seq, LANES), lambda b, p, i: (b, CB_SB_V + p)),
            pl.BlockSpec((tq, LANES), lambda b, p, i: (b * nq + i, CB_SB_Z + p)),
        ],
        out_specs=pl.BlockSpec((tq, LANES), lambda b, p, i: (b * nq + i, p)),
        out_shape=jax.ShapeDtypeStruct((bsz * seq, BRANCH_WIDTH), BF16),
        compiler_params=_params("parallel", "parallel", "arbitrary"),
        name="stickbreak",
    )(proj, proj, proj, proj)


def _pool_tile(u_ref, z_ref, w_ref, sc_ref, halo_scr, first_pos):
    tm = u_ref.shape[0]
    pos = first_pos + lax.broadcasted_iota(jnp.int32, (tm, 1), 0)
    outs = []
    for g, w in enumerate(POOL_WINDOWS):
        cols = slice(g * LANES, (g + 1) * LANES)
        u = u_ref[:, cols].astype(F32)
        s = jnp.concatenate([halo_scr[:, cols], u], axis=0)
        k = 1
        while k < w:
            s = s + pltpu.roll(s, k, axis=0)
            k *= 2
        cnt = jnp.minimum(pos + 1, w).astype(F32)
        d = s[POOL_HALO:, :] / cnt - u
        outs.append(_dot(d.astype(BF16), w_ref[g]))
        halo_scr[:, cols] = u[tm - POOL_HALO:, :]
    y = jnp.concatenate(outs, axis=1)
    return (y * sc_ref[...] * _silu(z_ref[...].astype(F32))).astype(BF16)


HG_LEVELS = HG_T.bit_length() - 1


def _hgrn_tables():
    r = np.arange(HG_T)
    tri = (r[None, :] <= r[:, None]).astype(np.float32)
    pair = np.zeros((HG_LEVELS, HG_T, HG_T), np.float32)
    sign = np.zeros((HG_LEVELS, HG_T, LANES), np.float32)
    for l in range(HG_LEVELS):
        m = 1 << l
        second = (r % (2 * m)) >= m
        same = (r[:, None] // (2 * m)) == (r[None, :] // (2 * m))
        pair[l] = same & second[:, None] & ~second[None, :]
        sign[l] = np.where(second, 1.0, -1.0)[:, None]
    return jnp.asarray(tri, BF16), jnp.asarray(pair), jnp.asarray(sign)


def _hgrn_edge(b, level):
    t = b.shape[0]
    m = 1 << level
    if m >= SUBLANES:
        blk = 2 * m
        e = b.reshape(t // blk, blk, LANES)[:, m - 1:m, :]
        return jnp.broadcast_to(e, (t // blk, blk, LANES)).reshape(t, LANES)
    b3 = b.reshape(t // SUBLANES, SUBLANES, LANES)
    if m == 4:
        e = jnp.broadcast_to(b3[:, 3:4, :], b3.shape)
    else:
        sub = lax.broadcasted_iota(jnp.int32, (1, SUBLANES, 1), 1)
        e = jnp.where(sub < 4, b3[:, 1:2, :], b3[:, 5:6, :])
    return e.reshape(t, LANES)


def _hgrn_stages(blk, rows, lb_ref, g_ref, tri_ref, pair_ref, sign_ref, o_ref, tick):
    t = HG_T

    def col(part, cols):
        return blk[rows, part * BRANCH_WIDTH + cols.start:part * BRANCH_WIDTH + cols.stop]

    def front(cols):
        lb = lb_ref[:, cols]
        sig = jax.nn.sigmoid(col(1, cols).astype(F32))
        f = lb + (1.0 - lb) * sig
        kk = (1.0 - lb) * (1.0 - sig)
        hi, mid, lo = _split3(jnp.log2(f))
        tri = tri_ref[...]
        b = _dot(tri, hi) + _dot(tri, mid) + _dot(tri, lo)
        tick()
        return f, kk, b

    def middle(cols, st, f, kk, b):
        q = col(0, cols).astype(F32)
        v_bf = col(2, cols)
        b_last = b[t - 1:t, :]
        o = _dot_nt((q * jnp.exp2(b)).astype(BF16), st.astype(BF16))
        kd = (kk * jnp.exp2(b_last - b)).astype(BF16)
        st = st * jnp.exp2(b_last) + _dot_tn(v_bf, kd)
        o = o + jnp.sum(q * kk, axis=-1, keepdims=True) * v_bf.astype(F32)
        amat = jnp.zeros((t, t), F32)
        for level in range(HG_LEVELS):
            if level == 0:
                dec = jnp.where(sign_ref[0] > 0.0, f, 1.0)
            else:
                dec = jnp.exp2((b - _hgrn_edge(b, level)) * sign_ref[level])
            p = _dot_nt((q * dec).astype(BF16), (kk * dec).astype(BF16))
            amat = amat + p * pair_ref[level]
            tick()
        return st, o, amat.astype(BF16), v_bf

    def back(cols, o, amat, v_bf):
        o = o + _dot(amat, v_bf)
        ms = jnp.mean(o * o, axis=-1, keepdims=True)
        o = o * lax.rsqrt(ms + EPS) * g_ref[:, cols]
        o_ref[rows, cols] = (o * _silu(col(3, cols).astype(F32))).astype(o_ref.dtype)
        tick()

    return front, middle, back


def _proj_hgrn_kernel(x_ref, g_ref, w_ref, cs_ref, lb_ref, ng_ref, tri_ref, pair_ref, sign_ref,
                      proj_ref, oc_ref, hg_scr, st_scr, *, blocks_per_seq):
    n = pl.program_id(0)
    slot = n % 2

    @pl.when(n == 0)
    def _():
        hg_scr[...] = jnp.zeros_like(hg_scr)
        st_scr[...] = jnp.zeros_like(st_scr)

    prev = hg_scr.at[1 - slot]
    nxt = hg_scr.at[slot]
    fresh = (n + blocks_per_seq - 1) % blocks_per_seq == 0
    heads = [slice(h * LANES, (h + 1) * LANES) for h in range(HGRN_HEADS)]
    states = [jnp.where(fresh, 0.0, st_scr[h]) for h in range(HGRN_HEADS)]

    x = x_ref[...]
    ms = jnp.mean(x * x, axis=-1, keepdims=True)
    hn = (x * lax.rsqrt(ms + EPS) * g_ref[...]).astype(BF16)
    hg0 = CB_HG_Q * LANES

    def project(c):
        cols = slice(c * PH_TN, (c + 1) * PH_TN)
        y = (_dot(hn, w_ref[:, cols]) * cs_ref[:, cols]).astype(BF16)
        proj_ref[:, cols] = y
        if hg0 <= cols.start < hg0 + 4 * BRANCH_WIDTH:
            nxt[:, cols.start - hg0:cols.stop - hg0] = y

    chunks = iter(range(IN_COLS // PH_TN))
    ticks = [0]

    def tick():
        ticks[0] += 1
        if ticks[0] % PH_TICKS_PER_CHUNK == 0:
            c = next(chunks, None)
            if c is not None:
                project(c)

    for k in range(PH_TM // HG_T):
        rows = slice(k * HG_T, (k + 1) * HG_T)
        front, middle, back = _hgrn_stages(prev, rows, lb_ref, ng_ref, tri_ref, pair_ref, sign_ref,
                                           oc_ref, tick)
        fronts = [front(c) for c in heads]
        mids = [middle(c, st, *fr) for c, st, fr in zip(heads, states, fronts)]
        states = [md[0] for md in mids]
        for c, md in zip(heads, mids):
            back(c, *md[1:])
    for c in chunks:
        project(c)
    for h in range(HGRN_HEADS):
        st_scr[h] = states[h]


def _proj_hgrn(h_res, g, w_bf, layer, lb, norm_g, seq):
    n = h_res.shape[0]
    nb = n // PH_TM
    col = lax.broadcasted_iota(jnp.int32, (1, IN_COLS), 1)
    col_scale = jnp.where(col < BRANCH_WIDTH, math.log2(math.e) / math.sqrt(SB_HEAD_DIM), 1.0)
    tri, pair, sign = _hgrn_tables()
    whole = lambda arr: pl.BlockSpec(arr.shape, lambda i: (0,) * arr.ndim)
    cur = lambda i: (jnp.minimum(i, nb - 1), 0)
    return pl.pallas_call(
        functools.partial(_proj_hgrn_kernel, blocks_per_seq=seq // PH_TM),
        grid=(nb + 1,),
        in_specs=[
            pl.BlockSpec((PH_TM, D_MODEL), cur),
            pl.BlockSpec((1, D_MODEL), lambda i: (0, 0)),
            pl.BlockSpec((None, D_MODEL, IN_COLS), lambda i: (layer, 0, 0),
                         pipeline_mode=pl.Buffered(1)),
            pl.BlockSpec((1, IN_COLS), lambda i: (0, 0)),
            pl.BlockSpec((1, BRANCH_WIDTH), lambda i: (0, 0)),
            pl.BlockSpec((1, BRANCH_WIDTH), lambda i: (0, 0)),
            whole(tri), whole(pair), whole(sign),
        ],
        out_specs=[
            pl.BlockSpec((PH_TM, IN_COLS), cur),
            pl.BlockSpec((PH_TM, BRANCH_WIDTH), lambda i: (jnp.maximum(i - 1, 0), 0)),
        ],
        out_shape=[jax.ShapeDtypeStruct((n, IN_COLS), BF16),
                   jax.ShapeDtypeStruct((n, BRANCH_WIDTH), BF16)],
        scratch_shapes=[pltpu.VMEM((2, PH_TM, 4 * BRANCH_WIDTH), BF16),
                        pltpu.VMEM((HGRN_HEADS, LANES, LANES), F32)],
        compiler_params=_params("arbitrary"),
        name="proj_hgrn2",
    )(h_res, g, w_bf, col_scale.astype(F32), lb.reshape(1, BRANCH_WIDTH),
      norm_g.reshape(1, BRANCH_WIDTH), tri, pair, sign)


def _merge_kernel(oa_ref, oc_ref, pu_ref, pz_ref, ga_ref, gb_ref, gc_ref, pw_ref, ps_ref,
                  wb_ref, wo_ref, res_ref, fg_ref, out_ref, halo_scr, *, final, tiles_per_seq):
    ti = pl.program_id(0) % tiles_per_seq

    @pl.when(ti == 0)
    def _():
        halo_scr[...] = jnp.zeros_like(halo_scr)

    o_b = _pool_tile(pu_ref, pz_ref, pw_ref, ps_ref, halo_scr, ti * MERGE_TM)
    merged = (jax.nn.sigmoid(ga_ref[...].astype(F32)) * _dot(oa_ref[...], wb_ref[0])
              + jax.nn.sigmoid(gb_ref[...].astype(F32)) * _dot(o_b, wb_ref[1])
              + jax.nn.sigmoid(gc_ref[...].astype(F32)) * _dot(oc_ref[...], wb_ref[2]))
    h = res_ref[...] + _dot(merged.astype(BF16), wo_ref[...])
    if final:
        ms = jnp.mean(h * h, axis=-1, keepdims=True)
        h = h * lax.rsqrt(ms + EPS) * fg_ref[...]
    out_ref[...] = h


def _merge(o_a, o_c, proj, pool_w_bf, pool_scale, wb_bf, wo_bf, h_res, final_g, seq, layer):
    n = h_res.shape[0]
    groups = len(POOL_WINDOWS)
    final = layer == DEPTH - 1
    row = lambda width, blk=0: pl.BlockSpec((MERGE_TM, width), lambda i: (i, blk))
    whole = lambda *shape: pl.BlockSpec((None,) + shape, lambda i: (layer,) + (0,) * len(shape))
    return pl.pallas_call(
        functools.partial(_merge_kernel, final=final, tiles_per_seq=seq // MERGE_TM),
        grid=(n // MERGE_TM,),
        in_specs=[
            row(BRANCH_WIDTH), row(BRANCH_WIDTH),
            row(BRANCH_WIDTH, CB_POOL_U // groups), row(BRANCH_WIDTH, CB_POOL_Z // groups),
            row(D_MODEL, GATE_BLOCK0), row(D_MODEL, GATE_BLOCK0 + 1), row(D_MODEL, GATE_BLOCK0 + 2),
            whole(groups, LANES, LANES), whole(1, BRANCH_WIDTH),
            whole(3, BRANCH_WIDTH, D_MODEL), whole(D_MODEL, D_MODEL),
            row(D_MODEL), pl.BlockSpec((1, D_MODEL), lambda i: (0, 0)),
        ],
        out_specs=row(D_MODEL),
        out_shape=jax.ShapeDtypeStruct((n, D_MODEL), F32),
        scratch_shapes=[pltpu.VMEM((POOL_HALO, BRANCH_WIDTH), F32)],
        compiler_params=_params("arbitrary"),
        name="merge_final" if final else "merge",
    )(o_a, o_c, proj, proj, proj, proj, proj, pool_w_bf,
      pool_scale.reshape(DEPTH, 1, BRANCH_WIDTH), wb_bf, wo_bf, h_res, final_g)


def kernel(x, norm_g, w_in, pool_w, pool_scale, hgrn_lb, hgrn_norm_g, w_branch, w_out, final_g):
    bsz, seq, _ = x.shape
    h_res = x.astype(F32).reshape(bsz * seq, D_MODEL)
    lb_all = jnp.cumsum(jax.nn.softmax(hgrn_lb.astype(F32), axis=0), axis=0)
    lb_all = lb_all - lb_all[:1]
    w_in_bf = w_in.astype(BF16)
    pool_w_bf = pool_w.astype(BF16)
    wb_bf = w_branch.astype(BF16)
    wo_bf = w_out.astype(BF16)
    fg = final_g.astype(F32).reshape(1, D_MODEL)
    for layer in range(DEPTH):
        proj, o_c = _proj_hgrn(h_res, norm_g[layer].astype(F32).reshape(1, D_MODEL), w_in_bf, layer,
                               lb_all[layer], hgrn_norm_g[layer].astype(F32), seq)
        o_a = _sb(proj, bsz, seq)
        h_res = _merge(o_a, o_c, proj, pool_w_bf, pool_scale.astype(F32),
                       wb_bf, wo_bf, h_res, fg, seq, layer)
    return h_res.reshape(bsz, seq, D_MODEL).astype(x.dtype)
```

```python
import functools
import math

import jax
import jax.numpy as jnp
import numpy as np
from jax import lax
from jax.experimental import pallas as pl
from jax.experimental.pallas import tpu as pltpu

D_MODEL = 1024
DEPTH = 2
BRANCH_WIDTH = D_MODEL // 2
SB_HEAD_DIM = 64
POOL_WINDOWS = (2, 4, 8, 16)
POOL_HALO = 16
HGRN_HEADS = 4
EPS = 1e-6
IN_COLS = 10 * BRANCH_WIDTH + 3 * D_MODEL

LANES = 128
SUBLANES = 8
VMEM_LIMIT_MIX = 60 * 1024 * 1024
VMEM_LIMIT_MERGE = 48 * 1024 * 1024

COL_SB_Q, COL_SB_K, COL_SB_V, COL_SB_Z = 0, 512, 1024, 1536
COL_POOL, COL_HG, COL_GATE = 2048, 3072, 5120
PM_COLS = 2 * BRANCH_WIDTH + 3 * D_MODEL

MIX_TM, MIX_TN = 512, 256
MIX_TICKS_PER_CHUNK = 6
SB_T = 256
SB_DEAD_LOG2 = -152.0
SB_OFF_LOG2 = -1e30
SB_Z_CAP = 126.0
HG_T = 128
MERGE_TM = 512

F32 = jnp.float32
BF16 = jnp.bfloat16


def _dot(a, b):
    return jnp.dot(a, b, preferred_element_type=F32)


def _dot_nt(a, b):
    return lax.dot_general(a, b, (((1,), (1,)), ((), ())), preferred_element_type=F32)


def _dot_tn(a, b):
    return lax.dot_general(a, b, (((0,), (0,)), ((), ())), preferred_element_type=F32)


def _split3(x):
    hi = x.astype(BF16)
    r = x - hi.astype(F32)
    mid = r.astype(BF16)
    lo = (r - mid.astype(F32)).astype(BF16)
    return hi, mid, lo


def _silu(x):
    return x * jax.nn.sigmoid(x)


def _sb_stages(ktile, vtile, tick):
    t = SB_T
    lane = lax.broadcasted_iota(jnp.int32, (1, LANES), 1)
    rows = lax.broadcasted_iota(jnp.int32, (t, t), 0)
    cols = lax.broadcasted_iota(jnp.int32, (t, t), 1)
    causal = cols < rows
    neg_upper = jnp.where(rows > cols, -1.0, 0.0).astype(BF16)

    def chain(z, masked):
        nl = jnp.maximum(z, jnp.log2(1.0 + jnp.exp2(jnp.minimum(z, SB_Z_CAP))))
        lz = z - nl
        if masked:
            nl = jnp.where(causal, nl, 0.0)
        out = lz, _dot(nl.astype(BF16), neg_upper), jnp.sum(nl, axis=-1, keepdims=True)
        tick()
        return out

    def weights(lz, rem, c, masked):
        a = jnp.exp2(lz + rem if c is None else lz + rem + c)
        if masked:
            a = jnp.where(causal, a, 0.0)
        tick()
        return a.astype(BF16)

    def live(carry):
        return jnp.max(jnp.maximum(carry[1], carry[3])) > SB_DEAD_LOG2

    def split(q):
        return (jnp.where(lane < SB_HEAD_DIM, q, jnp.zeros_like(q)),
                jnp.where(lane < SB_HEAD_DIM, jnp.zeros_like(q), q))

    def near(i, qa, qb):
        k0, v0 = ktile(i), vtile(i)
        prev = jnp.maximum(i - 1, 0)
        k1, v1 = ktile(prev), vtile(prev)
        z0a, z0b = _dot_nt(qa, k0), _dot_nt(qb, k0)
        z1a, z1b = _dot_nt(qa, k1), _dot_nt(qb, k1)
        lz0a, rem0a, rs0a = chain(z0a, True)
        lz0b, rem0b, rs0b = chain(z0b, True)
        acc_a = _dot(weights(lz0a, rem0a, None, True), v0)
        lz1a, rem1a, rs1a = chain(z1a, False)
        acc_b = _dot(weights(lz0b, rem0b, None, True), v0)
        lz1b, rem1b, rs1b = chain(z1b, False)
        off = jnp.where(i > 0, 0.0, SB_OFF_LOG2)
        c1a = off - rs0a
        c1b = off - rs0b
        acc_a = acc_a + _dot(weights(lz1a, rem1a, c1a, False), v1)
        acc_b = acc_b + _dot(weights(lz1b, rem1b, c1b, False), v1)
        return acc_a, c1a - rs1a, acc_b, c1b - rs1b

    def far(i, qa, qb, carry):
        def body(state):
            j, _, (acc_a, c_a, acc_b, c_b) = state
            k, v = ktile(j), vtile(j)
            lz_a, rem_a, rs_a = chain(_dot_nt(qa, k), False)
            lz_b, rem_b, rs_b = chain(_dot_nt(qb, k), False)
            acc_a = acc_a + _dot(weights(lz_a, rem_a, c_a, False), v)
            acc_b = acc_b + _dot(weights(lz_b, rem_b, c_b, False), v)
            new = (acc_a, c_a - rs_a, acc_b, c_b - rs_b)
            return j - 1, live(new), new

        state = lax.while_loop(lambda st: (st[0] >= 0) & st[1], body, (i - 2, live(carry), carry))
        carry = state[2]
        return jnp.where(lane < SB_HEAD_DIM, carry[0], carry[2])

    return split, near, far


def _pool_tile(u_ref, z_ref, w_ref, sc_ref, halo_scr, first_pos):
    tm = u_ref.shape[0]
    pos = first_pos + lax.broadcasted_iota(jnp.int32, (tm, 1), 0)
    outs = []
    for g, w in enumerate(POOL_WINDOWS):
        cols = slice(g * LANES, (g + 1) * LANES)
        u = u_ref[:, cols].astype(F32)
        s = jnp.concatenate([halo_scr[:, cols], u], axis=0)
        k = 1
        while k < w:
            s = s + pltpu.roll(s, k, axis=0)
            k *= 2
        cnt = jnp.minimum(pos + 1, w).astype(F32)
        d = s[POOL_HALO:, :] / cnt - u
        outs.append(_dot(d.astype(BF16), w_ref[g]))
        halo_scr[:, cols] = u[tm - POOL_HALO:, :]
    y = jnp.concatenate(outs, axis=1)
    return (y * sc_ref[...] * _silu(z_ref[...].astype(F32))).astype(BF16)


HG_LEVELS = HG_T.bit_length() - 1


def _hgrn_tables():
    r = np.arange(HG_T)
    tri = (r[None, :] <= r[:, None]).astype(np.float32)
    pair = np.zeros((HG_LEVELS, HG_T, HG_T), np.float32)
    sign = np.zeros((HG_LEVELS, HG_T, LANES), np.float32)
    for l in range(HG_LEVELS):
        m = 1 << l
        second = (r % (2 * m)) >= m
        same = (r[:, None] // (2 * m)) == (r[None, :] // (2 * m))
        pair[l] = same & second[:, None] & ~second[None, :]
        sign[l] = np.where(second, 1.0, -1.0)[:, None]
    return jnp.asarray(tri, BF16), jnp.asarray(pair), jnp.asarray(sign)


def _hgrn_edge(b, level):
    t = b.shape[0]
    m = 1 << level
    if m >= SUBLANES:
        blk = 2 * m
        e = b.reshape(t // blk, blk, LANES)[:, m - 1:m, :]
        return jnp.broadcast_to(e, (t // blk, blk, LANES)).reshape(t, LANES)
    b3 = b.reshape(t // SUBLANES, SUBLANES, LANES)
    if m == 4:
        e = jnp.broadcast_to(b3[:, 3:4, :], b3.shape)
    else:
        sub = lax.broadcasted_iota(jnp.int32, (1, SUBLANES, 1), 1)
        e = jnp.where(sub < 4, b3[:, 1:2, :], b3[:, 5:6, :])
    return e.reshape(t, LANES)


def _hgrn_stages(blk, rows, lb_ref, g_ref, tri_ref, pair_ref, sign_ref, o_ref, tick):
    t = HG_T

    def col(part, cols):
        return blk[rows, part * BRANCH_WIDTH + cols.start:part * BRANCH_WIDTH + cols.stop]

    def front(cols):
        lb = lb_ref[:, cols]
        sig = jax.nn.sigmoid(col(1, cols).astype(F32))
        f = lb + (1.0 - lb) * sig
        kk = (1.0 - lb) * (1.0 - sig)
        hi, mid, lo = _split3(jnp.log2(f))
        tri = tri_ref[...]
        b = _dot(tri, hi) + _dot(tri, mid) + _dot(tri, lo)
        tick()
        return f, kk, b

    def middle(cols, st, f, kk, b):
        q = col(0, cols).astype(F32)
        v_bf = col(2, cols)
        b_last = b[t - 1:t, :]
        o = _dot_nt((q * jnp.exp2(b)).astype(BF16), st.astype(BF16))
        kd = (kk * jnp.exp2(b_last - b)).astype(BF16)
        st = st * jnp.exp2(b_last) + _dot_tn(v_bf, kd)
        o = o + jnp.sum(q * kk, axis=-1, keepdims=True) * v_bf.astype(F32)
        amat = jnp.zeros((t, t), F32)
        for level in range(HG_LEVELS):
            if level == 0:
                dec = jnp.where(sign_ref[0] > 0.0, f, 1.0)
            else:
                dec = jnp.exp2((b - _hgrn_edge(b, level)) * sign_ref[level])
            p = _dot_nt((q * dec).astype(BF16), (kk * dec).astype(BF16))
            amat = amat + p * pair_ref[level]
            tick()
        return st, o, amat.astype(BF16), v_bf

    def back(cols, o, amat, v_bf):
        o = o + _dot(amat, v_bf)
        ms = jnp.mean(o * o, axis=-1, keepdims=True)
        o = o * lax.rsqrt(ms + EPS) * g_ref[:, cols]
        o_ref[rows, cols] = (o * _silu(col(3, cols).astype(F32))).astype(o_ref.dtype)
        tick()

    return front, middle, back


def _mix_kernel(x_ref, g_ref, w_ref, cs_ref, lb_ref, ng_ref, tri_ref, pair_ref, sign_ref,
                pm_ref, oa_ref, oc_ref, qz_scr, kv_scr, hg_scr, st_scr, *, blocks_per_seq):
    n = pl.program_id(0)
    slot = n % 2
    bps = blocks_per_seq

    @pl.when(n == 0)
    def _():
        qz_scr[...] = jnp.zeros_like(qz_scr)
        kv_scr[...] = jnp.zeros_like(kv_scr)
        hg_scr[...] = jnp.zeros_like(hg_scr)
        st_scr[...] = jnp.zeros_like(st_scr)

    lag = n - 1 + 2 * bps
    lag_pos = lag % bps
    lag_kv = kv_scr.at[(lag // bps) % 2]
    lag_qz = qz_scr.at[1 - slot]
    lag_hg = hg_scr.at[1 - slot]

    x = x_ref[...]
    ms = jnp.mean(x * x, axis=-1, keepdims=True)
    hn = (x * lax.rsqrt(ms + EPS) * g_ref[...]).astype(BF16)
    cur_qz = qz_scr.at[slot]
    cur_hg = hg_scr.at[slot]
    cur_kv = kv_scr.at[(n // bps) % 2]
    cur_rows = pl.ds(pl.multiple_of((n % bps) * MIX_TM, MIX_TM), MIX_TM)

    def project(c):
        lo, hi = c * MIX_TN, (c + 1) * MIX_TN
        y = (_dot(hn, w_ref[:, lo:hi]) * cs_ref[:, lo:hi]).astype(BF16)
        if lo < COL_SB_K:
            cur_qz[:, lo:hi] = y
        elif lo < COL_SB_Z:
            cur_kv[cur_rows, lo - COL_SB_K:hi - COL_SB_K] = y
        elif lo < COL_POOL:
            cur_qz[:, lo - COL_SB_Z + BRANCH_WIDTH:hi - COL_SB_Z + BRANCH_WIDTH] = y
        elif lo < COL_HG:
            pm_ref[:, lo - COL_POOL:hi - COL_POOL] = y
        elif lo < COL_GATE:
            cur_hg[:, lo - COL_HG:hi - COL_HG] = y
        else:
            pm_ref[:, lo - COL_GATE + 2 * BRANCH_WIDTH:hi - COL_GATE + 2 * BRANCH_WIDTH] = y

    chunks = iter(range(IN_COLS // MIX_TN))
    ticks = [0]

    def tick():
        ticks[0] += 1
        if ticks[0] % MIX_TICKS_PER_CHUNK == 0:
            c = next(chunks, None)
            if c is not None:
                project(c)

    fresh = lag_pos == 0
    heads = [slice(h * LANES, (h + 1) * LANES) for h in range(HGRN_HEADS)]
    states = [jnp.where(fresh, 0.0, st_scr[h]) for h in range(HGRN_HEADS)]
    pending = []
    for k in range(MIX_TM // HG_T):
        rows = slice(k * HG_T, (k + 1) * HG_T)
        front, middle, back = _hgrn_stages(lag_hg, rows, lb_ref, ng_ref, tri_ref, pair_ref, sign_ref,
                                           oc_ref, tick)
        fronts = [front(c) for c in heads]
        mids = [middle(c, st, *fr) for c, st, fr in zip(heads, states, fronts)]
        states = [md[0] for md in mids]
        for c, md in zip(heads, mids):
            back(c, *md[1:])

        pc = slice(k * LANES, (k + 1) * LANES)
        split, near, far = _sb_stages(
            lambda j, pc=pc: lag_kv[pl.ds(pl.multiple_of(j * SB_T, SB_T), SB_T), pc],
            lambda j, pc=pc: lag_kv[pl.ds(pl.multiple_of(j * SB_T, SB_T), SB_T),
                                    BRANCH_WIDTH + pc.start:BRANCH_WIDTH + pc.stop],
            tick)
        for u in range(MIX_TM // SB_T):
            qr = slice(u * SB_T, (u + 1) * SB_T)
            qa, qb = split(lag_qz[qr, pc])
            i = lag_pos * (MIX_TM // SB_T) + u
            pending.append((far, i, qa, qb, near(i, qa, qb), qr, pc))
    for c in chunks:
        project(c)
    for h in range(HGRN_HEADS):
        st_scr[h] = states[h]
    for far, i, qa, qb, carry, qr, pc in pending:
        o = far(i, qa, qb, carry)
        gate = _silu(lag_qz[qr, BRANCH_WIDTH + pc.start:BRANCH_WIDTH + pc.stop].astype(F32))
        oa_ref[qr, pc] = (o * gate).astype(oa_ref.dtype)


def _mix(h_res, g, w_bf, layer, lb, norm_g, seq):
    n = h_res.shape[0]
    nb = n // MIX_TM
    col = lax.broadcasted_iota(jnp.int32, (1, IN_COLS), 1)
    col_scale = jnp.where(col < BRANCH_WIDTH, math.log2(math.e) / math.sqrt(SB_HEAD_DIM), 1.0)
    tri, pair, sign = _hgrn_tables()
    whole = lambda arr: pl.BlockSpec(arr.shape, lambda i: (0,) * arr.ndim,
                                     pipeline_mode=pl.Buffered(1))
    cur = lambda i: (jnp.minimum(i, nb - 1), 0)
    lagged = lambda i: (jnp.maximum(i - 1, 0), 0)
    return pl.pallas_call(
        functools.partial(_mix_kernel, blocks_per_seq=seq // MIX_TM),
        grid=(nb + 1,),
        in_specs=[
            pl.BlockSpec((MIX_TM, D_MODEL), cur),
            pl.BlockSpec((1, D_MODEL), lambda i: (0, 0)),
            pl.BlockSpec((None, D_MODEL, IN_COLS), lambda i: (layer, 0, 0),
                         pipeline_mode=pl.Buffered(1)),
            pl.BlockSpec((1, IN_COLS), lambda i: (0, 0)),
            pl.BlockSpec((1, BRANCH_WIDTH), lambda i: (0, 0)),
            pl.BlockSpec((1, BRANCH_WIDTH), lambda i: (0, 0)),
            whole(tri), whole(pair), whole(sign),
        ],
        out_specs=[
            pl.BlockSpec((MIX_TM, PM_COLS), cur),
            pl.BlockSpec((MIX_TM, BRANCH_WIDTH), lagged),
            pl.BlockSpec((MIX_TM, BRANCH_WIDTH), lagged),
        ],
        out_shape=[jax.ShapeDtypeStruct((n, PM_COLS), BF16),
                   jax.ShapeDtypeStruct((n, BRANCH_WIDTH), BF16),
                   jax.ShapeDtypeStruct((n, BRANCH_WIDTH), BF16)],
        scratch_shapes=[pltpu.VMEM((2, MIX_TM, 2 * BRANCH_WIDTH), BF16),
                        pltpu.VMEM((2, seq, 2 * BRANCH_WIDTH), BF16),
                        pltpu.VMEM((2, MIX_TM, 4 * BRANCH_WIDTH), BF16),
                        pltpu.VMEM((HGRN_HEADS, LANES, LANES), F32)],
        compiler_params=pltpu.CompilerParams(dimension_semantics=("arbitrary",),
                                             vmem_limit_bytes=VMEM_LIMIT_MIX),
        name="mix",
    )(h_res, g, w_bf, col_scale.astype(F32), lb.reshape(1, BRANCH_WIDTH),
      norm_g.reshape(1, BRANCH_WIDTH), tri, pair, sign)


def _merge_kernel(oa_ref, oc_ref, pu_ref, pz_ref, ga_ref, gb_ref, gc_ref, pw_ref, ps_ref,
                  wb_ref, wo_ref, res_ref, fg_ref, out_ref, halo_scr, *, final, tiles_per_seq):
    ti = pl.program_id(0) % tiles_per_seq

    @pl.when(ti == 0)
    def _():
        halo_scr[...] = jnp.zeros_like(halo_scr)

    o_b = _pool_tile(pu_ref, pz_ref, pw_ref, ps_ref, halo_scr, ti * MERGE_TM)
    merged = (jax.nn.sigmoid(ga_ref[...].astype(F32)) * _dot(oa_ref[...], wb_ref[0])
              + jax.nn.sigmoid(gb_ref[...].astype(F32)) * _dot(o_b, wb_ref[1])
              + jax.nn.sigmoid(gc_ref[...].astype(F32)) * _dot(oc_ref[...], wb_ref[2]))
    h = res_ref[...] + _dot(merged.astype(BF16), wo_ref[...])
    if final:
        ms = jnp.mean(h * h, axis=-1, keepdims=True)
        h = h * lax.rsqrt(ms + EPS) * fg_ref[...]
    out_ref[...] = h


def _merge(o_a, o_c, pm, pool_w_bf, pool_scale, wb_bf, wo_bf, h_res, final_g, seq, layer):
    n = h_res.shape[0]
    groups = len(POOL_WINDOWS)
    final = layer == DEPTH - 1
    row = lambda width, blk=0: pl.BlockSpec((MERGE_TM, width), lambda i: (i, blk))
    whole = lambda *shape: pl.BlockSpec((None,) + shape, lambda i: (layer,) + (0,) * len(shape))
    return pl.pallas_call(
        functools.partial(_merge_kernel, final=final, tiles_per_seq=seq // MERGE_TM),
        grid=(n // MERGE_TM,),
        in_specs=[
            row(BRANCH_WIDTH), row(BRANCH_WIDTH),
            row(BRANCH_WIDTH, 0), row(BRANCH_WIDTH, 1),
            row(D_MODEL, 1), row(D_MODEL, 2), row(D_MODEL, 3),
            whole(groups, LANES, LANES), whole(1, BRANCH_WIDTH),
            whole(3, BRANCH_WIDTH, D_MODEL), whole(D_MODEL, D_MODEL),
            row(D_MODEL), pl.BlockSpec((1, D_MODEL), lambda i: (0, 0)),
        ],
        out_specs=row(D_MODEL),
        out_shape=jax.ShapeDtypeStruct((n, D_MODEL), F32),
        scratch_shapes=[pltpu.VMEM((POOL_HALO, BRANCH_WIDTH), F32)],
        compiler_params=pltpu.CompilerParams(dimension_semantics=("arbitrary",),
                                             vmem_limit_bytes=VMEM_LIMIT_MERGE),
        name="merge_final" if final else "merge",
    )(o_a, o_c, pm, pm, pm, pm, pm, pool_w_bf,
      pool_scale.reshape(DEPTH, 1, BRANCH_WIDTH), wb_bf, wo_bf, h_res, final_g)


def kernel(x, norm_g, w_in, pool_w, pool_scale, hgrn_lb, hgrn_norm_g, w_branch, w_out, final_g):
    bsz, seq, _ = x.shape
    h_res = x.astype(F32).reshape(bsz * seq, D_MODEL)
    lb_all = jnp.cumsum(jax.nn.softmax(hgrn_lb.astype(F32), axis=0), axis=0)
    lb_all = lb_all - lb_all[:1]
    w_in_bf = w_in.astype(BF16)
    pool_w_bf = pool_w.astype(BF16)
    wb_bf = w_branch.astype(BF16)
    wo_bf = w_out.astype(BF16)
    fg = final_g.astype(F32).reshape(1, D_MODEL)
    for layer in range(DEPTH):
        pm, o_a, o_c = _mix(h_res, norm_g[layer].astype(F32).reshape(1, D_MODEL), w_in_bf, layer,
                            lb_all[layer], hgrn_norm_g[layer].astype(F32), seq)
        h_res = _merge(o_a, o_c, pm, pool_w_bf, pool_scale.astype(F32),
                       wb_bf, wo_bf, h_res, fg, seq, layer)
    return h_res.reshape(bsz, seq, D_MODEL).astype(x.dtype)
```

```python
import functools
import math

import jax
import jax.numpy as jnp
import numpy as np
from jax import lax
from jax.experimental import pallas as pl
from jax.experimental.pallas import tpu as pltpu

D_MODEL = 1024
DEPTH = 2
BRANCH_WIDTH = D_MODEL // 2
SB_HEAD_DIM = 64
POOL_WINDOWS = (2, 4, 8, 16)
POOL_HALO = 16
HGRN_HEADS = 4
EPS = 1e-6
IN_COLS = 10 * BRANCH_WIDTH + 3 * D_MODEL

LANES = 128
SUBLANES = 8
VMEM_LIMIT_MIX = 62 * 1024 * 1024
VMEM_LIMIT_MERGE = 48 * 1024 * 1024

COL_SB_Q, COL_SB_K, COL_SB_V, COL_SB_Z = 0, 512, 1024, 1536
COL_POOL, COL_HG, COL_GATE = 2048, 3072, 5120
PM_COLS = 2 * BRANCH_WIDTH + 3 * D_MODEL

MIX_TM, MIX_TN = 512, 256
MIX_TICKS_PER_CHUNK = 6
SB_T = 256
SB_DEAD_LOG2 = -152.0
SB_OFF_LOG2 = -1e30
SB_Z_CAP = 126.0
HG_T = 128
MERGE_TM = 512
MERGE_ROWS = 512

F32 = jnp.float32
BF16 = jnp.bfloat16


def _dot(a, b):
    return jnp.dot(a, b, preferred_element_type=F32)


def _dot_nt(a, b):
    return lax.dot_general(a, b, (((1,), (1,)), ((), ())), preferred_element_type=F32)


def _dot_tn(a, b):
    return lax.dot_general(a, b, (((0,), (0,)), ((), ())), preferred_element_type=F32)


def _split3(x):
    hi = x.astype(BF16)
    r = x - hi.astype(F32)
    mid = r.astype(BF16)
    lo = (r - mid.astype(F32)).astype(BF16)
    return hi, mid, lo


def _silu(x):
    return x * jax.nn.sigmoid(x)


def _sb_stages(ktile, vtile, tick):
    t = SB_T
    lane = lax.broadcasted_iota(jnp.int32, (1, LANES), 1)
    rows = lax.broadcasted_iota(jnp.int32, (t, t), 0)
    cols = lax.broadcasted_iota(jnp.int32, (t, t), 1)
    causal = cols < rows
    neg_upper = jnp.where(rows > cols, -1.0, 0.0).astype(BF16)

    def chain(z, masked):
        nl = jnp.maximum(z, jnp.log2(1.0 + jnp.exp2(jnp.minimum(z, SB_Z_CAP))))
        lz = z - nl
        if masked:
            nl = jnp.where(causal, nl, 0.0)
        out = lz, _dot(nl.astype(BF16), neg_upper), jnp.sum(nl, axis=-1, keepdims=True)
        tick()
        return out

    def weights(lz, rem, c, masked):
        a = jnp.exp2(lz + rem if c is None else lz + rem + c)
        if masked:
            a = jnp.where(causal, a, 0.0)
        tick()
        return a.astype(BF16)

    def live(carry):
        return jnp.max(jnp.maximum(carry[1], carry[3])) > SB_DEAD_LOG2

    def split(q):
        return (jnp.where(lane < SB_HEAD_DIM, q, jnp.zeros_like(q)),
                jnp.where(lane < SB_HEAD_DIM, jnp.zeros_like(q), q))

    def near(i, qa, qb):
        k0, v0 = ktile(i), vtile(i)
        prev = jnp.maximum(i - 1, 0)
        k1, v1 = ktile(prev), vtile(prev)
        z0a, z0b = _dot_nt(qa, k0), _dot_nt(qb, k0)
        z1a, z1b = _dot_nt(qa, k1), _dot_nt(qb, k1)
        lz0a, rem0a, rs0a = chain(z0a, True)
        lz0b, rem0b, rs0b = chain(z0b, True)
        acc_a = _dot(weights(lz0a, rem0a, None, True), v0)
        lz1a, rem1a, rs1a = chain(z1a, False)
        acc_b = _dot(weights(lz0b, rem0b, None, True), v0)
        lz1b, rem1b, rs1b = chain(z1b, False)
        off = jnp.where(i > 0, 0.0, SB_OFF_LOG2)
        c1a = off - rs0a
        c1b = off - rs0b
        acc_a = acc_a + _dot(weights(lz1a, rem1a, c1a, False), v1)
        acc_b = acc_b + _dot(weights(lz1b, rem1b, c1b, False), v1)
        return acc_a, c1a - rs1a, acc_b, c1b - rs1b

    def far(i, qa, qb, carry, alive):
        def body(state):
            j, _, (acc_a, c_a, acc_b, c_b) = state
            k, v = ktile(j), vtile(j)
            lz_a, rem_a, rs_a = chain(_dot_nt(qa, k), False)
            lz_b, rem_b, rs_b = chain(_dot_nt(qb, k), False)
            acc_a = acc_a + _dot(weights(lz_a, rem_a, c_a, False), v)
            acc_b = acc_b + _dot(weights(lz_b, rem_b, c_b, False), v)
            new = (acc_a, c_a - rs_a, acc_b, c_b - rs_b)
            return j - 1, live(new), new

        state = lax.while_loop(lambda st: (st[0] >= 0) & st[1], body, (i - 2, alive, carry))
        carry = state[2]
        return jnp.where(lane < SB_HEAD_DIM, carry[0], carry[2])

    return split, near, live, far


def _pool_tile(u_ref, z_ref, w_ref, sc_ref, halo_scr, rows, first_pos):
    tm = rows.stop - rows.start
    pos = first_pos + rows.start + lax.broadcasted_iota(jnp.int32, (tm, 1), 0)
    outs = []
    for g, w in enumerate(POOL_WINDOWS):
        cols = slice(g * LANES, (g + 1) * LANES)
        u = u_ref[rows, cols].astype(F32)
        s = jnp.concatenate([halo_scr[:, cols], u], axis=0)
        k = 1
        while k < w:
            s = s + pltpu.roll(s, k, axis=0)
            k *= 2
        cnt = jnp.minimum(pos + 1, w).astype(F32)
        d = s[POOL_HALO:, :] / cnt - u
        outs.append(_dot(d.astype(BF16), w_ref[g]))
        halo_scr[:, cols] = u[tm - POOL_HALO:, :]
    y = jnp.concatenate(outs, axis=1)
    return (y * sc_ref[...] * z_ref[rows, :].astype(F32)).astype(BF16)


HG_LEVELS = HG_T.bit_length() - 1


def _hgrn_tables():
    r = np.arange(HG_T)
    tri = (r[None, :] <= r[:, None]).astype(np.float32)
    pair = np.zeros((HG_LEVELS, HG_T, HG_T), np.float32)
    sign = np.zeros((HG_LEVELS, HG_T, LANES), np.float32)
    for l in range(HG_LEVELS):
        m = 1 << l
        second = (r % (2 * m)) >= m
        same = (r[:, None] // (2 * m)) == (r[None, :] // (2 * m))
        pair[l] = same & second[:, None] & ~second[None, :]
        sign[l] = np.where(second, 1.0, -1.0)[:, None]
    return jnp.asarray(tri, BF16), jnp.asarray(pair), jnp.asarray(sign)


def _hgrn_edge(b, level):
    t = b.shape[0]
    m = 1 << level
    if m >= SUBLANES:
        blk = 2 * m
        e = b.reshape(t // blk, blk, LANES)[:, m - 1:m, :]
        return jnp.broadcast_to(e, (t // blk, blk, LANES)).reshape(t, LANES)
    b3 = b.reshape(t // SUBLANES, SUBLANES, LANES)
    if m == 4:
        e = jnp.broadcast_to(b3[:, 3:4, :], b3.shape)
    else:
        sub = lax.broadcasted_iota(jnp.int32, (1, SUBLANES, 1), 1)
        e = jnp.where(sub < 4, b3[:, 1:2, :], b3[:, 5:6, :])
    return e.reshape(t, LANES)


def _hgrn_stages(blk, rows, lb_ref, g_ref, tri_ref, pair_ref, sign_ref, o_ref, tick):
    t = HG_T

    def col(part, cols):
        return blk[rows, part * BRANCH_WIDTH + cols.start:part * BRANCH_WIDTH + cols.stop]

    def front(cols):
        lb = lb_ref[:, cols]
        sig = jax.nn.sigmoid(col(1, cols).astype(F32))
        f = lb + (1.0 - lb) * sig
        kk = (1.0 - lb) * (1.0 - sig)
        hi, mid, lo = _split3(jnp.log2(f))
        tri = tri_ref[...]
        b = _dot(tri, hi) + _dot(tri, mid) + _dot(tri, lo)
        tick()
        return f, kk, b

    def middle(cols, st, f, kk, b):
        q = col(0, cols).astype(F32)
        v_bf = col(2, cols)
        b_last = b[t - 1:t, :]
        o = _dot_nt((q * jnp.exp2(b)).astype(BF16), st.astype(BF16))
        kd = (kk * jnp.exp2(b_last - b)).astype(BF16)
        st = st * jnp.exp2(b_last) + _dot_tn(v_bf, kd)
        o = o + jnp.sum(q * kk, axis=-1, keepdims=True) * v_bf.astype(F32)
        amat = jnp.zeros((t, t), F32)
        for level in range(HG_LEVELS):
            if level == 0:
                dec = jnp.where(sign_ref[0] > 0.0, f, 1.0)
            else:
                dec = jnp.exp2((b - _hgrn_edge(b, level)) * sign_ref[level])
            p = _dot_nt((q * dec).astype(BF16), (kk * dec).astype(BF16))
            amat = amat + p * pair_ref[level]
            tick()
        return st, o, amat.astype(BF16), v_bf

    def back(cols, o, amat, v_bf):
        o = o + _dot(amat, v_bf)
        ms = jnp.mean(o * o, axis=-1, keepdims=True)
        o = o * lax.rsqrt(ms + EPS) * g_ref[:, cols]
        o_ref[rows, cols] = (o * _silu(col(3, cols).astype(F32))).astype(o_ref.dtype)
        tick()

    return front, middle, back


def _mix_kernel(x_ref, g_ref, w_ref, cs_ref, lb_ref, ng_ref, tri_ref, pair_ref, sign_ref,
                pm_ref, oa_ref, oc_ref, qz_scr, kv_scr, hg_scr, st_scr, *, blocks_per_seq):
    n = pl.program_id(0)
    last = pl.num_programs(0) - 1
    slot = n % 2
    bps = blocks_per_seq

    lag_pos = (n + bps - 1) % bps
    lag_kv = kv_scr.at[((n + 2 * bps - 1) // bps) % 2]
    lag_qz = qz_scr.at[1 - slot]
    lag_hg = hg_scr.at[1 - slot]
    cur_qz = qz_scr.at[slot]
    cur_hg = hg_scr.at[slot]
    cur_kv = kv_scr.at[(n // bps) % 2]
    cur_rows = pl.ds(pl.multiple_of((n % bps) * MIX_TM, MIX_TM), MIX_TM)

    def step(with_projection, with_mixers):
        if with_projection:
            x = x_ref[...]
            ms = jnp.mean(x * x, axis=-1, keepdims=True)
            hn = (x * lax.rsqrt(ms + EPS) * g_ref[...]).astype(BF16)

        def project(c):
            lo, hi = c * MIX_TN, (c + 1) * MIX_TN
            y = _dot(hn, w_ref[:, lo:hi])
            if lo < BRANCH_WIDTH:
                y = y * cs_ref[:, lo:hi]
            elif lo >= COL_GATE:
                y = jax.nn.sigmoid(y)
            elif COL_POOL + BRANCH_WIDTH <= lo < COL_HG:
                y = _silu(y)
            y = y.astype(BF16)
            if lo < COL_SB_K:
                cur_qz[:, lo:hi] = y
            elif lo < COL_SB_Z:
                cur_kv[cur_rows, lo - COL_SB_K:hi - COL_SB_K] = y
            elif lo < COL_POOL:
                cur_qz[:, lo - COL_SB_Z + BRANCH_WIDTH:hi - COL_SB_Z + BRANCH_WIDTH] = y
            elif lo < COL_HG:
                pm_ref[:, lo - COL_POOL:hi - COL_POOL] = y
            elif lo < COL_GATE:
                cur_hg[:, lo - COL_HG:hi - COL_HG] = y
            else:
                pm_ref[:, lo - COL_GATE + 2 * BRANCH_WIDTH:hi - COL_GATE + 2 * BRANCH_WIDTH] = y

        chunks = iter(range(IN_COLS // MIX_TN if with_projection else 0))
        ticks = [0]

        def tick():
            ticks[0] += 1
            if ticks[0] % MIX_TICKS_PER_CHUNK == 0:
                c = next(chunks, None)
                if c is not None:
                    project(c)

        if not with_mixers:
            for c in chunks:
                project(c)
            return

        fresh = lag_pos == 0
        heads = [slice(h * LANES, (h + 1) * LANES) for h in range(HGRN_HEADS)]
        states = [jnp.where(fresh, 0.0, st_scr[h]) for h in range(HGRN_HEADS)]
        pending = []
        for k in range(MIX_TM // HG_T):
            rows = slice(k * HG_T, (k + 1) * HG_T)
            front, middle, back = _hgrn_stages(lag_hg, rows, lb_ref, ng_ref, tri_ref, pair_ref,
                                               sign_ref, oc_ref, tick)
            fronts = [front(c) for c in heads]
            mids = [middle(c, st, *fr) for c, st, fr in zip(heads, states, fronts)]
            states = [md[0] for md in mids]
            for c, md in zip(heads, mids):
                back(c, *md[1:])

            pc = slice(k * LANES, (k + 1) * LANES)
            split, near, live, far = _sb_stages(
                lambda j, pc=pc: lag_kv[pl.ds(pl.multiple_of(j * SB_T, SB_T), SB_T), pc],
                lambda j, pc=pc: lag_kv[pl.ds(pl.multiple_of(j * SB_T, SB_T), SB_T),
                                        BRANCH_WIDTH + pc.start:BRANCH_WIDTH + pc.stop],
                tick)
            for u in range(MIX_TM // SB_T):
                qr = slice(u * SB_T, (u + 1) * SB_T)
                qa, qb = split(lag_qz[qr, pc])
                i = lag_pos * (MIX_TM // SB_T) + u
                carry = near(i, qa, qb)
                gate = _silu(lag_qz[qr, BRANCH_WIDTH + pc.start:BRANCH_WIDTH + pc.stop].astype(F32))
                pending.append((far, i, qa, qb, carry, live(carry), gate, qr, pc))
        for c in chunks:
            project(c)
        for h in range(HGRN_HEADS):
            st_scr[h] = states[h]
        outs = [far(i, qa, qb, carry, alive) for far, i, qa, qb, carry, alive, _, _, _ in pending]
        for o, (_, _, _, _, _, _, gate, qr, pc) in zip(outs, pending):
            oa_ref[qr, pc] = (o * gate).astype(oa_ref.dtype)

    pl.when(n == 0)(lambda: step(True, False))
    pl.when((n > 0) & (n < last))(lambda: step(True, True))
    pl.when(n == last)(lambda: step(False, True))


def _mix(h_res, g, w_bf, layer, lb, norm_g, seq):
    n = h_res.shape[0]
    nb = n // MIX_TM
    col = lax.broadcasted_iota(jnp.int32, (1, IN_COLS), 1)
    col_scale = jnp.where(col < BRANCH_WIDTH, math.log2(math.e) / math.sqrt(SB_HEAD_DIM), 1.0)
    tri, pair, sign = _hgrn_tables()
    whole = lambda arr: pl.BlockSpec(arr.shape, lambda i: (0,) * arr.ndim,
                                     pipeline_mode=pl.Buffered(1))
    cur = lambda i: (jnp.minimum(i, nb - 1), 0)
    lagged = lambda i: (jnp.maximum(i - 1, 0), 0)
    return pl.pallas_call(
        functools.partial(_mix_kernel, blocks_per_seq=seq // MIX_TM),
        grid=(nb + 1,),
        in_specs=[
            pl.BlockSpec((MIX_TM, D_MODEL), cur),
            pl.BlockSpec((1, D_MODEL), lambda i: (0, 0)),
            pl.BlockSpec((None, D_MODEL, IN_COLS), lambda i: (layer, 0, 0),
                         pipeline_mode=pl.Buffered(1)),
            pl.BlockSpec((1, IN_COLS), lambda i: (0, 0)),
            pl.BlockSpec((1, BRANCH_WIDTH), lambda i: (0, 0)),
            pl.BlockSpec((1, BRANCH_WIDTH), lambda i: (0, 0)),
            whole(tri), whole(pair), whole(sign),
        ],
        out_specs=[
            pl.BlockSpec((MIX_TM, PM_COLS), cur),
            pl.BlockSpec((MIX_TM, BRANCH_WIDTH), lagged),
            pl.BlockSpec((MIX_TM, BRANCH_WIDTH), lagged),
        ],
        out_shape=[jax.ShapeDtypeStruct((n, PM_COLS), BF16),
                   jax.ShapeDtypeStruct((n, BRANCH_WIDTH), BF16),
                   jax.ShapeDtypeStruct((n, BRANCH_WIDTH), BF16)],
        scratch_shapes=[pltpu.VMEM((2, MIX_TM, 2 * BRANCH_WIDTH), BF16),
                        pltpu.VMEM((2, seq, 2 * BRANCH_WIDTH), BF16),
                        pltpu.VMEM((2, MIX_TM, 4 * BRANCH_WIDTH), BF16),
                        pltpu.VMEM((HGRN_HEADS, LANES, LANES), F32)],
        compiler_params=pltpu.CompilerParams(dimension_semantics=("arbitrary",),
                                             vmem_limit_bytes=VMEM_LIMIT_MIX),
        name="mix",
    )(h_res, g, w_bf, col_scale.astype(F32), lb.reshape(1, BRANCH_WIDTH),
      norm_g.reshape(1, BRANCH_WIDTH), tri, pair, sign)


def _merge_kernel(oa_ref, oc_ref, pu_ref, pz_ref, ga_ref, gb_ref, gc_ref, pw_ref, ps_ref,
                  wb_ref, wo_ref, res_ref, fg_ref, out_ref, halo_scr, *, final, tiles_per_seq):
    ti = pl.program_id(0) % tiles_per_seq

    @pl.when(ti == 0)
    def _():
        halo_scr[...] = jnp.zeros_like(halo_scr)

    for r0 in range(0, MERGE_TM, MERGE_ROWS):
        rows = slice(r0, r0 + MERGE_ROWS)
        o_b = _pool_tile(pu_ref, pz_ref, pw_ref, ps_ref, halo_scr, rows, ti * MERGE_TM)
        merged = (ga_ref[rows, :].astype(F32) * _dot(oa_ref[rows, :], wb_ref[0])
                  + gb_ref[rows, :].astype(F32) * _dot(o_b, wb_ref[1])
                  + gc_ref[rows, :].astype(F32) * _dot(oc_ref[rows, :], wb_ref[2]))
        h = res_ref[rows, :] + _dot(merged.astype(BF16), wo_ref[...])
        if final:
            ms = jnp.mean(h * h, axis=-1, keepdims=True)
            h = h * lax.rsqrt(ms + EPS) * fg_ref[...]
        out_ref[rows, :] = h


def _merge(o_a, o_c, pm, pool_w_bf, pool_scale, wb_bf, wo_bf, h_res, final_g, seq, layer):
    n = h_res.shape[0]
    groups = len(POOL_WINDOWS)
    final = layer == DEPTH - 1
    row = lambda width, blk=0: pl.BlockSpec((MERGE_TM, width), lambda i: (i, blk))
    whole = lambda *shape: pl.BlockSpec((None,) + shape, lambda i: (layer,) + (0,) * len(shape))
    return pl.pallas_call(
        functools.partial(_merge_kernel, final=final, tiles_per_seq=seq // MERGE_TM),
        grid=(n // MERGE_TM,),
        in_specs=[
            row(BRANCH_WIDTH), row(BRANCH_WIDTH),
            row(BRANCH_WIDTH, 0), row(BRANCH_WIDTH, 1),
            row(D_MODEL, 1), row(D_MODEL, 2), row(D_MODEL, 3),
            whole(groups, LANES, LANES), whole(1, BRANCH_WIDTH),
            whole(3, BRANCH_WIDTH, D_MODEL), whole(D_MODEL, D_MODEL),
            row(D_MODEL), pl.BlockSpec((1, D_MODEL), lambda i: (0, 0)),
        ],
        out_specs=row(D_MODEL),
        out_shape=jax.ShapeDtypeStruct((n, D_MODEL), F32),
        scratch_shapes=[pltpu.VMEM((POOL_HALO, BRANCH_WIDTH), F32)],
        compiler_params=pltpu.CompilerParams(dimension_semantics=("arbitrary",),
                                             vmem_limit_bytes=VMEM_LIMIT_MERGE),
        name="merge_final" if final else "merge",
    )(o_a, o_c, pm, pm, pm, pm, pm, pool_w_bf,
      pool_scale.reshape(DEPTH, 1, BRANCH_WIDTH), wb_bf, wo_bf, h_res, final_g)


def kernel(x, norm_g, w_in, pool_w, pool_scale, hgrn_lb, hgrn_norm_g, w_branch, w_out, final_g):
    bsz, seq, _ = x.shape
    h_res = x.astype(F32).reshape(bsz * seq, D_MODEL)
    lb_all = jnp.cumsum(jax.nn.softmax(hgrn_lb.astype(F32), axis=0), axis=0)
    lb_all = lb_all - lb_all[:1]
    w_in_bf = w_in.astype(BF16)
    pool_w_bf = pool_w.astype(BF16)
    wb_bf = w_branch.astype(BF16)
    wo_bf = w_out.astype(BF16)
    fg = final_g.astype(F32).reshape(1, D_MODEL)
    for layer in range(DEPTH):
        pm, o_a, o_c = _mix(h_res, norm_g[layer].astype(F32).reshape(1, D_MODEL), w_in_bf, layer,
                            lb_all[layer], hgrn_norm_g[layer].astype(F32), seq)
        h_res = _merge(o_a, o_c, pm, pool_w_bf, pool_scale.astype(F32),
                       wb_bf, wo_bf, h_res, fg, seq, layer)
    return h_res.reshape(bsz, seq, D_MODEL).astype(x.dtype)
```

```python
import functools
import math

import jax
import jax.numpy as jnp
import numpy as np
from jax import lax
from jax.experimental import pallas as pl
from jax.experimental.pallas import tpu as pltpu

D_MODEL = 1024
DEPTH = 2
BRANCH_WIDTH = D_MODEL // 2
SB_HEAD_DIM = 64
POOL_WINDOWS = (2, 4, 8, 16)
POOL_HALO = 16
HGRN_HEADS = 4
EPS = 1e-6
IN_COLS = 10 * BRANCH_WIDTH + 3 * D_MODEL

LANES = 128
SUBLANES = 8
VMEM_LIMIT_MIX = 62 * 1024 * 1024
VMEM_LIMIT_MERGE = 48 * 1024 * 1024

COL_SB_Q, COL_SB_K, COL_SB_V, COL_SB_Z = 0, 512, 1024, 1536
COL_POOL, COL_HG, COL_GATE = 2048, 3072, 5120
PM_COLS = 2 * BRANCH_WIDTH + 3 * D_MODEL

MIX_TM, MIX_TN = 512, 256
MIX_TICKS_PER_CHUNK = 6
SB_T = 256
SB_DEAD_LOG2 = -152.0
SB_OFF_LOG2 = -1e30
SB_Z_CAP = 126.0
HG_T = 128
MERGE_TM = 512
MERGE_ROWS = 512

F32 = jnp.float32
BF16 = jnp.bfloat16


def _dot(a, b):
    return jnp.dot(a, b, preferred_element_type=F32)


def _dot_nt(a, b):
    return lax.dot_general(a, b, (((1,), (1,)), ((), ())), preferred_element_type=F32)


def _dot_tn(a, b):
    return lax.dot_general(a, b, (((0,), (0,)), ((), ())), preferred_element_type=F32)


def _split3(x):
    hi = x.astype(BF16)
    r = x - hi.astype(F32)
    mid = r.astype(BF16)
    lo = (r - mid.astype(F32)).astype(BF16)
    return hi, mid, lo


def _silu(x):
    return x * jax.nn.sigmoid(x)


def _sb_stages(ktile, vtile, tick):
    t = SB_T
    lane = lax.broadcasted_iota(jnp.int32, (1, LANES), 1)
    rows = lax.broadcasted_iota(jnp.int32, (t, t), 0)
    cols = lax.broadcasted_iota(jnp.int32, (t, t), 1)
    causal = cols < rows
    neg_upper = jnp.where(rows > cols, -1.0, 0.0).astype(BF16)

    def chain(z, masked):
        nl = jnp.maximum(z, jnp.log2(1.0 + jnp.exp2(jnp.minimum(z, SB_Z_CAP))))
        lz = z - nl
        if masked:
            nl = jnp.where(causal, nl, 0.0)
        out = lz, _dot(nl.astype(BF16), neg_upper), jnp.sum(nl, axis=-1, keepdims=True)
        tick()
        return out

    def weights(lz, rem, c, masked):
        a = jnp.exp2(lz + rem if c is None else lz + rem + c)
        if masked:
            a = jnp.where(causal, a, 0.0)
        tick()
        return a.astype(BF16)

    def live(carry):
        return jnp.max(jnp.maximum(carry[1], carry[3])) > SB_DEAD_LOG2

    def split(q):
        return (jnp.where(lane < SB_HEAD_DIM, q, jnp.zeros_like(q)),
                jnp.where(lane < SB_HEAD_DIM, jnp.zeros_like(q), q))

    def near(i, qa, qb):
        k0, v0 = ktile(i), vtile(i)
        prev = jnp.maximum(i - 1, 0)
        k1, v1 = ktile(prev), vtile(prev)
        z0a, z0b = _dot_nt(qa, k0), _dot_nt(qb, k0)
        z1a, z1b = _dot_nt(qa, k1), _dot_nt(qb, k1)
        lz0a, rem0a, rs0a = chain(z0a, True)
        lz0b, rem0b, rs0b = chain(z0b, True)
        acc_a = _dot(weights(lz0a, rem0a, None, True), v0)
        lz1a, rem1a, rs1a = chain(z1a, False)
        acc_b = _dot(weights(lz0b, rem0b, None, True), v0)
        lz1b, rem1b, rs1b = chain(z1b, False)
        off = jnp.where(i > 0, 0.0, SB_OFF_LOG2)
        c1a = off - rs0a
        c1b = off - rs0b
        acc_a = acc_a + _dot(weights(lz1a, rem1a, c1a, False), v1)
        acc_b = acc_b + _dot(weights(lz1b, rem1b, c1b, False), v1)
        return acc_a, c1a - rs1a, acc_b, c1b - rs1b

    def far(i, qa, qb, carry, alive):
        def body(state):
            j, _, (acc_a, c_a, acc_b, c_b) = state
            k, v = ktile(j), vtile(j)
            lz_a, rem_a, rs_a = chain(_dot_nt(qa, k), False)
            lz_b, rem_b, rs_b = chain(_dot_nt(qb, k), False)
            acc_a = acc_a + _dot(weights(lz_a, rem_a, c_a, False), v)
            acc_b = acc_b + _dot(weights(lz_b, rem_b, c_b, False), v)
            new = (acc_a, c_a - rs_a, acc_b, c_b - rs_b)
            return j - 1, live(new), new

        state = lax.while_loop(lambda st: (st[0] >= 0) & st[1], body, (i - 2, alive, carry))
        carry = state[2]
        return jnp.where(lane < SB_HEAD_DIM, carry[0], carry[2])

    return split, near, live, far


def _pool_tile(u_ref, z_ref, w_ref, sc_ref, halo_scr, rows, first_pos):
    tm = rows.stop - rows.start
    pos = first_pos + rows.start + lax.broadcasted_iota(jnp.int32, (tm, 1), 0)
    outs = []
    for g, w in enumerate(POOL_WINDOWS):
        cols = slice(g * LANES, (g + 1) * LANES)
        u = u_ref[rows, cols].astype(F32)
        s = jnp.concatenate([halo_scr[:, cols], u], axis=0)
        k = 1
        while k < w:
            s = s + pltpu.roll(s, k, axis=0)
            k *= 2
        cnt = jnp.minimum(pos + 1, w).astype(F32)
        d = s[POOL_HALO:, :] / cnt - u
        outs.append(_dot(d.astype(BF16), w_ref[g]))
        halo_scr[:, cols] = u[tm - POOL_HALO:, :]
    y = jnp.concatenate(outs, axis=1)
    return (y * sc_ref[...] * z_ref[rows, :].astype(F32)).astype(BF16)


HG_LEVELS = HG_T.bit_length() - 1


def _hgrn_tables():
    r = np.arange(HG_T)
    tri = (r[None, :] <= r[:, None]).astype(np.float32)
    pair = np.zeros((HG_LEVELS, HG_T, HG_T), np.float32)
    sign = np.zeros((HG_LEVELS, HG_T, LANES), np.float32)
    for l in range(HG_LEVELS):
        m = 1 << l
        second = (r % (2 * m)) >= m
        same = (r[:, None] // (2 * m)) == (r[None, :] // (2 * m))
        pair[l] = same & second[:, None] & ~second[None, :]
        sign[l] = np.where(second, 1.0, -1.0)[:, None]
    return jnp.asarray(tri, BF16), jnp.asarray(pair), jnp.asarray(sign)


def _hgrn_edge(b, level):
    t = b.shape[0]
    m = 1 << level
    if m >= SUBLANES:
        blk = 2 * m
        e = b.reshape(t // blk, blk, LANES)[:, m - 1:m, :]
        return jnp.broadcast_to(e, (t // blk, blk, LANES)).reshape(t, LANES)
    b3 = b.reshape(t // SUBLANES, SUBLANES, LANES)
    if m == 4:
        e = jnp.broadcast_to(b3[:, 3:4, :], b3.shape)
    else:
        sub = lax.broadcasted_iota(jnp.int32, (1, SUBLANES, 1), 1)
        e = jnp.where(sub < 4, b3[:, 1:2, :], b3[:, 5:6, :])
    return e.reshape(t, LANES)


def _hgrn_stages(blk, rows, lb_ref, g_ref, tri_ref, pair_ref, sign_ref, o_ref, tick):
    t = HG_T

    def col(part, cols):
        return blk[rows, part * BRANCH_WIDTH + cols.start:part * BRANCH_WIDTH + cols.stop]

    def front(cols):
        lb = lb_ref[:, cols]
        sig = jax.nn.sigmoid(col(1, cols).astype(F32))
        f = lb + (1.0 - lb) * sig
        kk = (1.0 - lb) * (1.0 - sig)
        hi, mid, lo = _split3(jnp.log2(f))
        tri = tri_ref[...]
        b = _dot(tri, hi) + _dot(tri, mid) + _dot(tri, lo)
        tick()
        return f, kk, b

    def middle(cols, st, f, kk, b):
        q = col(0, cols).astype(F32)
        v_bf = col(2, cols)
        b_last = b[t - 1:t, :]
        o = _dot_nt((q * jnp.exp2(b)).astype(BF16), st.astype(BF16))
        kd = (kk * jnp.exp2(b_last - b)).astype(BF16)
        st = st * jnp.exp2(b_last) + _dot_tn(v_bf, kd)
        o = o + jnp.sum(q * kk, axis=-1, keepdims=True) * v_bf.astype(F32)
        amat = jnp.zeros((t, t), F32)
        for level in range(HG_LEVELS):
            if level == 0:
                dec = jnp.where(sign_ref[0] > 0.0, f, 1.0)
            else:
                dec = jnp.exp2((b - _hgrn_edge(b, level)) * sign_ref[level])
            p = _dot_nt((q * dec).astype(BF16), (kk * dec).astype(BF16))
            amat = amat + p * pair_ref[level]
            tick()
        return st, o, amat.astype(BF16), v_bf

    def back(cols, o, amat, v_bf):
        o = o + _dot(amat, v_bf)
        ms = jnp.mean(o * o, axis=-1, keepdims=True)
        o = o * lax.rsqrt(ms + EPS) * g_ref[:, cols]
        o_ref[rows, cols] = (o * _silu(col(3, cols).astype(F32))).astype(o_ref.dtype)
        tick()

    return front, middle, back


def _mix_kernel(x_ref, g_ref, w_ref, cs_ref, lb_ref, ng_ref, tri_ref, pair_ref, sign_ref,
                pm_ref, oa_ref, oc_ref, qz_scr, kv_scr, hg_scr, st_scr, *, blocks_per_seq):
    n = pl.program_id(0)
    slot = n % 2
    bps = blocks_per_seq

    @pl.when(n == 0)
    def _():
        qz_scr[...] = jnp.zeros_like(qz_scr)
        kv_scr[...] = jnp.zeros_like(kv_scr)
        hg_scr[...] = jnp.zeros_like(hg_scr)
        st_scr[...] = jnp.zeros_like(st_scr)

    lag_pos = (n + bps - 1) % bps
    lag_kv = kv_scr.at[((n + 2 * bps - 1) // bps) % 2]
    lag_qz = qz_scr.at[1 - slot]
    lag_hg = hg_scr.at[1 - slot]
    cur_qz = qz_scr.at[slot]
    cur_hg = hg_scr.at[slot]
    cur_kv = kv_scr.at[(n // bps) % 2]
    cur_rows = pl.ds(pl.multiple_of((n % bps) * MIX_TM, MIX_TM), MIX_TM)

    def step(with_projection, with_mixers):
        if with_projection:
            x = x_ref[...]
            ms = jnp.mean(x * x, axis=-1, keepdims=True)
            hn = (x * lax.rsqrt(ms + EPS) * g_ref[...]).astype(BF16)

        def project(c):
            lo, hi = c * MIX_TN, (c + 1) * MIX_TN
            y = _dot(hn, w_ref[:, lo:hi])
            if lo < BRANCH_WIDTH:
                y = y * cs_ref[:, lo:hi]
            elif lo >= COL_GATE:
                y = jax.nn.sigmoid(y)
            elif COL_POOL + BRANCH_WIDTH <= lo < COL_HG:
                y = _silu(y)
            y = y.astype(BF16)
            if lo < COL_SB_K:
                cur_qz[:, lo:hi] = y
            elif lo < COL_SB_Z:
                cur_kv[cur_rows, lo - COL_SB_K:hi - COL_SB_K] = y
            elif lo < COL_POOL:
                cur_qz[:, lo - COL_SB_Z + BRANCH_WIDTH:hi - COL_SB_Z + BRANCH_WIDTH] = y
            elif lo < COL_HG:
                pm_ref[:, lo - COL_POOL:hi - COL_POOL] = y
            elif lo < COL_GATE:
                cur_hg[:, lo - COL_HG:hi - COL_HG] = y
            else:
                pm_ref[:, lo - COL_GATE + 2 * BRANCH_WIDTH:hi - COL_GATE + 2 * BRANCH_WIDTH] = y

        chunks = iter(range(IN_COLS // MIX_TN if with_projection else 0))
        ticks = [0]

        def tick():
            ticks[0] += 1
            if ticks[0] % MIX_TICKS_PER_CHUNK == 0:
                c = next(chunks, None)
                if c is not None:
                    project(c)

        if not with_mixers:
            for c in chunks:
                project(c)
            return

        fresh = lag_pos == 0
        heads = [slice(h * LANES, (h + 1) * LANES) for h in range(HGRN_HEADS)]
        states = [jnp.where(fresh, 0.0, st_scr[h]) for h in range(HGRN_HEADS)]
        pending = []
        for k in range(MIX_TM // HG_T):
            rows = slice(k * HG_T, (k + 1) * HG_T)
            front, middle, back = _hgrn_stages(lag_hg, rows, lb_ref, ng_ref, tri_ref, pair_ref,
                                               sign_ref, oc_ref, tick)
            fronts = [front(c) for c in heads]
            mids = [middle(c, st, *fr) for c, st, fr in zip(heads, states, fronts)]
            states = [md[0] for md in mids]
            for c, md in zip(heads, mids):
                back(c, *md[1:])

            pc = slice(k * LANES, (k + 1) * LANES)
            split, near, live, far = _sb_stages(
                lambda j, pc=pc: lag_kv[pl.ds(pl.multiple_of(j * SB_T, SB_T), SB_T), pc],
                lambda j, pc=pc: lag_kv[pl.ds(pl.multiple_of(j * SB_T, SB_T), SB_T),
                                        BRANCH_WIDTH + pc.start:BRANCH_WIDTH + pc.stop],
                tick)
            for u in range(MIX_TM // SB_T):
                qr = slice(u * SB_T, (u + 1) * SB_T)
                qa, qb = split(lag_qz[qr, pc])
                i = lag_pos * (MIX_TM // SB_T) + u
                carry = near(i, qa, qb)
                gate = _silu(lag_qz[qr, BRANCH_WIDTH + pc.start:BRANCH_WIDTH + pc.stop].astype(F32))
                pending.append((far, i, qa, qb, carry, live(carry), gate, qr, pc))
        for c in chunks:
            project(c)
        for h in range(HGRN_HEADS):
            st_scr[h] = states[h]
        outs = [far(i, qa, qb, carry, alive) for far, i, qa, qb, carry, alive, _, _, _ in pending]
        for o, (_, _, _, _, _, _, gate, qr, pc) in zip(outs, pending):
            oa_ref[qr, pc] = (o * gate).astype(oa_ref.dtype)

    step(True, True)


def _mix(h_res, g, w_bf, layer, lb, norm_g, seq):
    n = h_res.shape[0]
    nb = n // MIX_TM
    col = lax.broadcasted_iota(jnp.int32, (1, IN_COLS), 1)
    col_scale = jnp.where(col < BRANCH_WIDTH, math.log2(math.e) / math.sqrt(SB_HEAD_DIM), 1.0)
    tri, pair, sign = _hgrn_tables()
    whole = lambda arr: pl.BlockSpec(arr.shape, lambda i: (0,) * arr.ndim,
                                     pipeline_mode=pl.Buffered(1))
    cur = lambda i: (jnp.minimum(i, nb - 1), 0)
    lagged = lambda i: (jnp.maximum(i - 1, 0), 0)
    return pl.pallas_call(
        functools.partial(_mix_kernel, blocks_per_seq=seq // MIX_TM),
        grid=(nb + 1,),
        in_specs=[
            pl.BlockSpec((MIX_TM, D_MODEL), cur),
            pl.BlockSpec((1, D_MODEL), lambda i: (0, 0)),
            pl.BlockSpec((None, D_MODEL, IN_COLS), lambda i: (layer, 0, 0),
                         pipeline_mode=pl.Buffered(1)),
            pl.BlockSpec((1, IN_COLS), lambda i: (0, 0)),
            pl.BlockSpec((1, BRANCH_WIDTH), lambda i: (0, 0)),
            pl.BlockSpec((1, BRANCH_WIDTH), lambda i: (0, 0)),
            whole(tri), whole(pair), whole(sign),
        ],
        out_specs=[
            pl.BlockSpec((MIX_TM, PM_COLS), cur),
            pl.BlockSpec((MIX_TM, BRANCH_WIDTH), lagged),
            pl.BlockSpec((MIX_TM, BRANCH_WIDTH), lagged),
        ],
        out_shape=[jax.ShapeDtypeStruct((n, PM_COLS), BF16),
                   jax.ShapeDtypeStruct((n, BRANCH_WIDTH), BF16),
                   jax.ShapeDtypeStruct((n, BRANCH_WIDTH), BF16)],
        scratch_shapes=[pltpu.VMEM((2, MIX_TM, 2 * BRANCH_WIDTH), BF16),
                        pltpu.VMEM((2, seq, 2 * BRANCH_WIDTH), BF16),
                        pltpu.VMEM((2, MIX_TM, 4 * BRANCH_WIDTH), BF16),
                        pltpu.VMEM((HGRN_HEADS, LANES, LANES), F32)],
        compiler_params=pltpu.CompilerParams(dimension_semantics=("arbitrary",),
                                             vmem_limit_bytes=VMEM_LIMIT_MIX),
        name="mix",
    )(h_res, g, w_bf, col_scale.astype(F32), lb.reshape(1, BRANCH_WIDTH),
      norm_g.reshape(1, BRANCH_WIDTH), tri, pair, sign)


def _merge_kernel(oa_ref, oc_ref, pu_ref, pz_ref, ga_ref, gb_ref, gc_ref, pw_ref, ps_ref,
                  wb_ref, wo_ref, res_ref, fg_ref, out_ref, halo_scr, *, final, tiles_per_seq):
    ti = pl.program_id(0) % tiles_per_seq

    @pl.when(ti == 0)
    def _():
        halo_scr[...] = jnp.zeros_like(halo_scr)

    for r0 in range(0, MERGE_TM, MERGE_ROWS):
        rows = slice(r0, r0 + MERGE_ROWS)
        o_b = _pool_tile(pu_ref, pz_ref, pw_ref, ps_ref, halo_scr, rows, ti * MERGE_TM)
        merged = (ga_ref[rows, :].astype(F32) * _dot(oa_ref[rows, :], wb_ref[0])
                  + gb_ref[rows, :].astype(F32) * _dot(o_b, wb_ref[1])
                  + gc_ref[rows, :].astype(F32) * _dot(oc_ref[rows, :], wb_ref[2]))
        h = res_ref[rows, :] + _dot(merged.astype(BF16), wo_ref[...])
        if final:
            ms = jnp.mean(h * h, axis=-1, keepdims=True)
            h = h * lax.rsqrt(ms + EPS) * fg_ref[...]
        out_ref[rows, :] = h


def _merge(o_a, o_c, pm, pool_w_bf, pool_scale, wb_bf, wo_bf, h_res, final_g, seq, layer):
    n = h_res.shape[0]
    groups = len(POOL_WINDOWS)
    final = layer == DEPTH - 1
    row = lambda width, blk=0: pl.BlockSpec((MERGE_TM, width), lambda i: (i, blk))
    whole = lambda *shape: pl.BlockSpec((None,) + shape, lambda i: (layer,) + (0,) * len(shape))
    return pl.pallas_call(
        functools.partial(_merge_kernel, final=final, tiles_per_seq=seq // MERGE_TM),
        grid=(n // MERGE_TM,),
        in_specs=[
            row(BRANCH_WIDTH), row(BRANCH_WIDTH),
            row(BRANCH_WIDTH, 0), row(BRANCH_WIDTH, 1),
            row(D_MODEL, 1), row(D_MODEL, 2), row(D_MODEL, 3),
            whole(groups, LANES, LANES), whole(1, BRANCH_WIDTH),
            whole(3, BRANCH_WIDTH, D_MODEL), whole(D_MODEL, D_MODEL),
            row(D_MODEL), pl.BlockSpec((1, D_MODEL), lambda i: (0, 0)),
        ],
        out_specs=row(D_MODEL),
        out_shape=jax.ShapeDtypeStruct((n, D_MODEL), F32),
        scratch_shapes=[pltpu.VMEM((POOL_HALO, BRANCH_WIDTH), F32)],
        compiler_params=pltpu.CompilerParams(dimension_semantics=("arbitrary",),
                                             vmem_limit_bytes=VMEM_LIMIT_MERGE),
        name="merge_final" if final else "merge",
    )(o_a, o_c, pm, pm, pm, pm, pm, pool_w_bf,
      pool_scale.reshape(DEPTH, 1, BRANCH_WIDTH), wb_bf, wo_bf, h_res, final_g)


def kernel(x, norm_g, w_in, pool_w, pool_scale, hgrn_lb, hgrn_norm_g, w_branch, w_out, final_g):
    bsz, seq, _ = x.shape
    h_res = x.astype(F32).reshape(bsz * seq, D_MODEL)
    lb_all = jnp.cumsum(jax.nn.softmax(hgrn_lb.astype(F32), axis=0), axis=0)
    lb_all = lb_all - lb_all[:1]
    w_in_bf = w_in.astype(BF16)
    pool_w_bf = pool_w.astype(BF16)
    wb_bf = w_branch.astype(BF16)
    wo_bf = w_out.astype(BF16)
    fg = final_g.astype(F32).reshape(1, D_MODEL)
    for layer in range(DEPTH):
        pm, o_a, o_c = _mix(h_res, norm_g[layer].astype(F32).reshape(1, D_MODEL), w_in_bf, layer,
                            lb_all[layer], hgrn_norm_g[layer].astype(F32), seq)
        h_res = _merge(o_a, o_c, pm, pool_w_bf, pool_scale.astype(F32),
                       wb_bf, wo_bf, h_res, fg, seq, layer)
    return h_res.reshape(bsz, seq, D_MODEL).astype(x.dtype)
```

```python
import functools
import math

import jax
import jax.numpy as jnp
import numpy as np
from jax import lax
from jax.experimental import pallas as pl
from jax.experimental.pallas import tpu as pltpu

D_MODEL = 1024
DEPTH = 2
BRANCH_WIDTH = D_MODEL // 2
SB_HEAD_DIM = 64
POOL_WINDOWS = (2, 4, 8, 16)
POOL_HALO = 16
HGRN_HEADS = 4
EPS = 1e-6
IN_COLS = 10 * BRANCH_WIDTH + 3 * D_MODEL

LANES = 128
SUBLANES = 8
VMEM_LIMIT_MIX = 62 * 1024 * 1024
VMEM_LIMIT_MERGE = 48 * 1024 * 1024

COL_SB_Q, COL_SB_K, COL_SB_V, COL_SB_Z = 0, 512, 1024, 1536
COL_POOL, COL_HG, COL_GATE = 2048, 3072, 5120

MIX_TM, MIX_TN = 512, 256
MIX_TICKS_PER_CHUNK = 6
SB_T = 256
SB_DEAD_LOG2 = -152.0
SB_OFF_LOG2 = -1e30
SB_Z_CAP = 126.0
HG_T = 128
MERGE_TM = 512
MERGE_ROWS = 512

F32 = jnp.float32
BF16 = jnp.bfloat16


def _dot(a, b):
    return jnp.dot(a, b, preferred_element_type=F32)


def _dot_nt(a, b):
    return lax.dot_general(a, b, (((1,), (1,)), ((), ())), preferred_element_type=F32)


def _dot_tn(a, b):
    return lax.dot_general(a, b, (((0,), (0,)), ((), ())), preferred_element_type=F32)


def _split3(x):
    hi = x.astype(BF16)
    r = x - hi.astype(F32)
    mid = r.astype(BF16)
    lo = (r - mid.astype(F32)).astype(BF16)
    return hi, mid, lo


def _silu(x):
    return x * jax.nn.sigmoid(x)


def _sb_stages(ktile, vtile, tick):
    t = SB_T
    lane = lax.broadcasted_iota(jnp.int32, (1, LANES), 1)
    rows = lax.broadcasted_iota(jnp.int32, (t, t), 0)
    cols = lax.broadcasted_iota(jnp.int32, (t, t), 1)
    causal = cols < rows
    neg_upper = jnp.where(rows > cols, -1.0, 0.0).astype(BF16)

    def chain(z, masked):
        nl = jnp.maximum(z, jnp.log2(1.0 + jnp.exp2(jnp.minimum(z, SB_Z_CAP))))
        lz = z - nl
        if masked:
            nl = jnp.where(causal, nl, 0.0)
        out = lz, _dot(nl.astype(BF16), neg_upper), jnp.sum(nl, axis=-1, keepdims=True)
        tick()
        return out

    def weights(lz, rem, c, masked):
        a = jnp.exp2(lz + rem if c is None else lz + rem + c)
        if masked:
            a = jnp.where(causal, a, 0.0)
        tick()
        return a.astype(BF16)

    def live(carry):
        return jnp.max(jnp.maximum(carry[1], carry[3])) > SB_DEAD_LOG2

    def split(q):
        return (jnp.where(lane < SB_HEAD_DIM, q, jnp.zeros_like(q)),
                jnp.where(lane < SB_HEAD_DIM, jnp.zeros_like(q), q))

    def near(i, qa, qb):
        k0, v0 = ktile(i), vtile(i)
        prev = jnp.maximum(i - 1, 0)
        k1, v1 = ktile(prev), vtile(prev)
        z0a, z0b = _dot_nt(qa, k0), _dot_nt(qb, k0)
        z1a, z1b = _dot_nt(qa, k1), _dot_nt(qb, k1)
        lz0a, rem0a, rs0a = chain(z0a, True)
        lz0b, rem0b, rs0b = chain(z0b, True)
        acc_a = _dot(weights(lz0a, rem0a, None, True), v0)
        lz1a, rem1a, rs1a = chain(z1a, False)
        acc_b = _dot(weights(lz0b, rem0b, None, True), v0)
        lz1b, rem1b, rs1b = chain(z1b, False)
        off = jnp.where(i > 0, 0.0, SB_OFF_LOG2)
        c1a = off - rs0a
        c1b = off - rs0b
        acc_a = acc_a + _dot(weights(lz1a, rem1a, c1a, False), v1)
        acc_b = acc_b + _dot(weights(lz1b, rem1b, c1b, False), v1)
        return acc_a, c1a - rs1a, acc_b, c1b - rs1b

    def far(i, qa, qb, carry, alive):
        def body(state):
            j, _, (acc_a, c_a, acc_b, c_b) = state
            k, v = ktile(j), vtile(j)
            lz_a, rem_a, rs_a = chain(_dot_nt(qa, k), False)
            lz_b, rem_b, rs_b = chain(_dot_nt(qb, k), False)
            acc_a = acc_a + _dot(weights(lz_a, rem_a, c_a, False), v)
            acc_b = acc_b + _dot(weights(lz_b, rem_b, c_b, False), v)
            new = (acc_a, c_a - rs_a, acc_b, c_b - rs_b)
            return j - 1, live(new), new

        state = lax.while_loop(lambda st: (st[0] >= 0) & st[1], body, (i - 2, alive, carry))
        carry = state[2]
        return jnp.where(lane < SB_HEAD_DIM, carry[0], carry[2])

    return split, near, live, far


def _pool_tile(u_ref, z_ref, w_ref, sc_ref, halo_scr, rows, first_pos):
    tm = rows.stop - rows.start
    pos = first_pos + rows.start + lax.broadcasted_iota(jnp.int32, (tm, 1), 0)
    outs = []
    for g, w in enumerate(POOL_WINDOWS):
        cols = slice(g * LANES, (g + 1) * LANES)
        u = u_ref[rows, cols].astype(F32)
        s = jnp.concatenate([halo_scr[:, cols], u], axis=0)
        k = 1
        while k < w:
            s = s + pltpu.roll(s, k, axis=0)
            k *= 2
        cnt = jnp.minimum(pos + 1, w).astype(F32)
        d = s[POOL_HALO:, :] / cnt - u
        outs.append(_dot(d.astype(BF16), w_ref[g]))
        halo_scr[:, cols] = u[tm - POOL_HALO:, :]
    y = jnp.concatenate(outs, axis=1)
    return (y * sc_ref[...] * z_ref[rows, :].astype(F32)).astype(BF16)


HG_LEVELS = HG_T.bit_length() - 1


def _hgrn_tables():
    r = np.arange(HG_T)
    tri = (r[None, :] <= r[:, None]).astype(np.float32)
    pair = np.zeros((HG_LEVELS, HG_T, HG_T), np.float32)
    sign = np.zeros((HG_LEVELS, HG_T, LANES), np.float32)
    for l in range(HG_LEVELS):
        m = 1 << l
        second = (r % (2 * m)) >= m
        same = (r[:, None] // (2 * m)) == (r[None, :] // (2 * m))
        pair[l] = same & second[:, None] & ~second[None, :]
        sign[l] = np.where(second, 1.0, -1.0)[:, None]
    return jnp.asarray(tri, BF16), jnp.asarray(pair), jnp.asarray(sign)


def _hgrn_edge(b, level):
    t = b.shape[0]
    m = 1 << level
    if m >= SUBLANES:
        blk = 2 * m
        e = b.reshape(t // blk, blk, LANES)[:, m - 1:m, :]
        return jnp.broadcast_to(e, (t // blk, blk, LANES)).reshape(t, LANES)
    b3 = b.reshape(t // SUBLANES, SUBLANES, LANES)
    if m == 4:
        e = jnp.broadcast_to(b3[:, 3:4, :], b3.shape)
    else:
        sub = lax.broadcasted_iota(jnp.int32, (1, SUBLANES, 1), 1)
        e = jnp.where(sub < 4, b3[:, 1:2, :], b3[:, 5:6, :])
    return e.reshape(t, LANES)


def _hgrn_stages(blk, rows, lb_ref, g_ref, tri_ref, pair_ref, sign_ref, o_ref, tick):
    t = HG_T

    def col(part, cols):
        return blk[rows, part * BRANCH_WIDTH + cols.start:part * BRANCH_WIDTH + cols.stop]

    def front(cols):
        lb = lb_ref[:, cols]
        sig = jax.nn.sigmoid(col(1, cols).astype(F32))
        f = lb + (1.0 - lb) * sig
        kk = (1.0 - lb) * (1.0 - sig)
        hi, mid, lo = _split3(jnp.log2(f))
        tri = tri_ref[...]
        b = _dot(tri, hi) + _dot(tri, mid) + _dot(tri, lo)
        tick()
        return f, kk, b

    def middle(cols, st, f, kk, b):
        q = col(0, cols).astype(F32)
        v_bf = col(2, cols)
        b_last = b[t - 1:t, :]
        o = _dot_nt((q * jnp.exp2(b)).astype(BF16), st.astype(BF16))
        kd = (kk * jnp.exp2(b_last - b)).astype(BF16)
        st = st * jnp.exp2(b_last) + _dot_tn(v_bf, kd)
        o = o + jnp.sum(q * kk, axis=-1, keepdims=True) * v_bf.astype(F32)
        amat = jnp.zeros((t, t), F32)
        for level in range(HG_LEVELS):
            if level == 0:
                dec = jnp.where(sign_ref[0] > 0.0, f, 1.0)
            else:
                dec = jnp.exp2((b - _hgrn_edge(b, level)) * sign_ref[level])
            p = _dot_nt((q * dec).astype(BF16), (kk * dec).astype(BF16))
            amat = amat + p * pair_ref[level]
            tick()
        return st, o, amat.astype(BF16), v_bf

    def back(cols, o, amat, v_bf):
        o = o + _dot(amat, v_bf)
        ms = jnp.mean(o * o, axis=-1, keepdims=True)
        o = o * lax.rsqrt(ms + EPS) * g_ref[:, cols]
        o_ref[rows, cols] = (o * _silu(col(3, cols).astype(F32))).astype(o_ref.dtype)
        tick()

    return front, middle, back


def _mix_kernel(x_ref, g_ref, w_ref, cs_ref, lb_ref, ng_ref, tri_ref, pair_ref, sign_ref,
                pu_ref, pz_ref, ga_ref, gb_ref, gc_ref, oa_ref, oc_ref,
                qz_scr, kv_scr, hg_scr, st_scr, *, blocks_per_seq):
    n = pl.program_id(0)
    slot = n % 2
    bps = blocks_per_seq

    @pl.when(n == 0)
    def _():
        qz_scr[...] = jnp.zeros_like(qz_scr)
        kv_scr[...] = jnp.zeros_like(kv_scr)
        hg_scr[...] = jnp.zeros_like(hg_scr)
        st_scr[...] = jnp.zeros_like(st_scr)

    lag_pos = (n + bps - 1) % bps
    lag_kv = kv_scr.at[((n + 2 * bps - 1) // bps) % 2]
    lag_qz = qz_scr.at[1 - slot]
    lag_hg = hg_scr.at[1 - slot]
    cur_qz = qz_scr.at[slot]
    cur_hg = hg_scr.at[slot]
    cur_kv = kv_scr.at[(n // bps) % 2]
    cur_rows = pl.ds(pl.multiple_of((n % bps) * MIX_TM, MIX_TM), MIX_TM)

    def step(with_projection, with_mixers):
        if with_projection:
            x = x_ref[...]
            ms = jnp.mean(x * x, axis=-1, keepdims=True)
            hn = (x * lax.rsqrt(ms + EPS) * g_ref[...]).astype(BF16)

        def project(c):
            lo, hi = c * MIX_TN, (c + 1) * MIX_TN
            y = _dot(hn, w_ref[:, lo:hi])
            if lo < BRANCH_WIDTH:
                y = y * cs_ref[:, lo:hi]
            elif lo >= COL_GATE:
                y = jax.nn.sigmoid(y)
            elif COL_POOL + BRANCH_WIDTH <= lo < COL_HG:
                y = _silu(y)
            y = y.astype(BF16)
            if lo < COL_SB_K:
                cur_qz[:, lo:hi] = y
            elif lo < COL_SB_Z:
                cur_kv[cur_rows, lo - COL_SB_K:hi - COL_SB_K] = y
            elif lo < COL_POOL:
                cur_qz[:, lo - COL_SB_Z + BRANCH_WIDTH:hi - COL_SB_Z + BRANCH_WIDTH] = y
            elif lo < COL_HG:
                dst = pu_ref if lo < COL_POOL + BRANCH_WIDTH else pz_ref
                off = (lo - COL_POOL) % BRANCH_WIDTH
                dst[:, off:off + MIX_TN] = y
            elif lo < COL_GATE:
                cur_hg[:, lo - COL_HG:hi - COL_HG] = y
            else:
                dst = (ga_ref, gb_ref, gc_ref)[(lo - COL_GATE) // D_MODEL]
                off = (lo - COL_GATE) % D_MODEL
                dst[:, off:off + MIX_TN] = y

        chunks = iter(range(IN_COLS // MIX_TN if with_projection else 0))
        ticks = [0]

        def tick():
            ticks[0] += 1
            if ticks[0] % MIX_TICKS_PER_CHUNK == 0:
                c = next(chunks, None)
                if c is not None:
                    project(c)

        if not with_mixers:
            for c in chunks:
                project(c)
            return

        fresh = lag_pos == 0
        heads = [slice(h * LANES, (h + 1) * LANES) for h in range(HGRN_HEADS)]
        states = [jnp.where(fresh, 0.0, st_scr[h]) for h in range(HGRN_HEADS)]
        pending = []
        for k in range(MIX_TM // HG_T):
            rows = slice(k * HG_T, (k + 1) * HG_T)
            front, middle, back = _hgrn_stages(lag_hg, rows, lb_ref, ng_ref, tri_ref, pair_ref,
                                               sign_ref, oc_ref, tick)
            fronts = [front(c) for c in heads]
            mids = [middle(c, st, *fr) for c, st, fr in zip(heads, states, fronts)]
            states = [md[0] for md in mids]
            for c, md in zip(heads, mids):
                back(c, *md[1:])

            pc = slice(k * LANES, (k + 1) * LANES)
            split, near, live, far = _sb_stages(
                lambda j, pc=pc: lag_kv[pl.ds(pl.multiple_of(j * SB_T, SB_T), SB_T), pc],
                lambda j, pc=pc: lag_kv[pl.ds(pl.multiple_of(j * SB_T, SB_T), SB_T),
                                        BRANCH_WIDTH + pc.start:BRANCH_WIDTH + pc.stop],
                tick)
            for u in range(MIX_TM // SB_T):
                qr = slice(u * SB_T, (u + 1) * SB_T)
                qa, qb = split(lag_qz[qr, pc])
                i = lag_pos * (MIX_TM // SB_T) + u
                carry = near(i, qa, qb)
                gate = _silu(lag_qz[qr, BRANCH_WIDTH + pc.start:BRANCH_WIDTH + pc.stop].astype(F32))
                pending.append((far, i, qa, qb, carry, live(carry), gate, qr, pc))
        for c in chunks:
            project(c)
        for h in range(HGRN_HEADS):
            st_scr[h] = states[h]
        outs = [far(i, qa, qb, carry, alive) for far, i, qa, qb, carry, alive, _, _, _ in pending]
        for o, (_, _, _, _, _, _, gate, qr, pc) in zip(outs, pending):
            oa_ref[qr, pc] = (o * gate).astype(oa_ref.dtype)

    step(True, True)


def _mix(h_res, g, w_bf, layer, lb, norm_g, seq):
    n = h_res.shape[0]
    nb = n // MIX_TM
    col = lax.broadcasted_iota(jnp.int32, (1, IN_COLS), 1)
    col_scale = jnp.where(col < BRANCH_WIDTH, math.log2(math.e) / math.sqrt(SB_HEAD_DIM), 1.0)
    tri, pair, sign = _hgrn_tables()
    whole = lambda arr: pl.BlockSpec(arr.shape, lambda i: (0,) * arr.ndim,
                                     pipeline_mode=pl.Buffered(1))
    cur = lambda i: (jnp.minimum(i, nb - 1), 0)
    lagged = lambda i: (jnp.maximum(i - 1, 0), 0)
    return pl.pallas_call(
        functools.partial(_mix_kernel, blocks_per_seq=seq // MIX_TM),
        grid=(nb + 1,),
        in_specs=[
            pl.BlockSpec((MIX_TM, D_MODEL), cur),
            pl.BlockSpec((1, D_MODEL), lambda i: (0, 0)),
            pl.BlockSpec((None, D_MODEL, IN_COLS), lambda i: (layer, 0, 0),
                         pipeline_mode=pl.Buffered(1)),
            pl.BlockSpec((1, IN_COLS), lambda i: (0, 0)),
            pl.BlockSpec((1, BRANCH_WIDTH), lambda i: (0, 0)),
            pl.BlockSpec((1, BRANCH_WIDTH), lambda i: (0, 0)),
            whole(tri), whole(pair), whole(sign),
        ],
        out_specs=[
            pl.BlockSpec((MIX_TM, BRANCH_WIDTH), cur), pl.BlockSpec((MIX_TM, BRANCH_WIDTH), cur),
            pl.BlockSpec((MIX_TM, D_MODEL), cur), pl.BlockSpec((MIX_TM, D_MODEL), cur),
            pl.BlockSpec((MIX_TM, D_MODEL), cur),
            pl.BlockSpec((MIX_TM, BRANCH_WIDTH), lagged),
            pl.BlockSpec((MIX_TM, BRANCH_WIDTH), lagged),
        ],
        out_shape=[jax.ShapeDtypeStruct((n, BRANCH_WIDTH), BF16),
                   jax.ShapeDtypeStruct((n, BRANCH_WIDTH), BF16),
                   jax.ShapeDtypeStruct((n, D_MODEL), BF16),
                   jax.ShapeDtypeStruct((n, D_MODEL), BF16),
                   jax.ShapeDtypeStruct((n, D_MODEL), BF16),
                   jax.ShapeDtypeStruct((n, BRANCH_WIDTH), BF16),
                   jax.ShapeDtypeStruct((n, BRANCH_WIDTH), BF16)],
        scratch_shapes=[pltpu.VMEM((2, MIX_TM, 2 * BRANCH_WIDTH), BF16),
                        pltpu.VMEM((2, seq, 2 * BRANCH_WIDTH), BF16),
                        pltpu.VMEM((2, MIX_TM, 4 * BRANCH_WIDTH), BF16),
                        pltpu.VMEM((HGRN_HEADS, LANES, LANES), F32)],
        compiler_params=pltpu.CompilerParams(dimension_semantics=("arbitrary",),
                                             vmem_limit_bytes=VMEM_LIMIT_MIX),
        name="mix",
    )(h_res, g, w_bf, col_scale.astype(F32), lb.reshape(1, BRANCH_WIDTH),
      norm_g.reshape(1, BRANCH_WIDTH), tri, pair, sign)


def _merge_kernel(oa_ref, oc_ref, pu_ref, pz_ref, ga_ref, gb_ref, gc_ref, pw_ref, ps_ref,
                  wb_ref, wo_ref, res_ref, fg_ref, out_ref, halo_scr, *, final, tiles_per_seq):
    ti = pl.program_id(0) % tiles_per_seq

    @pl.when(ti == 0)
    def _():
        halo_scr[...] = jnp.zeros_like(halo_scr)

    for r0 in range(0, MERGE_TM, MERGE_ROWS):
        rows = slice(r0, r0 + MERGE_ROWS)
        o_b = _pool_tile(pu_ref, pz_ref, pw_ref, ps_ref, halo_scr, rows, ti * MERGE_TM)
        merged = (ga_ref[rows, :].astype(F32) * _dot(oa_ref[rows, :], wb_ref[0])
                  + gb_ref[rows, :].astype(F32) * _dot(o_b, wb_ref[1])
                  + gc_ref[rows, :].astype(F32) * _dot(oc_ref[rows, :], wb_ref[2]))
        h = res_ref[rows, :] + _dot(merged.astype(BF16), wo_ref[...])
        if final:
            ms = jnp.mean(h * h, axis=-1, keepdims=True)
            h = h * lax.rsqrt(ms + EPS) * fg_ref[...]
        out_ref[rows, :] = h


def _merge(o_a, o_c, pu, pz, g_a, g_b, g_c, pool_w_bf, pool_scale, wb_bf, wo_bf, h_res, final_g, seq,
           layer):
    n = h_res.shape[0]
    groups = len(POOL_WINDOWS)
    final = layer == DEPTH - 1
    row = lambda width, blk=0: pl.BlockSpec((MERGE_TM, width), lambda i: (i, blk))
    whole = lambda *shape: pl.BlockSpec((None,) + shape, lambda i: (layer,) + (0,) * len(shape))
    return pl.pallas_call(
        functools.partial(_merge_kernel, final=final, tiles_per_seq=seq // MERGE_TM),
        grid=(n // MERGE_TM,),
        in_specs=[
            row(BRANCH_WIDTH), row(BRANCH_WIDTH),
            row(BRANCH_WIDTH), row(BRANCH_WIDTH),
            row(D_MODEL), row(D_MODEL), row(D_MODEL),
            whole(groups, LANES, LANES), whole(1, BRANCH_WIDTH),
            whole(3, BRANCH_WIDTH, D_MODEL), whole(D_MODEL, D_MODEL),
            row(D_MODEL), pl.BlockSpec((1, D_MODEL), lambda i: (0, 0)),
        ],
        out_specs=row(D_MODEL),
        out_shape=jax.ShapeDtypeStruct((n, D_MODEL), F32),
        scratch_shapes=[pltpu.VMEM((POOL_HALO, BRANCH_WIDTH), F32)],
        compiler_params=pltpu.CompilerParams(dimension_semantics=("arbitrary",),
                                             vmem_limit_bytes=VMEM_LIMIT_MERGE),
        name="merge_final" if final else "merge",
    )(o_a, o_c, pu, pz, g_a, g_b, g_c, pool_w_bf,
      pool_scale.reshape(DEPTH, 1, BRANCH_WIDTH), wb_bf, wo_bf, h_res, final_g)


def kernel(x, norm_g, w_in, pool_w, pool_scale, hgrn_lb, hgrn_norm_g, w_branch, w_out, final_g):
    bsz, seq, _ = x.shape
    h_res = x.astype(F32).reshape(bsz * seq, D_MODEL)
    lb_all = jnp.cumsum(jax.nn.softmax(hgrn_lb.astype(F32), axis=0), axis=0)
    lb_all = lb_all - lb_all[:1]
    w_in_bf = w_in.astype(BF16)
    pool_w_bf = pool_w.astype(BF16)
    wb_bf = w_branch.astype(BF16)
    wo_bf = w_out.astype(BF16)
    fg = final_g.astype(F32).reshape(1, D_MODEL)
    for layer in range(DEPTH):
        *pm, o_a, o_c = _mix(h_res, norm_g[layer].astype(F32).reshape(1, D_MODEL), w_in_bf, layer,
                            lb_all[layer], hgrn_norm_g[layer].astype(F32), seq)
        h_res = _merge(o_a, o_c, *pm, pool_w_bf, pool_scale.astype(F32),
                       wb_bf, wo_bf, h_res, fg, seq, layer)
    return h_res.reshape(bsz, seq, D_MODEL).astype(x.dtype)
```

```python
import functools
import math

import jax
import jax.numpy as jnp
import numpy as np
from jax import lax
from jax.experimental import pallas as pl
from jax.experimental.pallas import tpu as pltpu

D_MODEL = 1024
DEPTH = 2
BRANCH_WIDTH = D_MODEL // 2
SB_HEAD_DIM = 64
POOL_WINDOWS = (2, 4, 8, 16)
POOL_HALO = 16
HGRN_HEADS = 4
EPS = 1e-6
IN_COLS = 10 * BRANCH_WIDTH + 3 * D_MODEL

LANES = 128
SUBLANES = 8
VMEM_LIMIT_MIX = 62 * 1024 * 1024
VMEM_LIMIT_MERGE = 48 * 1024 * 1024

COL_SB_Q, COL_SB_K, COL_SB_V, COL_SB_Z = 0, 512, 1024, 1536
COL_POOL, COL_HG, COL_GATE = 2048, 3072, 5120
PM_COLS = 2 * BRANCH_WIDTH + 3 * D_MODEL

MIX_TM, MIX_TN = 256, 256
MIX_TICKS_PER_CHUNK = 3
SB_T = 256
SB_DEAD_LOG2 = -152.0
SB_OFF_LOG2 = -1e30
SB_Z_CAP = 126.0
HG_T = 128
MERGE_TM = 512
MERGE_ROWS = 512

F32 = jnp.float32
BF16 = jnp.bfloat16


def _dot(a, b):
    return jnp.dot(a, b, preferred_element_type=F32)


def _dot_nt(a, b):
    return lax.dot_general(a, b, (((1,), (1,)), ((), ())), preferred_element_type=F32)


def _dot_tn(a, b):
    return lax.dot_general(a, b, (((0,), (0,)), ((), ())), preferred_element_type=F32)


def _split3(x):
    hi = x.astype(BF16)
    r = x - hi.astype(F32)
    mid = r.astype(BF16)
    lo = (r - mid.astype(F32)).astype(BF16)
    return hi, mid, lo


def _silu(x):
    return x * jax.nn.sigmoid(x)


def _sb_stages(ktile, vtile, tick):
    t = SB_T
    lane = lax.broadcasted_iota(jnp.int32, (1, LANES), 1)
    rows = lax.broadcasted_iota(jnp.int32, (t, t), 0)
    cols = lax.broadcasted_iota(jnp.int32, (t, t), 1)
    causal = cols < rows
    neg_upper = jnp.where(rows > cols, -1.0, 0.0).astype(BF16)

    def chain(z, masked):
        nl = jnp.maximum(z, jnp.log2(1.0 + jnp.exp2(jnp.minimum(z, SB_Z_CAP))))
        lz = z - nl
        if masked:
            nl = jnp.where(causal, nl, 0.0)
        out = lz, _dot(nl.astype(BF16), neg_upper), jnp.sum(nl, axis=-1, keepdims=True)
        tick()
        return out

    def weights(lz, rem, c, masked):
        a = jnp.exp2(lz + rem if c is None else lz + rem + c)
        if masked:
            a = jnp.where(causal, a, 0.0)
        tick()
        return a.astype(BF16)

    def live(carry):
        return jnp.max(jnp.maximum(carry[1], carry[3])) > SB_DEAD_LOG2

    def split(q):
        return (jnp.where(lane < SB_HEAD_DIM, q, jnp.zeros_like(q)),
                jnp.where(lane < SB_HEAD_DIM, jnp.zeros_like(q), q))

    def near(i, qa, qb):
        k0, v0 = ktile(i), vtile(i)
        prev = jnp.maximum(i - 1, 0)
        k1, v1 = ktile(prev), vtile(prev)
        z0a, z0b = _dot_nt(qa, k0), _dot_nt(qb, k0)
        z1a, z1b = _dot_nt(qa, k1), _dot_nt(qb, k1)
        lz0a, rem0a, rs0a = chain(z0a, True)
        lz0b, rem0b, rs0b = chain(z0b, True)
        acc_a = _dot(weights(lz0a, rem0a, None, True), v0)
        lz1a, rem1a, rs1a = chain(z1a, False)
        acc_b = _dot(weights(lz0b, rem0b, None, True), v0)
        lz1b, rem1b, rs1b = chain(z1b, False)
        off = jnp.where(i > 0, 0.0, SB_OFF_LOG2)
        c1a = off - rs0a
        c1b = off - rs0b
        acc_a = acc_a + _dot(weights(lz1a, rem1a, c1a, False), v1)
        acc_b = acc_b + _dot(weights(lz1b, rem1b, c1b, False), v1)
        return acc_a, c1a - rs1a, acc_b, c1b - rs1b

    def far(i, qa, qb, carry, alive):
        def body(state):
            j, _, (acc_a, c_a, acc_b, c_b) = state
            k, v = ktile(j), vtile(j)
            lz_a, rem_a, rs_a = chain(_dot_nt(qa, k), False)
            lz_b, rem_b, rs_b = chain(_dot_nt(qb, k), False)
            acc_a = acc_a + _dot(weights(lz_a, rem_a, c_a, False), v)
            acc_b = acc_b + _dot(weights(lz_b, rem_b, c_b, False), v)
            new = (acc_a, c_a - rs_a, acc_b, c_b - rs_b)
            return j - 1, live(new), new

        state = lax.while_loop(lambda st: (st[0] >= 0) & st[1], body, (i - 2, alive, carry))
        carry = state[2]
        return jnp.where(lane < SB_HEAD_DIM, carry[0], carry[2])

    return split, near, live, far


def _pool_tile(u_ref, z_ref, w_ref, sc_ref, halo_scr, rows, first_pos):
    tm = rows.stop - rows.start
    pos = first_pos + rows.start + lax.broadcasted_iota(jnp.int32, (tm, 1), 0)
    outs = []
    for g, w in enumerate(POOL_WINDOWS):
        cols = slice(g * LANES, (g + 1) * LANES)
        u = u_ref[rows, cols].astype(F32)
        s = jnp.concatenate([halo_scr[:, cols], u], axis=0)
        k = 1
        while k < w:
            s = s + pltpu.roll(s, k, axis=0)
            k *= 2
        cnt = jnp.minimum(pos + 1, w).astype(F32)
        d = s[POOL_HALO:, :] / cnt - u
        outs.append(_dot(d.astype(BF16), w_ref[g]))
        halo_scr[:, cols] = u[tm - POOL_HALO:, :]
    y = jnp.concatenate(outs, axis=1)
    return (y * sc_ref[...] * z_ref[rows, :].astype(F32)).astype(BF16)


HG_LEVELS = HG_T.bit_length() - 1


def _hgrn_tables():
    r = np.arange(HG_T)
    tri = (r[None, :] <= r[:, None]).astype(np.float32)
    pair = np.zeros((HG_LEVELS, HG_T, HG_T), np.float32)
    sign = np.zeros((HG_LEVELS, HG_T, LANES), np.float32)
    for l in range(HG_LEVELS):
        m = 1 << l
        second = (r % (2 * m)) >= m
        same = (r[:, None] // (2 * m)) == (r[None, :] // (2 * m))
        pair[l] = same & second[:, None] & ~second[None, :]
        sign[l] = np.where(second, 1.0, -1.0)[:, None]
    return jnp.asarray(tri, BF16), jnp.asarray(pair), jnp.asarray(sign)


def _hgrn_edge(b, level):
    t = b.shape[0]
    m = 1 << level
    if m >= SUBLANES:
        blk = 2 * m
        e = b.reshape(t // blk, blk, LANES)[:, m - 1:m, :]
        return jnp.broadcast_to(e, (t // blk, blk, LANES)).reshape(t, LANES)
    b3 = b.reshape(t // SUBLANES, SUBLANES, LANES)
    if m == 4:
        e = jnp.broadcast_to(b3[:, 3:4, :], b3.shape)
    else:
        sub = lax.broadcasted_iota(jnp.int32, (1, SUBLANES, 1), 1)
        e = jnp.where(sub < 4, b3[:, 1:2, :], b3[:, 5:6, :])
    return e.reshape(t, LANES)


def _hgrn_stages(blk, rows, lb_ref, g_ref, tri_ref, pair_ref, sign_ref, o_ref, tick):
    t = HG_T

    def col(part, cols):
        return blk[rows, part * BRANCH_WIDTH + cols.start:part * BRANCH_WIDTH + cols.stop]

    def front(cols):
        lb = lb_ref[:, cols]
        sig = jax.nn.sigmoid(col(1, cols).astype(F32))
        f = lb + (1.0 - lb) * sig
        kk = (1.0 - lb) * (1.0 - sig)
        hi, mid, lo = _split3(jnp.log2(f))
        tri = tri_ref[...]
        b = _dot(tri, hi) + _dot(tri, mid) + _dot(tri, lo)
        tick()
        return f, kk, b

    def middle(cols, st, f, kk, b):
        q = col(0, cols).astype(F32)
        v_bf = col(2, cols)
        b_last = b[t - 1:t, :]
        o = _dot_nt((q * jnp.exp2(b)).astype(BF16), st.astype(BF16))
        kd = (kk * jnp.exp2(b_last - b)).astype(BF16)
        st = st * jnp.exp2(b_last) + _dot_tn(v_bf, kd)
        o = o + jnp.sum(q * kk, axis=-1, keepdims=True) * v_bf.astype(F32)
        amat = jnp.zeros((t, t), F32)
        for level in range(HG_LEVELS):
            if level == 0:
                dec = jnp.where(sign_ref[0] > 0.0, f, 1.0)
            else:
                dec = jnp.exp2((b - _hgrn_edge(b, level)) * sign_ref[level])
            p = _dot_nt((q * dec).astype(BF16), (kk * dec).astype(BF16))
            amat = amat + p * pair_ref[level]
            tick()
        return st, o, amat.astype(BF16), v_bf

    def back(cols, o, amat, v_bf):
        o = o + _dot(amat, v_bf)
        ms = jnp.mean(o * o, axis=-1, keepdims=True)
        o = o * lax.rsqrt(ms + EPS) * g_ref[:, cols]
        o_ref[rows, cols] = (o * _silu(col(3, cols).astype(F32))).astype(o_ref.dtype)
        tick()

    return front, middle, back


def _mix_kernel(x_ref, g_ref, w_ref, cs_ref, lb_ref, ng_ref, tri_ref, pair_ref, sign_ref,
                pm_ref, oa_ref, oc_ref, qz_scr, kv_scr, hg_scr, st_scr, *, blocks_per_seq):
    n = pl.program_id(0)
    slot = n % 2
    bps = blocks_per_seq

    @pl.when(n == 0)
    def _():
        qz_scr[...] = jnp.zeros_like(qz_scr)
        kv_scr[...] = jnp.zeros_like(kv_scr)
        hg_scr[...] = jnp.zeros_like(hg_scr)
        st_scr[...] = jnp.zeros_like(st_scr)

    lag_pos = (n + bps - 1) % bps
    lag_kv = kv_scr.at[((n + 2 * bps - 1) // bps) % 2]
    lag_qz = qz_scr.at[1 - slot]
    lag_hg = hg_scr.at[1 - slot]
    cur_qz = qz_scr.at[slot]
    cur_hg = hg_scr.at[slot]
    cur_kv = kv_scr.at[(n // bps) % 2]
    cur_rows = pl.ds(pl.multiple_of((n % bps) * MIX_TM, MIX_TM), MIX_TM)

    def step(with_projection, with_mixers):
        if with_projection:
            x = x_ref[...]
            ms = jnp.mean(x * x, axis=-1, keepdims=True)
            hn = (x * lax.rsqrt(ms + EPS) * g_ref[...]).astype(BF16)

        def project(c):
            lo, hi = c * MIX_TN, (c + 1) * MIX_TN
            y = _dot(hn, w_ref[:, lo:hi])
            if lo < BRANCH_WIDTH:
                y = y * cs_ref[:, lo:hi]
            elif lo >= COL_GATE:
                y = jax.nn.sigmoid(y)
            elif COL_POOL + BRANCH_WIDTH <= lo < COL_HG:
                y = _silu(y)
            y = y.astype(BF16)
            if lo < COL_SB_K:
                cur_qz[:, lo:hi] = y
            elif lo < COL_SB_Z:
                cur_kv[cur_rows, lo - COL_SB_K:hi - COL_SB_K] = y
            elif lo < COL_POOL:
                cur_qz[:, lo - COL_SB_Z + BRANCH_WIDTH:hi - COL_SB_Z + BRANCH_WIDTH] = y
            elif lo < COL_HG:
                pm_ref[:, lo - COL_POOL:hi - COL_POOL] = y
            elif lo < COL_GATE:
                cur_hg[:, lo - COL_HG:hi - COL_HG] = y
            else:
                pm_ref[:, lo - COL_GATE + 2 * BRANCH_WIDTH:hi - COL_GATE + 2 * BRANCH_WIDTH] = y

        chunks = iter(range(IN_COLS // MIX_TN if with_projection else 0))
        ticks = [0]

        def tick():
            ticks[0] += 1
            if ticks[0] % MIX_TICKS_PER_CHUNK == 0:
                c = next(chunks, None)
                if c is not None:
                    project(c)

        if not with_mixers:
            for c in chunks:
                project(c)
            return

        fresh = lag_pos == 0
        heads = [slice(h * LANES, (h + 1) * LANES) for h in range(HGRN_HEADS)]
        states = [jnp.where(fresh, 0.0, st_scr[h]) for h in range(HGRN_HEADS)]
        pending = []
        n_tiles = MIX_TM // HG_T
        n_pairs = BRANCH_WIDTH // LANES
        for k in range(n_tiles):
            rows = slice(k * HG_T, (k + 1) * HG_T)
            front, middle, back = _hgrn_stages(lag_hg, rows, lb_ref, ng_ref, tri_ref, pair_ref,
                                               sign_ref, oc_ref, tick)
            fronts = [front(c) for c in heads]
            mids = [middle(c, st, *fr) for c, st, fr in zip(heads, states, fronts)]
            states = [md[0] for md in mids]
            for c, md in zip(heads, mids):
                back(c, *md[1:])

            for p in range(k * n_pairs // n_tiles, (k + 1) * n_pairs // n_tiles):
                pc = slice(p * LANES, (p + 1) * LANES)
                split, near, live, far = _sb_stages(
                    lambda j, pc=pc: lag_kv[pl.ds(pl.multiple_of(j * SB_T, SB_T), SB_T), pc],
                    lambda j, pc=pc: lag_kv[pl.ds(pl.multiple_of(j * SB_T, SB_T), SB_T),
                                            BRANCH_WIDTH + pc.start:BRANCH_WIDTH + pc.stop],
                    tick)
                for u in range(MIX_TM // SB_T):
                    qr = slice(u * SB_T, (u + 1) * SB_T)
                    qa, qb = split(lag_qz[qr, pc])
                    i = lag_pos * (MIX_TM // SB_T) + u
                    carry = near(i, qa, qb)
                    gate = _silu(
                        lag_qz[qr, BRANCH_WIDTH + pc.start:BRANCH_WIDTH + pc.stop].astype(F32))
                    pending.append((far, i, qa, qb, carry, live(carry), gate, qr, pc))
        for c in chunks:
            project(c)
        for h in range(HGRN_HEADS):
            st_scr[h] = states[h]
        outs = [far(i, qa, qb, carry, alive) for far, i, qa, qb, carry, alive, _, _, _ in pending]
        for o, (_, _, _, _, _, _, gate, qr, pc) in zip(outs, pending):
            oa_ref[qr, pc] = (o * gate).astype(oa_ref.dtype)

    step(True, True)


def _mix(h_res, g, w_bf, layer, lb, norm_g, seq):
    n = h_res.shape[0]
    nb = n // MIX_TM
    col = lax.broadcasted_iota(jnp.int32, (1, IN_COLS), 1)
    col_scale = jnp.where(col < BRANCH_WIDTH, math.log2(math.e) / math.sqrt(SB_HEAD_DIM), 1.0)
    tri, pair, sign = _hgrn_tables()
    whole = lambda arr: pl.BlockSpec(arr.shape, lambda i: (0,) * arr.ndim,
                                     pipeline_mode=pl.Buffered(1))
    cur = lambda i: (jnp.minimum(i, nb - 1), 0)
    lagged = lambda i: (jnp.maximum(i - 1, 0), 0)
    return pl.pallas_call(
        functools.partial(_mix_kernel, blocks_per_seq=seq // MIX_TM),
        grid=(nb + 1,),
        in_specs=[
            pl.BlockSpec((MIX_TM, D_MODEL), cur),
            pl.BlockSpec((1, D_MODEL), lambda i: (0, 0)),
            pl.BlockSpec((None, D_MODEL, IN_COLS), lambda i: (layer, 0, 0),
                         pipeline_mode=pl.Buffered(1)),
            pl.BlockSpec((1, IN_COLS), lambda i: (0, 0)),
            pl.BlockSpec((1, BRANCH_WIDTH), lambda i: (0, 0)),
            pl.BlockSpec((1, BRANCH_WIDTH), lambda i: (0, 0)),
            whole(tri), whole(pair), whole(sign),
        ],
        out_specs=[
            pl.BlockSpec((MIX_TM, PM_COLS), cur),
            pl.BlockSpec((MIX_TM, BRANCH_WIDTH), lagged),
            pl.BlockSpec((MIX_TM, BRANCH_WIDTH), lagged),
        ],
        out_shape=[jax.ShapeDtypeStruct((n, PM_COLS), BF16),
                   jax.ShapeDtypeStruct((n, BRANCH_WIDTH), BF16),
                   jax.ShapeDtypeStruct((n, BRANCH_WIDTH), BF16)],
        scratch_shapes=[pltpu.VMEM((2, MIX_TM, 2 * BRANCH_WIDTH), BF16),
                        pltpu.VMEM((2, seq, 2 * BRANCH_WIDTH), BF16),
                        pltpu.VMEM((2, MIX_TM, 4 * BRANCH_WIDTH), BF16),
                        pltpu.VMEM((HGRN_HEADS, LANES, LANES), F32)],
        compiler_params=pltpu.CompilerParams(dimension_semantics=("arbitrary",),
                                             vmem_limit_bytes=VMEM_LIMIT_MIX),
        name="mix",
    )(h_res, g, w_bf, col_scale.astype(F32), lb.reshape(1, BRANCH_WIDTH),
      norm_g.reshape(1, BRANCH_WIDTH), tri, pair, sign)


def _merge_kernel(oa_ref, oc_ref, pu_ref, pz_ref, ga_ref, gb_ref, gc_ref, pw_ref, ps_ref,
                  wb_ref, wo_ref, res_ref, fg_ref, out_ref, halo_scr, *, final, tiles_per_seq):
    ti = pl.program_id(0) % tiles_per_seq

    @pl.when(ti == 0)
    def _():
        halo_scr[...] = jnp.zeros_like(halo_scr)

    for r0 in range(0, MERGE_TM, MERGE_ROWS):
        rows = slice(r0, r0 + MERGE_ROWS)
        o_b = _pool_tile(pu_ref, pz_ref, pw_ref, ps_ref, halo_scr, rows, ti * MERGE_TM)
        merged = (ga_ref[rows, :].astype(F32) * _dot(oa_ref[rows, :], wb_ref[0])
                  + gb_ref[rows, :].astype(F32) * _dot(o_b, wb_ref[1])
                  + gc_ref[rows, :].astype(F32) * _dot(oc_ref[rows, :], wb_ref[2]))
        h = res_ref[rows, :] + _dot(merged.astype(BF16), wo_ref[...])
        if final:
            ms = jnp.mean(h * h, axis=-1, keepdims=True)
            h = h * lax.rsqrt(ms + EPS) * fg_ref[...]
        out_ref[rows, :] = h


def _merge(o_a, o_c, pm, pool_w_bf, pool_scale, wb_bf, wo_bf, h_res, final_g, seq, layer):
    n = h_res.shape[0]
    groups = len(POOL_WINDOWS)
    final = layer == DEPTH - 1
    row = lambda width, blk=0: pl.BlockSpec((MERGE_TM, width), lambda i: (i, blk))
    whole = lambda *shape: pl.BlockSpec((None,) + shape, lambda i: (layer,) + (0,) * len(shape))
    return pl.pallas_call(
        functools.partial(_merge_kernel, final=final, tiles_per_seq=seq // MERGE_TM),
        grid=(n // MERGE_TM,),
        in_specs=[
            row(BRANCH_WIDTH), row(BRANCH_WIDTH),
            row(BRANCH_WIDTH, 0), row(BRANCH_WIDTH, 1),
            row(D_MODEL, 1), row(D_MODEL, 2), row(D_MODEL, 3),
            whole(groups, LANES, LANES), whole(1, BRANCH_WIDTH),
            whole(3, BRANCH_WIDTH, D_MODEL), whole(D_MODEL, D_MODEL),
            row(D_MODEL), pl.BlockSpec((1, D_MODEL), lambda i: (0, 0)),
        ],
        out_specs=row(D_MODEL),
        out_shape=jax.ShapeDtypeStruct((n, D_MODEL), F32),
        scratch_shapes=[pltpu.VMEM((POOL_HALO, BRANCH_WIDTH), F32)],
        compiler_params=pltpu.CompilerParams(dimension_semantics=("arbitrary",),
                                             vmem_limit_bytes=VMEM_LIMIT_MERGE),
        name="merge_final" if final else "merge",
    )(o_a, o_c, pm, pm, pm, pm, pm, pool_w_bf,
      pool_scale.reshape(DEPTH, 1, BRANCH_WIDTH), wb_bf, wo_bf, h_res, final_g)


def kernel(x, norm_g, w_in, pool_w, pool_scale, hgrn_lb, hgrn_norm_g, w_branch, w_out, final_g):
    bsz, seq, _ = x.shape
    h_res = x.astype(F32).reshape(bsz * seq, D_MODEL)
    lb_all = jnp.cumsum(jax.nn.softmax(hgrn_lb.astype(F32), axis=0), axis=0)
    lb_all = lb_all - lb_all[:1]
    w_in_bf = w_in.astype(BF16)
    pool_w_bf = pool_w.astype(BF16)
    wb_bf = w_branch.astype(BF16)
    wo_bf = w_out.astype(BF16)
    fg = final_g.astype(F32).reshape(1, D_MODEL)
    for layer in range(DEPTH):
        pm, o_a, o_c = _mix(h_res, norm_g[layer].astype(F32).reshape(1, D_MODEL), w_in_bf, layer,
                            lb_all[layer], hgrn_norm_g[layer].astype(F32), seq)
        h_res = _merge(o_a, o_c, pm, pool_w_bf, pool_scale.astype(F32),
                       wb_bf, wo_bf, h_res, fg, seq, layer)
    return h_res.reshape(bsz, seq, D_MODEL).astype(x.dtype)
```

```python
import functools
import math

import jax
import jax.numpy as jnp
import numpy as np
from jax import lax
from jax.experimental import pallas as pl
from jax.experimental.pallas import tpu as pltpu

D_MODEL = 1024
DEPTH = 2
BRANCH_WIDTH = D_MODEL // 2
SB_HEAD_DIM = 64
POOL_WINDOWS = (2, 4, 8, 16)
POOL_HALO = 16
HGRN_HEADS = 4
EPS = 1e-6
IN_COLS = 10 * BRANCH_WIDTH + 3 * D_MODEL

LANES = 128
SUBLANES = 8
VMEM_LIMIT_MIX = 48 * 1024 * 1024
VMEM_LIMIT_MERGE = 48 * 1024 * 1024

COL_SB_Q, COL_SB_K, COL_SB_V, COL_SB_Z = 0, 512, 1024, 1536
COL_POOL, COL_HG, COL_GATE = 2048, 3072, 5120
PM_COLS = 3 * D_MODEL

MIX_TM, MIX_TN = 256, 256
MIX_TICKS_PER_CHUNK = 3
SB_T = 256
SB_DEAD_LOG2 = -152.0
SB_OFF_LOG2 = -1e30
SB_Z_CAP = 126.0
HG_T = 128
MERGE_TM = 512

F32 = jnp.float32
BF16 = jnp.bfloat16


def _dot(a, b):
    return jnp.dot(a, b, preferred_element_type=F32)


def _dot_nt(a, b):
    return lax.dot_general(a, b, (((1,), (1,)), ((), ())), preferred_element_type=F32)


def _dot_tn(a, b):
    return lax.dot_general(a, b, (((0,), (0,)), ((), ())), preferred_element_type=F32)


def _split3(x):
    hi = x.astype(BF16)
    r = x - hi.astype(F32)
    mid = r.astype(BF16)
    lo = (r - mid.astype(F32)).astype(BF16)
    return hi, mid, lo


def _silu(x):
    return x * jax.nn.sigmoid(x)


def _sb_stages(ktile, vtile, tick):
    t = SB_T
    lane = lax.broadcasted_iota(jnp.int32, (1, LANES), 1)
    rows = lax.broadcasted_iota(jnp.int32, (t, t), 0)
    cols = lax.broadcasted_iota(jnp.int32, (t, t), 1)
    causal = cols < rows
    neg_upper = jnp.where(rows > cols, -1.0, 0.0).astype(BF16)

    def chain(z, masked):
        nl = jnp.maximum(z, jnp.log2(1.0 + jnp.exp2(jnp.minimum(z, SB_Z_CAP))))
        lz = z - nl
        if masked:
            nl = jnp.where(causal, nl, 0.0)
        out = lz, _dot(nl.astype(BF16), neg_upper), jnp.sum(nl, axis=-1, keepdims=True)
        tick()
        return out

    def weights(lz, rem, c, masked):
        a = jnp.exp2(lz + rem if c is None else lz + rem + c)
        if masked:
            a = jnp.where(causal, a, 0.0)
        tick()
        return a.astype(BF16)

    def live(carry):
        return jnp.max(jnp.maximum(carry[1], carry[3])) > SB_DEAD_LOG2

    def split(q):
        return (jnp.where(lane < SB_HEAD_DIM, q, jnp.zeros_like(q)),
                jnp.where(lane < SB_HEAD_DIM, jnp.zeros_like(q), q))

    def near(i, qa, qb):
        k0, v0 = ktile(i), vtile(i)
        prev = jnp.maximum(i - 1, 0)
        k1, v1 = ktile(prev), vtile(prev)
        z0a, z0b = _dot_nt(qa, k0), _dot_nt(qb, k0)
        z1a, z1b = _dot_nt(qa, k1), _dot_nt(qb, k1)
        lz0a, rem0a, rs0a = chain(z0a, True)
        lz0b, rem0b, rs0b = chain(z0b, True)
        acc_a = _dot(weights(lz0a, rem0a, None, True), v0)
        lz1a, rem1a, rs1a = chain(z1a, False)
        acc_b = _dot(weights(lz0b, rem0b, None, True), v0)
        lz1b, rem1b, rs1b = chain(z1b, False)
        off = jnp.where(i > 0, 0.0, SB_OFF_LOG2)
        c1a = off - rs0a
        c1b = off - rs0b
        acc_a = acc_a + _dot(weights(lz1a, rem1a, c1a, False), v1)
        acc_b = acc_b + _dot(weights(lz1b, rem1b, c1b, False), v1)
        return acc_a, c1a - rs1a, acc_b, c1b - rs1b

    def far(i, qa, qb, carry, alive):
        def body(state):
            j, _, (acc_a, c_a, acc_b, c_b) = state
            k, v = ktile(j), vtile(j)
            lz_a, rem_a, rs_a = chain(_dot_nt(qa, k), False)
            lz_b, rem_b, rs_b = chain(_dot_nt(qb, k), False)
            acc_a = acc_a + _dot(weights(lz_a, rem_a, c_a, False), v)
            acc_b = acc_b + _dot(weights(lz_b, rem_b, c_b, False), v)
            new = (acc_a, c_a - rs_a, acc_b, c_b - rs_b)
            return j - 1, live(new), new

        state = lax.while_loop(lambda st: (st[0] >= 0) & st[1], body, (i - 2, alive, carry))
        carry = state[2]
        return jnp.where(lane < SB_HEAD_DIM, carry[0], carry[2])

    return split, near, live, far


def _pool_tile(uz_ref, w_ref, sc_ref, halo_scr, fresh, first_pos):
    tm = uz_ref.shape[0]
    pos = first_pos + lax.broadcasted_iota(jnp.int32, (tm, 1), 0)
    outs = []
    for g, w in enumerate(POOL_WINDOWS):
        cols = slice(g * LANES, (g + 1) * LANES)
        u = uz_ref[:, cols].astype(F32)
        s = jnp.concatenate([jnp.where(fresh, 0.0, halo_scr[:, cols]), u], axis=0)
        k = 1
        while k < w:
            s = s + pltpu.roll(s, k, axis=0)
            k *= 2
        cnt = jnp.minimum(pos + 1, w).astype(F32)
        d = s[POOL_HALO:, :] / cnt - u
        outs.append(_dot(d.astype(BF16), w_ref[g]))
        halo_scr[:, cols] = u[tm - POOL_HALO:, :]
    y = jnp.concatenate(outs, axis=1)
    gate = uz_ref[:, BRANCH_WIDTH:2 * BRANCH_WIDTH].astype(F32)
    return (y * sc_ref[...] * gate).astype(BF16)


HG_LEVELS = HG_T.bit_length() - 1


def _hgrn_tables():
    r = np.arange(HG_T)
    tri = (r[None, :] <= r[:, None]).astype(np.float32)
    pair = np.zeros((HG_LEVELS, HG_T, HG_T), np.float32)
    sign = np.zeros((HG_LEVELS, HG_T, LANES), np.float32)
    for l in range(HG_LEVELS):
        m = 1 << l
        second = (r % (2 * m)) >= m
        same = (r[:, None] // (2 * m)) == (r[None, :] // (2 * m))
        pair[l] = same & second[:, None] & ~second[None, :]
        sign[l] = np.where(second, 1.0, -1.0)[:, None]
    return jnp.asarray(tri, BF16), jnp.asarray(pair), jnp.asarray(sign)


def _hgrn_edge(b, level):
    t = b.shape[0]
    m = 1 << level
    if m >= SUBLANES:
        blk = 2 * m
        e = b.reshape(t // blk, blk, LANES)[:, m - 1:m, :]
        return jnp.broadcast_to(e, (t // blk, blk, LANES)).reshape(t, LANES)
    b3 = b.reshape(t // SUBLANES, SUBLANES, LANES)
    if m == 4:
        e = jnp.broadcast_to(b3[:, 3:4, :], b3.shape)
    else:
        sub = lax.broadcasted_iota(jnp.int32, (1, SUBLANES, 1), 1)
        e = jnp.where(sub < 4, b3[:, 1:2, :], b3[:, 5:6, :])
    return e.reshape(t, LANES)


def _hgrn_stages(blk, rows, lb_ref, g_ref, tri_ref, pair_ref, sign_ref, o_ref, tick):
    t = HG_T

    def col(part, cols):
        return blk[rows, part * BRANCH_WIDTH + cols.start:part * BRANCH_WIDTH + cols.stop]

    def front(cols):
        lb = lb_ref[:, cols]
        sig = jax.nn.sigmoid(col(1, cols).astype(F32))
        f = lb + (1.0 - lb) * sig
        kk = (1.0 - lb) * (1.0 - sig)
        hi, mid, lo = _split3(jnp.log2(f))
        tri = tri_ref[...]
        b = _dot(tri, hi) + _dot(tri, mid) + _dot(tri, lo)
        tick()
        return f, kk, b

    def middle(cols, st, f, kk, b):
        q = col(0, cols).astype(F32)
        v_bf = col(2, cols)
        b_last = b[t - 1:t, :]
        o = _dot_nt((q * jnp.exp2(b)).astype(BF16), st.astype(BF16))
        kd = (kk * jnp.exp2(b_last - b)).astype(BF16)
        st = st * jnp.exp2(b_last) + _dot_tn(v_bf, kd)
        o = o + jnp.sum(q * kk, axis=-1, keepdims=True) * v_bf.astype(F32)
        amat = jnp.zeros((t, t), F32)
        for level in range(HG_LEVELS):
            if level == 0:
                dec = jnp.where(sign_ref[0] > 0.0, f, 1.0)
            else:
                dec = jnp.exp2((b - _hgrn_edge(b, level)) * sign_ref[level])
            p = _dot_nt((q * dec).astype(BF16), (kk * dec).astype(BF16))
            amat = amat + p * pair_ref[level]
            tick()
        return st, o, amat.astype(BF16), v_bf

    def back(cols, o, amat, v_bf):
        o = o + _dot(amat, v_bf)
        ms = jnp.mean(o * o, axis=-1, keepdims=True)
        o = o * lax.rsqrt(ms + EPS) * g_ref[:, cols]
        o_ref[rows, cols] = (o * _silu(col(3, cols).astype(F32))).astype(o_ref.dtype)
        tick()

    return front, middle, back


def _mix_kernel(x_ref, g_ref, w_ref, cs_ref, lb_ref, ng_ref, tri_ref, pair_ref, sign_ref, pw_ref, ps_ref,
                pm_ref, oa_ref, ob_ref, oc_ref, qz_scr, kv_scr, hg_scr, st_scr, uz_scr, halo_scr, *,
                blocks_per_seq):
    n = pl.program_id(0)
    slot = n % 2
    bps = blocks_per_seq

    @pl.when(n == 0)
    def _():
        qz_scr[...] = jnp.zeros_like(qz_scr)
        kv_scr[...] = jnp.zeros_like(kv_scr)
        hg_scr[...] = jnp.zeros_like(hg_scr)
        st_scr[...] = jnp.zeros_like(st_scr)
        uz_scr[...] = jnp.zeros_like(uz_scr)
        halo_scr[...] = jnp.zeros_like(halo_scr)

    lag_pos = (n + bps - 1) % bps
    lag_kv = kv_scr.at[((n + 2 * bps - 1) // bps) % 2]
    lag_qz = qz_scr.at[1 - slot]
    lag_hg = hg_scr.at[1 - slot]
    cur_qz = qz_scr.at[slot]
    cur_hg = hg_scr.at[slot]
    lag_uz = uz_scr.at[1 - slot]
    cur_uz = uz_scr.at[slot]
    cur_kv = kv_scr.at[(n // bps) % 2]
    cur_rows = pl.ds(pl.multiple_of((n % bps) * MIX_TM, MIX_TM), MIX_TM)

    def step(with_projection, with_mixers):
        if with_projection:
            x = x_ref[...]
            ms = jnp.mean(x * x, axis=-1, keepdims=True)
            hn = (x * lax.rsqrt(ms + EPS) * g_ref[...]).astype(BF16)

        def project(c):
            lo, hi = c * MIX_TN, (c + 1) * MIX_TN
            y = _dot(hn, w_ref[:, lo:hi])
            if lo < BRANCH_WIDTH:
                y = y * cs_ref[:, lo:hi]
            elif lo >= COL_GATE:
                y = jax.nn.sigmoid(y)
            elif COL_POOL + BRANCH_WIDTH <= lo < COL_HG:
                y = _silu(y)
            y = y.astype(BF16)
            if lo < COL_SB_K:
                cur_qz[:, lo:hi] = y
            elif lo < COL_SB_Z:
                cur_kv[cur_rows, lo - COL_SB_K:hi - COL_SB_K] = y
            elif lo < COL_POOL:
                cur_qz[:, lo - COL_SB_Z + BRANCH_WIDTH:hi - COL_SB_Z + BRANCH_WIDTH] = y
            elif lo < COL_HG:
                cur_uz[:, lo - COL_POOL:hi - COL_POOL] = y
            elif lo < COL_GATE:
                cur_hg[:, lo - COL_HG:hi - COL_HG] = y
            else:
                pm_ref[:, lo - COL_GATE:hi - COL_GATE] = y

        chunks = iter(range(IN_COLS // MIX_TN if with_projection else 0))
        ticks = [0]

        def tick():
            ticks[0] += 1
            if ticks[0] % MIX_TICKS_PER_CHUNK == 0:
                c = next(chunks, None)
                if c is not None:
                    project(c)

        if not with_mixers:
            for c in chunks:
                project(c)
            return

        fresh = lag_pos == 0
        ob_ref[...] = _pool_tile(lag_uz, pw_ref, ps_ref, halo_scr, fresh, lag_pos * MIX_TM)
        heads = [slice(h * LANES, (h + 1) * LANES) for h in range(HGRN_HEADS)]
        states = [jnp.where(fresh, 0.0, st_scr[h]) for h in range(HGRN_HEADS)]
        pending = []
        n_tiles = MIX_TM // HG_T
        n_pairs = BRANCH_WIDTH // LANES
        for k in range(n_tiles):
            rows = slice(k * HG_T, (k + 1) * HG_T)
            front, middle, back = _hgrn_stages(lag_hg, rows, lb_ref, ng_ref, tri_ref, pair_ref,
                                               sign_ref, oc_ref, tick)
            fronts = [front(c) for c in heads]
            mids = [middle(c, st, *fr) for c, st, fr in zip(heads, states, fronts)]
            states = [md[0] for md in mids]
            for c, md in zip(heads, mids):
                back(c, *md[1:])

            for p in range(k * n_pairs // n_tiles, (k + 1) * n_pairs // n_tiles):
                pc = slice(p * LANES, (p + 1) * LANES)
                split, near, live, far = _sb_stages(
                    lambda j, pc=pc: lag_kv[pl.ds(pl.multiple_of(j * SB_T, SB_T), SB_T), pc],
                    lambda j, pc=pc: lag_kv[pl.ds(pl.multiple_of(j * SB_T, SB_T), SB_T),
                                            BRANCH_WIDTH + pc.start:BRANCH_WIDTH + pc.stop],
                    tick)
                for u in range(MIX_TM // SB_T):
                    qr = slice(u * SB_T, (u + 1) * SB_T)
                    qa, qb = split(lag_qz[qr, pc])
                    i = lag_pos * (MIX_TM // SB_T) + u
                    carry = near(i, qa, qb)
                    gate = _silu(
                        lag_qz[qr, BRANCH_WIDTH + pc.start:BRANCH_WIDTH + pc.stop].astype(F32))
                    pending.append((far, i, qa, qb, carry, live(carry), gate, qr, pc))
        for c in chunks:
            project(c)
        for h in range(HGRN_HEADS):
            st_scr[h] = states[h]
        outs = [far(i, qa, qb, carry, alive) for far, i, qa, qb, carry, alive, _, _, _ in pending]
        for o, (_, _, _, _, _, _, gate, qr, pc) in zip(outs, pending):
            oa_ref[qr, pc] = (o * gate).astype(oa_ref.dtype)

    step(True, True)


def _mix(h_res, g, w_bf, layer, lb, norm_g, pool_w_bf, pool_scale, seq):
    n = h_res.shape[0]
    nb = n // MIX_TM
    col = lax.broadcasted_iota(jnp.int32, (1, IN_COLS), 1)
    col_scale = jnp.where(col < BRANCH_WIDTH, math.log2(math.e) / math.sqrt(SB_HEAD_DIM), 1.0)
    tri, pair, sign = _hgrn_tables()
    whole = lambda arr: pl.BlockSpec(arr.shape, lambda i: (0,) * arr.ndim,
                                     pipeline_mode=pl.Buffered(1))
    cur = lambda i: (jnp.minimum(i, nb - 1), 0)
    lagged = lambda i: (jnp.maximum(i - 1, 0), 0)
    return pl.pallas_call(
        functools.partial(_mix_kernel, blocks_per_seq=seq // MIX_TM),
        grid=(nb + 1,),
        in_specs=[
            pl.BlockSpec((MIX_TM, D_MODEL), cur),
            pl.BlockSpec((1, D_MODEL), lambda i: (0, 0)),
            pl.BlockSpec((None, D_MODEL, IN_COLS), lambda i: (layer, 0, 0),
                         pipeline_mode=pl.Buffered(1)),
            pl.BlockSpec((1, IN_COLS), lambda i: (0, 0)),
            pl.BlockSpec((1, BRANCH_WIDTH), lambda i: (0, 0)),
            pl.BlockSpec((1, BRANCH_WIDTH), lambda i: (0, 0)),
            whole(tri), whole(pair), whole(sign),
            pl.BlockSpec((None, len(POOL_WINDOWS), LANES, LANES), lambda i: (layer, 0, 0, 0),
                         pipeline_mode=pl.Buffered(1)),
            pl.BlockSpec((None, 1, BRANCH_WIDTH), lambda i: (layer, 0, 0),
                         pipeline_mode=pl.Buffered(1)),
        ],
        out_specs=[
            pl.BlockSpec((MIX_TM, PM_COLS), cur),
            pl.BlockSpec((MIX_TM, BRANCH_WIDTH), lagged),
            pl.BlockSpec((MIX_TM, BRANCH_WIDTH), lagged),
            pl.BlockSpec((MIX_TM, BRANCH_WIDTH), lagged),
        ],
        out_shape=[jax.ShapeDtypeStruct((n, PM_COLS), BF16),
                   jax.ShapeDtypeStruct((n, BRANCH_WIDTH), BF16),
                   jax.ShapeDtypeStruct((n, BRANCH_WIDTH), BF16),
                   jax.ShapeDtypeStruct((n, BRANCH_WIDTH), BF16)],
        scratch_shapes=[pltpu.VMEM((2, MIX_TM, 2 * BRANCH_WIDTH), BF16),
                        pltpu.VMEM((2, seq, 2 * BRANCH_WIDTH), BF16),
                        pltpu.VMEM((2, MIX_TM, 4 * BRANCH_WIDTH), BF16),
                        pltpu.VMEM((HGRN_HEADS, LANES, LANES), F32),
                        pltpu.VMEM((2, MIX_TM, 2 * BRANCH_WIDTH), BF16),
                        pltpu.VMEM((POOL_HALO, BRANCH_WIDTH), F32)],
        compiler_params=pltpu.CompilerParams(dimension_semantics=("arbitrary",),
                                             vmem_limit_bytes=VMEM_LIMIT_MIX),
        name="mix",
    )(h_res, g, w_bf, col_scale.astype(F32), lb.reshape(1, BRANCH_WIDTH),
      norm_g.reshape(1, BRANCH_WIDTH), tri, pair, sign, pool_w_bf,
      pool_scale.reshape(DEPTH, 1, BRANCH_WIDTH))


def _merge_kernel(oa_ref, ob_ref, oc_ref, ga_ref, gb_ref, gc_ref, wb_ref, wo_ref, res_ref, fg_ref,
                  out_ref, *, final):
    merged = (ga_ref[...].astype(F32) * _dot(oa_ref[...], wb_ref[0])
              + gb_ref[...].astype(F32) * _dot(ob_ref[...], wb_ref[1])
              + gc_ref[...].astype(F32) * _dot(oc_ref[...], wb_ref[2]))
    h = res_ref[...] + _dot(merged.astype(BF16), wo_ref[...])
    if final:
        ms = jnp.mean(h * h, axis=-1, keepdims=True)
        h = h * lax.rsqrt(ms + EPS) * fg_ref[...]
    out_ref[...] = h


def _merge(o_a, o_b, o_c, pm, wb_bf, wo_bf, h_res, final_g, layer):
    n = h_res.shape[0]
    final = layer == DEPTH - 1
    row = lambda width, blk=0: pl.BlockSpec((MERGE_TM, width), lambda i: (i, blk))
    whole = lambda *shape: pl.BlockSpec((None,) + shape, lambda i: (layer,) + (0,) * len(shape))
    return pl.pallas_call(
        functools.partial(_merge_kernel, final=final),
        grid=(n // MERGE_TM,),
        in_specs=[
            row(BRANCH_WIDTH), row(BRANCH_WIDTH), row(BRANCH_WIDTH),
            row(D_MODEL, 0), row(D_MODEL, 1), row(D_MODEL, 2),
            whole(3, BRANCH_WIDTH, D_MODEL), whole(D_MODEL, D_MODEL),
            row(D_MODEL), pl.BlockSpec((1, D_MODEL), lambda i: (0, 0)),
        ],
        out_specs=row(D_MODEL),
        out_shape=jax.ShapeDtypeStruct((n, D_MODEL), F32),
        compiler_params=pltpu.CompilerParams(dimension_semantics=("parallel",),
                                             vmem_limit_bytes=VMEM_LIMIT_MERGE),
        name="merge_final" if final else "merge",
    )(o_a, o_b, o_c, pm, pm, pm, wb_bf, wo_bf, h_res, final_g)


def kernel(x, norm_g, w_in, pool_w, pool_scale, hgrn_lb, hgrn_norm_g, w_branch, w_out, final_g):
    bsz, seq, _ = x.shape
    h_res = x.astype(F32).reshape(bsz * seq, D_MODEL)
    lb_all = jnp.cumsum(jax.nn.softmax(hgrn_lb.astype(F32), axis=0), axis=0)
    lb_all = lb_all - lb_all[:1]
    w_in_bf = w_in.astype(BF16)
    pool_w_bf = pool_w.astype(BF16)
    wb_bf = w_branch.astype(BF16)
    wo_bf = w_out.astype(BF16)
    fg = final_g.astype(F32).reshape(1, D_MODEL)
    for layer in range(DEPTH):
        pm, o_a, o_b, o_c = _mix(h_res, norm_g[layer].astype(F32).reshape(1, D_MODEL), w_in_bf, layer,
                                 lb_all[layer], hgrn_norm_g[layer].astype(F32), pool_w_bf,
                                 pool_scale.astype(F32), seq)
        h_res = _merge(o_a, o_b, o_c, pm, wb_bf, wo_bf, h_res, fg, layer)
    return h_res.reshape(bsz, seq, D_MODEL).astype(x.dtype)
```

```python
import functools
import math

import jax
import jax.numpy as jnp
import numpy as np
from jax import lax
from jax.experimental import pallas as pl
from jax.experimental.pallas import tpu as pltpu

D_MODEL = 1024
DEPTH = 2
BRANCH_WIDTH = D_MODEL // 2
SB_HEAD_DIM = 64
POOL_WINDOWS = (2, 4, 8, 16)
POOL_HALO = 16
HGRN_HEADS = 4
EPS = 1e-6
IN_COLS = 10 * BRANCH_WIDTH + 3 * D_MODEL

LANES = 128
SUBLANES = 8
VMEM_LIMIT_MIX = 62 * 1024 * 1024
VMEM_LIMIT_MERGE = 56 * 1024 * 1024

COL_SB_Q, COL_SB_K, COL_SB_V, COL_SB_Z = 0, 512, 1024, 1536
COL_POOL, COL_HG, COL_GATE = 2048, 3072, 5120
PM_COLS = 2 * BRANCH_WIDTH + 3 * D_MODEL

MIX_TM, MIX_TN = 512, 256
MIX_TICKS_PER_CHUNK = 6
SB_T = 256
SB_DEAD_LOG2 = -152.0
SB_OFF_LOG2 = -1e30
SB_Z_CAP = 126.0
HG_T = 128
MERGE_TM = 1024
MERGE_ROWS = 1024

F32 = jnp.float32
BF16 = jnp.bfloat16


def _dot(a, b):
    return jnp.dot(a, b, preferred_element_type=F32)


def _dot_nt(a, b):
    return lax.dot_general(a, b, (((1,), (1,)), ((), ())), preferred_element_type=F32)


def _dot_tn(a, b):
    return lax.dot_general(a, b, (((0,), (0,)), ((), ())), preferred_element_type=F32)


def _split3(x):
    hi = x.astype(BF16)
    r = x - hi.astype(F32)
    mid = r.astype(BF16)
    lo = (r - mid.astype(F32)).astype(BF16)
    return hi, mid, lo


def _silu(x):
    return x * jax.nn.sigmoid(x)


def _sb_stages(ktile, vtile, tick):
    t = SB_T
    lane = lax.broadcasted_iota(jnp.int32, (1, LANES), 1)
    rows = lax.broadcasted_iota(jnp.int32, (t, t), 0)
    cols = lax.broadcasted_iota(jnp.int32, (t, t), 1)
    causal = cols < rows
    neg_upper = jnp.where(rows > cols, -1.0, 0.0).astype(BF16)

    def chain(z, masked):
        nl = jnp.maximum(z, jnp.log2(1.0 + jnp.exp2(jnp.minimum(z, SB_Z_CAP))))
        lz = z - nl
        if masked:
            nl = jnp.where(causal, nl, 0.0)
        out = lz, _dot(nl.astype(BF16), neg_upper), jnp.sum(nl, axis=-1, keepdims=True)
        tick()
        return out

    def weights(lz, rem, c, masked):
        a = jnp.exp2(lz + rem if c is None else lz + rem + c)
        if masked:
            a = jnp.where(causal, a, 0.0)
        tick()
        return a.astype(BF16)

    def live(carry):
        return jnp.max(jnp.maximum(carry[1], carry[3])) > SB_DEAD_LOG2

    def split(q):
        return (jnp.where(lane < SB_HEAD_DIM, q, jnp.zeros_like(q)),
                jnp.where(lane < SB_HEAD_DIM, jnp.zeros_like(q), q))

    def near(i, qa, qb):
        k0, v0 = ktile(i), vtile(i)
        prev = jnp.maximum(i - 1, 0)
        k1, v1 = ktile(prev), vtile(prev)
        z0a, z0b = _dot_nt(qa, k0), _dot_nt(qb, k0)
        z1a, z1b = _dot_nt(qa, k1), _dot_nt(qb, k1)
        lz0a, rem0a, rs0a = chain(z0a, True)
        lz0b, rem0b, rs0b = chain(z0b, True)
        acc_a = _dot(weights(lz0a, rem0a, None, True), v0)
        lz1a, rem1a, rs1a = chain(z1a, False)
        acc_b = _dot(weights(lz0b, rem0b, None, True), v0)
        lz1b, rem1b, rs1b = chain(z1b, False)
        off = jnp.where(i > 0, 0.0, SB_OFF_LOG2)
        c1a = off - rs0a
        c1b = off - rs0b
        acc_a = acc_a + _dot(weights(lz1a, rem1a, c1a, False), v1)
        acc_b = acc_b + _dot(weights(lz1b, rem1b, c1b, False), v1)
        return acc_a, c1a - rs1a, acc_b, c1b - rs1b

    def far(i, qa, qb, carry, alive):
        def body(state):
            j, _, (acc_a, c_a, acc_b, c_b) = state
            k, v = ktile(j), vtile(j)
            lz_a, rem_a, rs_a = chain(_dot_nt(qa, k), False)
            lz_b, rem_b, rs_b = chain(_dot_nt(qb, k), False)
            acc_a = acc_a + _dot(weights(lz_a, rem_a, c_a, False), v)
            acc_b = acc_b + _dot(weights(lz_b, rem_b, c_b, False), v)
            new = (acc_a, c_a - rs_a, acc_b, c_b - rs_b)
            return j - 1, live(new), new

        state = lax.while_loop(lambda st: (st[0] >= 0) & st[1], body, (i - 2, alive, carry))
        carry = state[2]
        return jnp.where(lane < SB_HEAD_DIM, carry[0], carry[2])

    return split, near, live, far


def _pool_tile(u_ref, z_ref, w_ref, sc_ref, halo_scr, rows, first_pos):
    tm = rows.stop - rows.start
    pos = first_pos + rows.start + lax.broadcasted_iota(jnp.int32, (tm, 1), 0)
    outs = []
    for g, w in enumerate(POOL_WINDOWS):
        cols = slice(g * LANES, (g + 1) * LANES)
        u = u_ref[rows, cols].astype(F32)
        s = jnp.concatenate([halo_scr[:, cols], u], axis=0)
        k = 1
        while k < w:
            s = s + pltpu.roll(s, k, axis=0)
            k *= 2
        cnt = jnp.minimum(pos + 1, w).astype(F32)
        d = s[POOL_HALO:, :] / cnt - u
        outs.append(_dot(d.astype(BF16), w_ref[g]))
        halo_scr[:, cols] = u[tm - POOL_HALO:, :]
    y = jnp.concatenate(outs, axis=1)
    return (y * sc_ref[...] * z_ref[rows, :].astype(F32)).astype(BF16)


HG_LEVELS = HG_T.bit_length() - 1


def _hgrn_tables():
    r = np.arange(HG_T)
    tri = (r[None, :] <= r[:, None]).astype(np.float32)
    pair = np.zeros((HG_LEVELS, HG_T, HG_T), np.float32)
    sign = np.zeros((HG_LEVELS, HG_T, LANES), np.float32)
    for l in range(HG_LEVELS):
        m = 1 << l
        second = (r % (2 * m)) >= m
        same = (r[:, None] // (2 * m)) == (r[None, :] // (2 * m))
        pair[l] = same & second[:, None] & ~second[None, :]
        sign[l] = np.where(second, 1.0, -1.0)[:, None]
    return jnp.asarray(tri, BF16), jnp.asarray(pair), jnp.asarray(sign)


def _hgrn_edge(b, level):
    t = b.shape[0]
    m = 1 << level
    if m >= SUBLANES:
        blk = 2 * m
        e = b.reshape(t // blk, blk, LANES)[:, m - 1:m, :]
        return jnp.broadcast_to(e, (t // blk, blk, LANES)).reshape(t, LANES)
    b3 = b.reshape(t // SUBLANES, SUBLANES, LANES)
    if m == 4:
        e = jnp.broadcast_to(b3[:, 3:4, :], b3.shape)
    else:
        sub = lax.broadcasted_iota(jnp.int32, (1, SUBLANES, 1), 1)
        e = jnp.where(sub < 4, b3[:, 1:2, :], b3[:, 5:6, :])
    return e.reshape(t, LANES)


def _hgrn_stages(blk, rows, lb_ref, g_ref, tri_ref, pair_ref, sign_ref, o_ref, tick):
    t = HG_T

    def col(part, cols):
        return blk[rows, part * BRANCH_WIDTH + cols.start:part * BRANCH_WIDTH + cols.stop]

    def front(cols):
        lb = lb_ref[:, cols]
        sig = jax.nn.sigmoid(col(1, cols).astype(F32))
        f = lb + (1.0 - lb) * sig
        kk = (1.0 - lb) * (1.0 - sig)
        hi, mid, lo = _split3(jnp.log2(f))
        tri = tri_ref[...]
        b = _dot(tri, hi) + _dot(tri, mid) + _dot(tri, lo)
        tick()
        return f, kk, b

    def middle(cols, st, f, kk, b):
        q = col(0, cols).astype(F32)
        v_bf = col(2, cols)
        b_last = b[t - 1:t, :]
        o = _dot_nt((q * jnp.exp2(b)).astype(BF16), st.astype(BF16))
        kd = (kk * jnp.exp2(b_last - b)).astype(BF16)
        st = st * jnp.exp2(b_last) + _dot_tn(v_bf, kd)
        o = o + jnp.sum(q * kk, axis=-1, keepdims=True) * v_bf.astype(F32)
        amat = jnp.zeros((t, t), F32)
        for level in range(HG_LEVELS):
            if level == 0:
                dec = jnp.where(sign_ref[0] > 0.0, f, 1.0)
            else:
                dec = jnp.exp2((b - _hgrn_edge(b, level)) * sign_ref[level])
            p = _dot_nt((q * dec).astype(BF16), (kk * dec).astype(BF16))
            amat = amat + p * pair_ref[level]
            tick()
        return st, o, amat.astype(BF16), v_bf

    def back(cols, o, amat, v_bf):
        o = o + _dot(amat, v_bf)
        ms = jnp.mean(o * o, axis=-1, keepdims=True)
        o = o * lax.rsqrt(ms + EPS) * g_ref[:, cols]
        o_ref[rows, cols] = (o * _silu(col(3, cols).astype(F32))).astype(o_ref.dtype)
        tick()

    return front, middle, back


def _mix_kernel(x_ref, g_ref, w_ref, cs_ref, lb_ref, ng_ref, tri_ref, pair_ref, sign_ref,
                pm_ref, oa_ref, oc_ref, qz_scr, kv_scr, hg_scr, st_scr, *, blocks_per_seq):
    n = pl.program_id(0)
    slot = n % 2
    bps = blocks_per_seq

    @pl.when(n == 0)
    def _():
        qz_scr[...] = jnp.zeros_like(qz_scr)
        kv_scr[...] = jnp.zeros_like(kv_scr)
        hg_scr[...] = jnp.zeros_like(hg_scr)
        st_scr[...] = jnp.zeros_like(st_scr)

    lag_pos = (n + bps - 1) % bps
    lag_kv = kv_scr.at[((n + 2 * bps - 1) // bps) % 2]
    lag_qz = qz_scr.at[1 - slot]
    lag_hg = hg_scr.at[1 - slot]
    cur_qz = qz_scr.at[slot]
    cur_hg = hg_scr.at[slot]
    cur_kv = kv_scr.at[(n // bps) % 2]
    cur_rows = pl.ds(pl.multiple_of((n % bps) * MIX_TM, MIX_TM), MIX_TM)

    def step(with_projection, with_mixers):
        if with_projection:
            x = x_ref[...]
            ms = jnp.mean(x * x, axis=-1, keepdims=True)
            hn = (x * lax.rsqrt(ms + EPS) * g_ref[...]).astype(BF16)

        def project(c):
            lo, hi = c * MIX_TN, (c + 1) * MIX_TN
            y = _dot(hn, w_ref[:, lo:hi])
            if lo < BRANCH_WIDTH:
                y = y * cs_ref[:, lo:hi]
            elif lo >= COL_GATE:
                y = jax.nn.sigmoid(y)
            elif COL_POOL + BRANCH_WIDTH <= lo < COL_HG:
                y = _silu(y)
            y = y.astype(BF16)
            if lo < COL_SB_K:
                cur_qz[:, lo:hi] = y
            elif lo < COL_SB_Z:
                cur_kv[cur_rows, lo - COL_SB_K:hi - COL_SB_K] = y
            elif lo < COL_POOL:
                cur_qz[:, lo - COL_SB_Z + BRANCH_WIDTH:hi - COL_SB_Z + BRANCH_WIDTH] = y
            elif lo < COL_HG:
                pm_ref[:, lo - COL_POOL:hi - COL_POOL] = y
            elif lo < COL_GATE:
                cur_hg[:, lo - COL_HG:hi - COL_HG] = y
            else:
                pm_ref[:, lo - COL_GATE + 2 * BRANCH_WIDTH:hi - COL_GATE + 2 * BRANCH_WIDTH] = y

        chunks = iter(range(IN_COLS // MIX_TN if with_projection else 0))
        ticks = [0]

        def tick():
            ticks[0] += 1
            if ticks[0] % MIX_TICKS_PER_CHUNK == 0:
                c = next(chunks, None)
                if c is not None:
                    project(c)

        if not with_mixers:
            for c in chunks:
                project(c)
            return

        fresh = lag_pos == 0
        heads = [slice(h * LANES, (h + 1) * LANES) for h in range(HGRN_HEADS)]
        states = [jnp.where(fresh, 0.0, st_scr[h]) for h in range(HGRN_HEADS)]
        pending = []
        for k in range(MIX_TM // HG_T):
            rows = slice(k * HG_T, (k + 1) * HG_T)
            front, middle, back = _hgrn_stages(lag_hg, rows, lb_ref, ng_ref, tri_ref, pair_ref,
                                               sign_ref, oc_ref, tick)
            fronts = [front(c) for c in heads]
            mids = [middle(c, st, *fr) for c, st, fr in zip(heads, states, fronts)]
            states = [md[0] for md in mids]
            for c, md in zip(heads, mids):
                back(c, *md[1:])

            pc = slice(k * LANES, (k + 1) * LANES)
            split, near, live, far = _sb_stages(
                lambda j, pc=pc: lag_kv[pl.ds(pl.multiple_of(j * SB_T, SB_T), SB_T), pc],
                lambda j, pc=pc: lag_kv[pl.ds(pl.multiple_of(j * SB_T, SB_T), SB_T),
                                        BRANCH_WIDTH + pc.start:BRANCH_WIDTH + pc.stop],
                tick)
            for u in range(MIX_TM // SB_T):
                qr = slice(u * SB_T, (u + 1) * SB_T)
                qa, qb = split(lag_qz[qr, pc])
                i = lag_pos * (MIX_TM // SB_T) + u
                carry = near(i, qa, qb)
                gate = _silu(lag_qz[qr, BRANCH_WIDTH + pc.start:BRANCH_WIDTH + pc.stop].astype(F32))
                pending.append((far, i, qa, qb, carry, live(carry), gate, qr, pc))
        for c in chunks:
            project(c)
        for h in range(HGRN_HEADS):
            st_scr[h] = states[h]
        outs = [far(i, qa, qb, carry, alive) for far, i, qa, qb, carry, alive, _, _, _ in pending]
        for o, (_, _, _, _, _, _, gate, qr, pc) in zip(outs, pending):
            oa_ref[qr, pc] = (o * gate).astype(oa_ref.dtype)

    step(True, True)


def _mix(h_res, g, w_bf, layer, lb, norm_g, seq):
    n = h_res.shape[0]
    nb = n // MIX_TM
    col = lax.broadcasted_iota(jnp.int32, (1, IN_COLS), 1)
    col_scale = jnp.where(col < BRANCH_WIDTH, math.log2(math.e) / math.sqrt(SB_HEAD_DIM), 1.0)
    tri, pair, sign = _hgrn_tables()
    whole = lambda arr: pl.BlockSpec(arr.shape, lambda i: (0,) * arr.ndim,
                                     pipeline_mode=pl.Buffered(1))
    cur = lambda i: (jnp.minimum(i, nb - 1), 0)
    lagged = lambda i: (jnp.maximum(i - 1, 0), 0)
    return pl.pallas_call(
        functools.partial(_mix_kernel, blocks_per_seq=seq // MIX_TM),
        grid=(nb + 1,),
        in_specs=[
            pl.BlockSpec((MIX_TM, D_MODEL), cur),
            pl.BlockSpec((1, D_MODEL), lambda i: (0, 0)),
            pl.BlockSpec((None, D_MODEL, IN_COLS), lambda i: (layer, 0, 0),
                         pipeline_mode=pl.Buffered(1)),
            pl.BlockSpec((1, IN_COLS), lambda i: (0, 0)),
            pl.BlockSpec((1, BRANCH_WIDTH), lambda i: (0, 0)),
            pl.BlockSpec((1, BRANCH_WIDTH), lambda i: (0, 0)),
            whole(tri), whole(pair), whole(sign),
        ],
        out_specs=[
            pl.BlockSpec((MIX_TM, PM_COLS), cur),
            pl.BlockSpec((MIX_TM, BRANCH_WIDTH), lagged),
            pl.BlockSpec((MIX_TM, BRANCH_WIDTH), lagged),
        ],
        out_shape=[jax.ShapeDtypeStruct((n, PM_COLS), BF16),
                   jax.ShapeDtypeStruct((n, BRANCH_WIDTH), BF16),
                   jax.ShapeDtypeStruct((n, BRANCH_WIDTH), BF16)],
        scratch_shapes=[pltpu.VMEM((2, MIX_TM, 2 * BRANCH_WIDTH), BF16),
                        pltpu.VMEM((2, seq, 2 * BRANCH_WIDTH), BF16),
                        pltpu.VMEM((2, MIX_TM, 4 * BRANCH_WIDTH), BF16),
                        pltpu.VMEM((HGRN_HEADS, LANES, LANES), F32)],
        compiler_params=pltpu.CompilerParams(dimension_semantics=("arbitrary",),
                                             vmem_limit_bytes=VMEM_LIMIT_MIX),
        name="mix",
    )(h_res, g, w_bf, col_scale.astype(F32), lb.reshape(1, BRANCH_WIDTH),
      norm_g.reshape(1, BRANCH_WIDTH), tri, pair, sign)


def _merge_kernel(oa_ref, oc_ref, pu_ref, pz_ref, ga_ref, gb_ref, gc_ref, pw_ref, ps_ref,
                  wb_ref, wo_ref, res_ref, fg_ref, out_ref, halo_scr, *, final, tiles_per_seq):
    ti = pl.program_id(0) % tiles_per_seq

    @pl.when(ti == 0)
    def _():
        halo_scr[...] = jnp.zeros_like(halo_scr)

    for r0 in range(0, MERGE_TM, MERGE_ROWS):
        rows = slice(r0, r0 + MERGE_ROWS)
        o_b = _pool_tile(pu_ref, pz_ref, pw_ref, ps_ref, halo_scr, rows, ti * MERGE_TM)
        merged = (ga_ref[rows, :].astype(F32) * _dot(oa_ref[rows, :], wb_ref[0])
                  + gb_ref[rows, :].astype(F32) * _dot(o_b, wb_ref[1])
                  + gc_ref[rows, :].astype(F32) * _dot(oc_ref[rows, :], wb_ref[2]))
        h = res_ref[rows, :] + _dot(merged.astype(BF16), wo_ref[...])
        if final:
            ms = jnp.mean(h * h, axis=-1, keepdims=True)
            h = h * lax.rsqrt(ms + EPS) * fg_ref[...]
        out_ref[rows, :] = h


def _merge(o_a, o_c, pm, pool_w_bf, pool_scale, wb_bf, wo_bf, h_res, final_g, seq, layer):
    n = h_res.shape[0]
    groups = len(POOL_WINDOWS)
    final = layer == DEPTH - 1
    row = lambda width, blk=0: pl.BlockSpec((MERGE_TM, width), lambda i: (i, blk))
    whole = lambda *shape: pl.BlockSpec((None,) + shape, lambda i: (layer,) + (0,) * len(shape))
    return pl.pallas_call(
        functools.partial(_merge_kernel, final=final, tiles_per_seq=seq // MERGE_TM),
        grid=(n // MERGE_TM,),
        in_specs=[
            row(BRANCH_WIDTH), row(BRANCH_WIDTH),
            row(BRANCH_WIDTH, 0), row(BRANCH_WIDTH, 1),
            row(D_MODEL, 1), row(D_MODEL, 2), row(D_MODEL, 3),
            whole(groups, LANES, LANES), whole(1, BRANCH_WIDTH),
            whole(3, BRANCH_WIDTH, D_MODEL), whole(D_MODEL, D_MODEL),
            row(D_MODEL), pl.BlockSpec((1, D_MODEL), lambda i: (0, 0)),
        ],
        out_specs=row(D_MODEL),
        out_shape=jax.ShapeDtypeStruct((n, D_MODEL), F32),
        scratch_shapes=[pltpu.VMEM((POOL_HALO, BRANCH_WIDTH), F32)],
        compiler_params=pltpu.CompilerParams(dimension_semantics=("arbitrary",),
                                             vmem_limit_bytes=VMEM_LIMIT_MERGE),
        name="merge_final" if final else "merge",
    )(o_a, o_c, pm, pm, pm, pm, pm, pool_w_bf,
      pool_scale.reshape(DEPTH, 1, BRANCH_WIDTH), wb_bf, wo_bf, h_res, final_g)


def kernel(x, norm_g, w_in, pool_w, pool_scale, hgrn_lb, hgrn_norm_g, w_branch, w_out, final_g):
    bsz, seq, _ = x.shape
    h_res = x.astype(F32).reshape(bsz * seq, D_MODEL)
    lb_all = jnp.cumsum(jax.nn.softmax(hgrn_lb.astype(F32), axis=0), axis=0)
    lb_all = lb_all - lb_all[:1]
    w_in_bf = w_in.astype(BF16)
    pool_w_bf = pool_w.astype(BF16)
    wb_bf = w_branch.astype(BF16)
    wo_bf = w_out.astype(BF16)
    fg = final_g.astype(F32).reshape(1, D_MODEL)
    for layer in range(DEPTH):
        pm, o_a, o_c = _mix(h_res, norm_g[layer].astype(F32).reshape(1, D_MODEL), w_in_bf, layer,
                            lb_all[layer], hgrn_norm_g[layer].astype(F32), seq)
        h_res = _merge(o_a, o_c, pm, pool_w_bf, pool_scale.astype(F32),
                       wb_bf, wo_bf, h_res, fg, seq, layer)
    return h_res.reshape(bsz, seq, D_MODEL).astype(x.dtype)
```

```python
import functools
import math

import jax
import jax.numpy as jnp
import numpy as np
from jax import lax
from jax.experimental import pallas as pl
from jax.experimental.pallas import tpu as pltpu

D_MODEL = 1024
DEPTH = 2
BRANCH_WIDTH = D_MODEL // 2
SB_HEAD_DIM = 64
POOL_WINDOWS = (2, 4, 8, 16)
POOL_HALO = 16
HGRN_HEADS = 4
EPS = 1e-6
IN_COLS = 10 * BRANCH_WIDTH + 3 * D_MODEL

LANES = 128
SUBLANES = 8
VMEM_LIMIT_MIX = 62 * 1024 * 1024
VMEM_LIMIT_MERGE = 48 * 1024 * 1024

COL_SB_Q, COL_SB_K, COL_SB_V, COL_SB_Z = 0, 512, 1024, 1536
COL_POOL, COL_HG, COL_GATE = 2048, 3072, 5120
PM_COLS = 2 * BRANCH_WIDTH + 3 * D_MODEL

MIX_TM, MIX_TN = 512, 256
MIX_TICKS_PER_CHUNK = 6
SB_T = 256
SB_DEAD_LOG2 = -152.0
SB_OFF_LOG2 = -1e30
SB_Z_CAP = 126.0
HG_T = 128
MERGE_TM = 512

F32 = jnp.float32
BF16 = jnp.bfloat16


def _dot(a, b):
    return jnp.dot(a, b, preferred_element_type=F32)


def _dot_nt(a, b):
    return lax.dot_general(a, b, (((1,), (1,)), ((), ())), preferred_element_type=F32)


def _dot_tn(a, b):
    return lax.dot_general(a, b, (((0,), (0,)), ((), ())), preferred_element_type=F32)


def _split3(x):
    hi = x.astype(BF16)
    r = x - hi.astype(F32)
    mid = r.astype(BF16)
    lo = (r - mid.astype(F32)).astype(BF16)
    return hi, mid, lo


def _silu(x):
    return x * jax.nn.sigmoid(x)


def _sb_stages(ktile, vtile, tick):
    t = SB_T
    lane = lax.broadcasted_iota(jnp.int32, (1, LANES), 1)
    rows = lax.broadcasted_iota(jnp.int32, (t, t), 0)
    cols = lax.broadcasted_iota(jnp.int32, (t, t), 1)
    causal = cols < rows
    neg_upper = jnp.where(rows > cols, -1.0, 0.0).astype(BF16)

    def chain(z, masked):
        nl = jnp.maximum(z, jnp.log2(1.0 + jnp.exp2(jnp.minimum(z, SB_Z_CAP))))
        lz = z - nl
        if masked:
            nl = jnp.where(causal, nl, 0.0)
        out = lz, _dot(nl.astype(BF16), neg_upper), jnp.sum(nl, axis=-1, keepdims=True)
        tick()
        return out

    def weights(lz, rem, c, masked):
        a = jnp.exp2(lz + rem if c is None else lz + rem + c)
        if masked:
            a = jnp.where(causal, a, 0.0)
        tick()
        return a.astype(BF16)

    def live(carry):
        return jnp.max(jnp.maximum(carry[1], carry[3])) > SB_DEAD_LOG2

    def split(q):
        return (jnp.where(lane < SB_HEAD_DIM, q, jnp.zeros_like(q)),
                jnp.where(lane < SB_HEAD_DIM, jnp.zeros_like(q), q))

    def near(i, qa, qb):
        k0, v0 = ktile(i), vtile(i)
        prev = jnp.maximum(i - 1, 0)
        k1, v1 = ktile(prev), vtile(prev)
        z0a, z0b = _dot_nt(qa, k0), _dot_nt(qb, k0)
        z1a, z1b = _dot_nt(qa, k1), _dot_nt(qb, k1)
        lz0a, rem0a, rs0a = chain(z0a, True)
        lz0b, rem0b, rs0b = chain(z0b, True)
        acc_a = _dot(weights(lz0a, rem0a, None, True), v0)
        lz1a, rem1a, rs1a = chain(z1a, False)
        acc_b = _dot(weights(lz0b, rem0b, None, True), v0)
        lz1b, rem1b, rs1b = chain(z1b, False)
        off = jnp.where(i > 0, 0.0, SB_OFF_LOG2)
        c1a = off - rs0a
        c1b = off - rs0b
        acc_a = acc_a + _dot(weights(lz1a, rem1a, c1a, False), v1)
        acc_b = acc_b + _dot(weights(lz1b, rem1b, c1b, False), v1)
        return acc_a, c1a - rs1a, acc_b, c1b - rs1b

    def far(i, qa, qb, carry, alive):
        def body(state):
            j, _, (acc_a, c_a, acc_b, c_b) = state
            k, v = ktile(j), vtile(j)
            lz_a, rem_a, rs_a = chain(_dot_nt(qa, k), False)
            lz_b, rem_b, rs_b = chain(_dot_nt(qb, k), False)
            acc_a = acc_a + _dot(weights(lz_a, rem_a, c_a, False), v)
            acc_b = acc_b + _dot(weights(lz_b, rem_b, c_b, False), v)
            new = (acc_a, c_a - rs_a, acc_b, c_b - rs_b)
            return j - 1, live(new), new

        state = lax.while_loop(lambda st: (st[0] >= 0) & st[1], body, (i - 2, alive, carry))
        carry = state[2]
        return jnp.where(lane < SB_HEAD_DIM, carry[0], carry[2])

    return split, near, live, far


def _pool_tile(u_ref, z_ref, w_ref, sc_ref, halo_scr, first_pos):
    tm = u_ref.shape[0]
    pos = first_pos + lax.broadcasted_iota(jnp.int32, (tm, 1), 0)
    outs = []
    for g, w in enumerate(POOL_WINDOWS):
        cols = slice(g * LANES, (g + 1) * LANES)
        u = u_ref[:, cols].astype(F32)
        s = jnp.concatenate([halo_scr[:, cols], u], axis=0)
        k = 1
        while k < w:
            s = s + pltpu.roll(s, k, axis=0)
            k *= 2
        cnt = jnp.minimum(pos + 1, w).astype(F32)
        d = s[POOL_HALO:, :] / cnt - u
        outs.append(_dot(d.astype(BF16), w_ref[g]))
        halo_scr[:, cols] = u[tm - POOL_HALO:, :]
    y = jnp.concatenate(outs, axis=1)
    return (y * sc_ref[...] * z_ref[...].astype(F32)).astype(BF16)


HG_LEVELS = HG_T.bit_length() - 1


def _hgrn_tables():
    r = np.arange(HG_T)
    tri = (r[None, :] <= r[:, None]).astype(np.float32)
    pair = np.zeros((HG_LEVELS, HG_T, HG_T), np.float32)
    sign = np.zeros((HG_LEVELS, HG_T, LANES), np.float32)
    for l in range(HG_LEVELS):
        m = 1 << l
        second = (r % (2 * m)) >= m
        same = (r[:, None] // (2 * m)) == (r[None, :] // (2 * m))
        pair[l] = same & second[:, None] & ~second[None, :]
        sign[l] = np.where(second, 1.0, -1.0)[:, None]
    return jnp.asarray(tri, BF16), jnp.asarray(pair), jnp.asarray(sign)


def _hgrn_edge(b, level):
    t = b.shape[0]
    m = 1 << level
    if m >= SUBLANES:
        blk = 2 * m
        e = b.reshape(t // blk, blk, LANES)[:, m - 1:m, :]
        return jnp.broadcast_to(e, (t // blk, blk, LANES)).reshape(t, LANES)
    b3 = b.reshape(t // SUBLANES, SUBLANES, LANES)
    if m == 4:
        e = jnp.broadcast_to(b3[:, 3:4, :], b3.shape)
    else:
        sub = lax.broadcasted_iota(jnp.int32, (1, SUBLANES, 1), 1)
        e = jnp.where(sub < 4, b3[:, 1:2, :], b3[:, 5:6, :])
    return e.reshape(t, LANES)


def _hgrn_stages(blk, rows, lb_ref, g_ref, tri_ref, pair_ref, sign_ref, o_ref, tick):
    t = HG_T

    def col(part, cols):
        return blk[rows, part * BRANCH_WIDTH + cols.start:part * BRANCH_WIDTH + cols.stop]

    def front(cols):
        lb = lb_ref[:, cols]
        sig = jax.nn.sigmoid(col(1, cols).astype(F32))
        f = lb + (1.0 - lb) * sig
        kk = (1.0 - lb) * (1.0 - sig)
        hi, mid, lo = _split3(jnp.log2(f))
        tri = tri_ref[...]
        b = _dot(tri, hi) + _dot(tri, mid) + _dot(tri, lo)
        tick()
        return f, kk, b

    def middle(cols, st, f, kk, b):
        q = col(0, cols).astype(F32)
        v_bf = col(2, cols)
        b_last = b[t - 1:t, :]
        o = _dot_nt((q * jnp.exp2(b)).astype(BF16), st.astype(BF16))
        kd = (kk * jnp.exp2(b_last - b)).astype(BF16)
        st = st * jnp.exp2(b_last) + _dot_tn(v_bf, kd)
        o = o + jnp.sum(q * kk, axis=-1, keepdims=True) * v_bf.astype(F32)
        amat = jnp.zeros((t, t), F32)
        for level in range(HG_LEVELS):
            if level == 0:
                dec = jnp.where(sign_ref[0] > 0.0, f, 1.0)
            else:
                dec = jnp.exp2((b - _hgrn_edge(b, level)) * sign_ref[level])
            p = _dot_nt((q * dec).astype(BF16), (kk * dec).astype(BF16))
            amat = amat + p * pair_ref[level]
            tick()
        return st, o, amat.astype(BF16), v_bf

    def back(cols, o, amat, v_bf):
        o = o + _dot(amat, v_bf)
        ms = jnp.mean(o * o, axis=-1, keepdims=True)
        o = o * lax.rsqrt(ms + EPS) * g_ref[:, cols]
        o_ref[rows, cols] = (o * _silu(col(3, cols).astype(F32))).astype(o_ref.dtype)
        tick()

    return front, middle, back


def _mix_kernel(x_ref, g_ref, w_ref, cs_ref, lb_ref, ng_ref, tri_ref, pair_ref, sign_ref,
                pm_ref, oa_ref, oc_ref, qz_scr, kv_scr, hg_scr, st_scr, *, blocks_per_seq):
    n = pl.program_id(0)
    slot = n % 2
    bps = blocks_per_seq

    @pl.when(n == 0)
    def _():
        qz_scr[...] = jnp.zeros_like(qz_scr)
        kv_scr[...] = jnp.zeros_like(kv_scr)
        hg_scr[...] = jnp.zeros_like(hg_scr)
        st_scr[...] = jnp.zeros_like(st_scr)

    lag_pos = (n + bps - 1) % bps
    lag_kv = kv_scr.at[((n + 2 * bps - 1) // bps) % 2]
    lag_qz = qz_scr.at[1 - slot]
    lag_hg = hg_scr.at[1 - slot]
    cur_qz = qz_scr.at[slot]
    cur_hg = hg_scr.at[slot]
    cur_kv = kv_scr.at[(n // bps) % 2]
    cur_rows = pl.ds(pl.multiple_of((n % bps) * MIX_TM, MIX_TM), MIX_TM)

    def step(with_projection, with_mixers):
        if with_projection:
            x = x_ref[...]
            ms = jnp.mean(x * x, axis=-1, keepdims=True)
            hn = (x * lax.rsqrt(ms + EPS) * g_ref[...]).astype(BF16)

        def project(c):
            lo, hi = c * MIX_TN, (c + 1) * MIX_TN
            y = _dot(hn, w_ref[:, lo:hi])
            if lo < BRANCH_WIDTH:
                y = y * cs_ref[:, lo:hi]
            elif lo >= COL_GATE:
                y = jax.nn.sigmoid(y)
            elif COL_POOL + BRANCH_WIDTH <= lo < COL_HG:
                y = _silu(y)
            y = y.astype(BF16)
            if lo < COL_SB_K:
                cur_qz[:, lo:hi] = y
            elif lo < COL_SB_Z:
                cur_kv[cur_rows, lo - COL_SB_K:hi - COL_SB_K] = y
            elif lo < COL_POOL:
                cur_qz[:, lo - COL_SB_Z + BRANCH_WIDTH:hi - COL_SB_Z + BRANCH_WIDTH] = y
            elif lo < COL_HG:
                pm_ref[:, lo - COL_POOL:hi - COL_POOL] = y
            elif lo < COL_GATE:
                cur_hg[:, lo - COL_HG:hi - COL_HG] = y
            else:
                pm_ref[:, lo - COL_GATE + 2 * BRANCH_WIDTH:hi - COL_GATE + 2 * BRANCH_WIDTH] = y

        chunks = iter(range(IN_COLS // MIX_TN if with_projection else 0))
        ticks = [0]

        def tick():
            ticks[0] += 1
            if ticks[0] % MIX_TICKS_PER_CHUNK == 0:
                c = next(chunks, None)
                if c is not None:
                    project(c)

        if not with_mixers:
            for c in chunks:
                project(c)
            return

        fresh = lag_pos == 0
        heads = [slice(h * LANES, (h + 1) * LANES) for h in range(HGRN_HEADS)]
        states = [jnp.where(fresh, 0.0, st_scr[h]) for h in range(HGRN_HEADS)]
        pending = []
        for k in range(MIX_TM // HG_T):
            rows = slice(k * HG_T, (k + 1) * HG_T)
            front, middle, back = _hgrn_stages(lag_hg, rows, lb_ref, ng_ref, tri_ref, pair_ref,
                                               sign_ref, oc_ref, tick)
            fronts = [front(c) for c in heads]
            mids = [middle(c, st, *fr) for c, st, fr in zip(heads, states, fronts)]
            states = [md[0] for md in mids]
            for c, md in zip(heads, mids):
                back(c, *md[1:])

            pc = slice(k * LANES, (k + 1) * LANES)
            split, near, live, far = _sb_stages(
                lambda j, pc=pc: lag_kv[pl.ds(pl.multiple_of(j * SB_T, SB_T), SB_T), pc],
                lambda j, pc=pc: lag_kv[pl.ds(pl.multiple_of(j * SB_T, SB_T), SB_T),
                                        BRANCH_WIDTH + pc.start:BRANCH_WIDTH + pc.stop],
                tick)
            for u in range(MIX_TM // SB_T):
                qr = slice(u * SB_T, (u + 1) * SB_T)
                qa, qb = split(lag_qz[qr, pc])
                i = lag_pos * (MIX_TM // SB_T) + u
                carry = near(i, qa, qb)
                gate = _silu(lag_qz[qr, BRANCH_WIDTH + pc.start:BRANCH_WIDTH + pc.stop].astype(F32))
                pending.append((far, i, qa, qb, carry, live(carry), gate, qr, pc))
        for c in chunks:
            project(c)
        for h in range(HGRN_HEADS):
            st_scr[h] = states[h]
        outs = [far(i, qa, qb, carry, alive) for far, i, qa, qb, carry, alive, _, _, _ in pending]
        for o, (_, _, _, _, _, _, gate, qr, pc) in zip(outs, pending):
            oa_ref[qr, pc] = (o * gate).astype(oa_ref.dtype)

    step(True, True)


def _mix(h_res, g, w_bf, layer, lb, norm_g, seq):
    n = h_res.shape[0]
    nb = n // MIX_TM
    col = lax.broadcasted_iota(jnp.int32, (1, IN_COLS), 1)
    col_scale = jnp.where(col < BRANCH_WIDTH, math.log2(math.e) / math.sqrt(SB_HEAD_DIM), 1.0)
    tri, pair, sign = _hgrn_tables()
    whole = lambda arr: pl.BlockSpec(arr.shape, lambda i: (0,) * arr.ndim,
                                     pipeline_mode=pl.Buffered(1))
    cur = lambda i: (jnp.minimum(i, nb - 1), 0)
    lagged = lambda i: (jnp.maximum(i - 1, 0), 0)
    return pl.pallas_call(
        functools.partial(_mix_kernel, blocks_per_seq=seq // MIX_TM),
        grid=(nb + 1,),
        in_specs=[
            pl.BlockSpec((MIX_TM, D_MODEL), cur),
            pl.BlockSpec((1, D_MODEL), lambda i: (0, 0)),
            pl.BlockSpec((None, D_MODEL, IN_COLS), lambda i: (layer, 0, 0),
                         pipeline_mode=pl.Buffered(1)),
            pl.BlockSpec((1, IN_COLS), lambda i: (0, 0)),
            pl.BlockSpec((1, BRANCH_WIDTH), lambda i: (0, 0)),
            pl.BlockSpec((1, BRANCH_WIDTH), lambda i: (0, 0)),
            whole(tri), whole(pair), whole(sign),
        ],
        out_specs=[
            pl.BlockSpec((MIX_TM, PM_COLS), cur),
            pl.BlockSpec((MIX_TM, BRANCH_WIDTH), lagged),
            pl.BlockSpec((MIX_TM, BRANCH_WIDTH), lagged),
        ],
        out_shape=[jax.ShapeDtypeStruct((n, PM_COLS), BF16),
                   jax.ShapeDtypeStruct((n, BRANCH_WIDTH), BF16),
                   jax.ShapeDtypeStruct((n, BRANCH_WIDTH), BF16)],
        scratch_shapes=[pltpu.VMEM((2, MIX_TM, 2 * BRANCH_WIDTH), BF16),
                        pltpu.VMEM((2, seq, 2 * BRANCH_WIDTH), BF16),
                        pltpu.VMEM((2, MIX_TM, 4 * BRANCH_WIDTH), BF16),
                        pltpu.VMEM((HGRN_HEADS, LANES, LANES), F32)],
        compiler_params=pltpu.CompilerParams(dimension_semantics=("arbitrary",),
                                             vmem_limit_bytes=VMEM_LIMIT_MIX),
        name="mix",
    )(h_res, g, w_bf, col_scale.astype(F32), lb.reshape(1, BRANCH_WIDTH),
      norm_g.reshape(1, BRANCH_WIDTH), tri, pair, sign)


def _merge_kernel(oa_ref, oc_ref, pu_ref, pz_ref, ga_ref, gb_ref, gc_ref, pw_ref, ps_ref,
                  wb_ref, wo_ref, res_ref, fg_ref, out_ref, halo_scr, *, final, tiles_per_seq):
    ti = pl.program_id(0) % tiles_per_seq

    @pl.when(ti == 0)
    def _():
        halo_scr[...] = jnp.zeros_like(halo_scr)

    o_b = _pool_tile(pu_ref, pz_ref, pw_ref, ps_ref, halo_scr, ti * MERGE_TM)
    merged = (ga_ref[...].astype(F32) * _dot(oa_ref[...], wb_ref[0])
              + gb_ref[...].astype(F32) * _dot(o_b, wb_ref[1])
              + gc_ref[...].astype(F32) * _dot(oc_ref[...], wb_ref[2]))
    h = res_ref[...] + _dot(merged.astype(BF16), wo_ref[...])
    if final:
        ms = jnp.mean(h * h, axis=-1, keepdims=True)
        h = h * lax.rsqrt(ms + EPS) * fg_ref[...]
    out_ref[...] = h


def _merge(o_a, o_c, pm, pool_w_bf, pool_scale, wb_bf, wo_bf, h_res, final_g, seq, layer):
    n = h_res.shape[0]
    groups = len(POOL_WINDOWS)
    final = layer == DEPTH - 1
    row = lambda width, blk=0: pl.BlockSpec((MERGE_TM, width), lambda i: (i, blk))
    whole = lambda *shape: pl.BlockSpec((None,) + shape, lambda i: (layer,) + (0,) * len(shape))
    return pl.pallas_call(
        functools.partial(_merge_kernel, final=final, tiles_per_seq=seq // MERGE_TM),
        grid=(n // MERGE_TM,),
        in_specs=[
            row(BRANCH_WIDTH), row(BRANCH_WIDTH),
            row(BRANCH_WIDTH, 0), row(BRANCH_WIDTH, 1),
            row(D_MODEL, 1), row(D_MODEL, 2), row(D_MODEL, 3),
            whole(groups, LANES, LANES), whole(1, BRANCH_WIDTH),
            whole(3, BRANCH_WIDTH, D_MODEL), whole(D_MODEL, D_MODEL),
            row(D_MODEL), pl.BlockSpec((1, D_MODEL), lambda i: (0, 0)),
        ],
        out_specs=row(D_MODEL),
        out_shape=jax.ShapeDtypeStruct((n, D_MODEL), F32),
        scratch_shapes=[pltpu.VMEM((POOL_HALO, BRANCH_WIDTH), F32)],
        compiler_params=pltpu.CompilerParams(dimension_semantics=("arbitrary",),
                                             vmem_limit_bytes=VMEM_LIMIT_MERGE),
        name="merge_final" if final else "merge",
    )(o_a, o_c, pm, pm, pm, pm, pm, pool_w_bf,
      pool_scale.reshape(DEPTH, 1, BRANCH_WIDTH), wb_bf, wo_bf, h_res, final_g)


def kernel(x, norm_g, w_in, pool_w, pool_scale, hgrn_lb, hgrn_norm_g, w_branch, w_out, final_g):
    bsz, seq, _ = x.shape
    h_res = x.astype(F32).reshape(bsz * seq, D_MODEL)
    lb_all = jnp.cumsum(jax.nn.softmax(hgrn_lb.astype(F32), axis=0), axis=0)
    lb_all = lb_all - lb_all[:1]
    w_in_bf = w_in.astype(BF16)
    pool_w_bf = pool_w.astype(BF16)
    wb_bf = w_branch.astype(BF16)
    wo_bf = w_out.astype(BF16)
    fg = final_g.astype(F32).reshape(1, D_MODEL)
    for layer in range(DEPTH):
        pm, o_a, o_c = _mix(h_res, norm_g[layer].astype(F32).reshape(1, D_MODEL), w_in_bf, layer,
                            lb_all[layer], hgrn_norm_g[layer].astype(F32), seq)
        h_res = _merge(o_a, o_c, pm, pool_w_bf, pool_scale.astype(F32),
                       wb_bf, wo_bf, h_res, fg, seq, layer)
    return h_res.reshape(bsz, seq, D_MODEL).astype(x.dtype)
```

```python
import functools
import math

import jax
import jax.numpy as jnp
import numpy as np
from jax import lax
from jax.experimental import pallas as pl
from jax.experimental.pallas import tpu as pltpu

D_MODEL = 1024
DEPTH = 2
BRANCH_WIDTH = D_MODEL // 2
SB_HEAD_DIM = 64
POOL_WINDOWS = (2, 4, 8, 16)
POOL_HALO = 16
HGRN_HEADS = 4
EPS = 1e-6
IN_COLS = 10 * BRANCH_WIDTH + 3 * D_MODEL

LANES = 128
SUBLANES = 8
VMEM_LIMIT_MIX = 62 * 1024 * 1024
VMEM_LIMIT_MERGE = 48 * 1024 * 1024

COL_SB_Q, COL_SB_K, COL_SB_V, COL_SB_Z = 0, 512, 1024, 1536
COL_POOL, COL_HG, COL_GATE = 2048, 3072, 5120
PM_COLS = 2 * BRANCH_WIDTH + 3 * D_MODEL

MIX_TM, MIX_TN = 512, 256
MIX_TICKS_PER_CHUNK = 6
SB_T = 256
SB_DEAD_LOG2 = -152.0
SB_OFF_LOG2 = -1e30
SB_Z_CAP = 126.0
HG_T = 128
MERGE_TM = 512
MERGE_ROWS = 512

F32 = jnp.float32
BF16 = jnp.bfloat16


def _dot(a, b):
    return jnp.dot(a, b, preferred_element_type=F32)


def _dot_nt(a, b):
    return lax.dot_general(a, b, (((1,), (1,)), ((), ())), preferred_element_type=F32)


def _dot_tn(a, b):
    return lax.dot_general(a, b, (((0,), (0,)), ((), ())), preferred_element_type=F32)


def _split3(x):
    hi = x.astype(BF16)
    r = x - hi.astype(F32)
    mid = r.astype(BF16)
    lo = (r - mid.astype(F32)).astype(BF16)
    return hi, mid, lo


def _silu(x):
    return x * jax.nn.sigmoid(x)


def _sb_stages(ktile, vtile, tick):
    t = SB_T
    lane = lax.broadcasted_iota(jnp.int32, (1, LANES), 1)
    rows = lax.broadcasted_iota(jnp.int32, (t, t), 0)
    cols = lax.broadcasted_iota(jnp.int32, (t, t), 1)
    causal = cols < rows
    neg_upper = jnp.where(rows > cols, -1.0, 0.0).astype(BF16)

    def chain(z, masked):
        nl = jnp.maximum(z, jnp.log2(1.0 + jnp.exp2(jnp.minimum(z, SB_Z_CAP))))
        lz = z - nl
        if masked:
            nl = jnp.where(causal, nl, 0.0)
        out = lz, _dot(nl.astype(BF16), neg_upper), jnp.sum(nl, axis=-1, keepdims=True)
        tick()
        return out

    def weights(lz, rem, c, masked):
        a = jnp.exp2(lz + rem if c is None else lz + rem + c)
        if masked:
            a = jnp.where(causal, a, 0.0)
        tick()
        return a.astype(BF16)

    def live(carry):
        return jnp.max(jnp.maximum(carry[1], carry[3])) > SB_DEAD_LOG2

    def split(q):
        return (jnp.where(lane < SB_HEAD_DIM, q, jnp.zeros_like(q)),
                jnp.where(lane < SB_HEAD_DIM, jnp.zeros_like(q), q))

    def near(i, qa, qb):
        k0, v0 = ktile(i), vtile(i)
        prev = jnp.maximum(i - 1, 0)
        k1, v1 = ktile(prev), vtile(prev)
        z0a, z0b = _dot_nt(qa, k0), _dot_nt(qb, k0)
        z1a, z1b = _dot_nt(qa, k1), _dot_nt(qb, k1)
        lz0a, rem0a, rs0a = chain(z0a, True)
        lz0b, rem0b, rs0b = chain(z0b, True)
        acc_a = _dot(weights(lz0a, rem0a, None, True), v0)
        lz1a, rem1a, rs1a = chain(z1a, False)
        acc_b = _dot(weights(lz0b, rem0b, None, True), v0)
        lz1b, rem1b, rs1b = chain(z1b, False)
        off = jnp.where(i > 0, 0.0, SB_OFF_LOG2)
        c1a = off - rs0a
        c1b = off - rs0b
        acc_a = acc_a + _dot(weights(lz1a, rem1a, c1a, False), v1)
        acc_b = acc_b + _dot(weights(lz1b, rem1b, c1b, False), v1)
        return acc_a, c1a - rs1a, acc_b, c1b - rs1b

    def far(i, qa, qb, carry, alive):
        def body(state):
            j, _, (acc_a, c_a, acc_b, c_b) = state
            k, v = ktile(j), vtile(j)
            lz_a, rem_a, rs_a = chain(_dot_nt(qa, k), False)
            lz_b, rem_b, rs_b = chain(_dot_nt(qb, k), False)
            acc_a = acc_a + _dot(weights(lz_a, rem_a, c_a, False), v)
            acc_b = acc_b + _dot(weights(lz_b, rem_b, c_b, False), v)
            new = (acc_a, c_a - rs_a, acc_b, c_b - rs_b)
            return j - 1, live(new), new

        state = lax.while_loop(lambda st: (st[0] >= 0) & st[1], body, (i - 2, alive, carry))
        carry = state[2]
        return jnp.where(lane < SB_HEAD_DIM, carry[0], carry[2])

    return split, near, live, far


def _pool_tile(u_ref, z_ref, w_ref, sc_ref, halo_scr, rows, first_pos):
    tm = rows.stop - rows.start
    pos = first_pos + rows.start + lax.broadcasted_iota(jnp.int32, (tm, 1), 0)
    outs = []
    for g, w in enumerate(POOL_WINDOWS):
        cols = slice(g * LANES, (g + 1) * LANES)
        u = u_ref[rows, cols].astype(F32)
        s = jnp.concatenate([halo_scr[:, cols], u], axis=0)
        k = 1
        while k < w:
            s = s + pltpu.roll(s, k, axis=0)
            k *= 2
        cnt = jnp.minimum(pos + 1, w).astype(F32)
        d = s[POOL_HALO:, :] / cnt - u
        outs.append(_dot(d.astype(BF16), w_ref[g]))
        halo_scr[:, cols] = u[tm - POOL_HALO:, :]
    y = jnp.concatenate(outs, axis=1)
    return (y * sc_ref[...] * z_ref[rows, :].astype(F32)).astype(BF16)


HG_LEVELS = HG_T.bit_length() - 1


def _hgrn_tables():
    r = np.arange(HG_T)
    tri = (r[None, :] <= r[:, None]).astype(np.float32)
    pair = np.zeros((HG_LEVELS, HG_T, HG_T), np.float32)
    sign = np.zeros((HG_LEVELS, HG_T, LANES), np.float32)
    for l in range(HG_LEVELS):
        m = 1 << l
        second = (r % (2 * m)) >= m
        same = (r[:, None] // (2 * m)) == (r[None, :] // (2 * m))
        pair[l] = same & second[:, None] & ~second[None, :]
        sign[l] = np.where(second, 1.0, -1.0)[:, None]
    return jnp.asarray(tri, BF16), jnp.asarray(pair), jnp.asarray(sign)


def _hgrn_edge(b, level):
    t = b.shape[0]
    m = 1 << level
    if m >= SUBLANES:
        blk = 2 * m
        e = b.reshape(t // blk, blk, LANES)[:, m - 1:m, :]
        return jnp.broadcast_to(e, (t // blk, blk, LANES)).reshape(t, LANES)
    b3 = b.reshape(t // SUBLANES, SUBLANES, LANES)
    if m == 4:
        e = jnp.broadcast_to(b3[:, 3:4, :], b3.shape)
    else:
        sub = lax.broadcasted_iota(jnp.int32, (1, SUBLANES, 1), 1)
        e = jnp.where(sub < 4, b3[:, 1:2, :], b3[:, 5:6, :])
    return e.reshape(t, LANES)


def _hgrn_stages(blk, rows, lb_ref, g_ref, tri_ref, pair_ref, sign_ref, o_ref, tick):
    t = HG_T

    def col(part, cols):
        return blk[rows, part * BRANCH_WIDTH + cols.start:part * BRANCH_WIDTH + cols.stop]

    def front(cols):
        lb = lb_ref[:, cols]
        sig = jax.nn.sigmoid(col(1, cols).astype(F32))
        f = lb + (1.0 - lb) * sig
        kk = (1.0 - lb) * (1.0 - sig)
        hi, mid, lo = _split3(jnp.log2(f))
        tri = tri_ref[...]
        b = _dot(tri, hi) + _dot(tri, mid) + _dot(tri, lo)
        tick()
        return f, kk, b

    def middle(cols, st, f, kk, b):
        q = col(0, cols).astype(F32)
        v_bf = col(2, cols)
        b_last = b[t - 1:t, :]
        o = _dot_nt((q * jnp.exp2(b)).astype(BF16), st.astype(BF16))
        kd = (kk * jnp.exp2(b_last - b)).astype(BF16)
        st = st * jnp.exp2(b_last) + _dot_tn(v_bf, kd)
        o = o + jnp.sum(q * kk, axis=-1, keepdims=True) * v_bf.astype(F32)
        amat = jnp.zeros((t, t), F32)
        for level in range(HG_LEVELS):
            if level == 0:
                dec = jnp.where(sign_ref[0] > 0.0, f, 1.0)
            else:
                dec = jnp.exp2((b - _hgrn_edge(b, level)) * sign_ref[level])
            p = _dot_nt((q * dec).astype(BF16), (kk * dec).astype(BF16))
            amat = amat + p * pair_ref[level]
            tick()
        return st, o, amat.astype(BF16), v_bf

    def back(cols, o, amat, v_bf):
        o = o + _dot(amat, v_bf)
        ms = jnp.mean(o * o, axis=-1, keepdims=True)
        o = o * lax.rsqrt(ms + EPS) * g_ref[:, cols]
        o_ref[rows, cols] = (o * _silu(col(3, cols).astype(F32))).astype(o_ref.dtype)
        tick()

    return front, middle, back


def _mix_kernel(x_ref, g_ref, w_ref, cs_ref, lb_ref, ng_ref, tri_ref, pair_ref, sign_ref,
                pm_ref, oa_ref, oc_ref, qz_scr, kv_scr, hg_scr, st_scr, *, blocks_per_seq):
    n = pl.program_id(0)
    slot = n % 2
    bps = blocks_per_seq

    @pl.when(n == 0)
    def _():
        qz_scr[...] = jnp.zeros_like(qz_scr)
        kv_scr[...] = jnp.zeros_like(kv_scr)
        hg_scr[...] = jnp.zeros_like(hg_scr)
        st_scr[...] = jnp.zeros_like(st_scr)

    lag_pos = (n + bps - 1) % bps
    lag_kv = kv_scr.at[((n + 2 * bps - 1) // bps) % 2]
    lag_qz = qz_scr.at[1 - slot]
    lag_hg = hg_scr.at[1 - slot]
    cur_qz = qz_scr.at[slot]
    cur_hg = hg_scr.at[slot]
    cur_kv = kv_scr.at[(n // bps) % 2]
    cur_rows = pl.ds(pl.multiple_of((n % bps) * MIX_TM, MIX_TM), MIX_TM)

    def step(with_projection, with_mixers):
        if with_projection:
            x = x_ref[...]
            ms = jnp.mean(x * x, axis=-1, keepdims=True)
            hn = (x * lax.rsqrt(ms + EPS) * g_ref[...]).astype(BF16)

        def project(c):
            lo, hi = c * MIX_TN, (c + 1) * MIX_TN
            y = _dot(hn, w_ref[:, lo:hi])
            if lo < BRANCH_WIDTH:
                y = y * cs_ref[:, lo:hi]
            elif lo >= COL_GATE:
                y = jax.nn.sigmoid(y)
            elif COL_POOL + BRANCH_WIDTH <= lo < COL_HG:
                y = _silu(y)
            y = y.astype(BF16)
            if lo < COL_SB_K:
                cur_qz[:, lo:hi] = y
            elif lo < COL_SB_Z:
                cur_kv[cur_rows, lo - COL_SB_K:hi - COL_SB_K] = y
            elif lo < COL_POOL:
                cur_qz[:, lo - COL_SB_Z + BRANCH_WIDTH:hi - COL_SB_Z + BRANCH_WIDTH] = y
            elif lo < COL_HG:
                pm_ref[:, lo - COL_POOL:hi - COL_POOL] = y
            elif lo < COL_GATE:
                cur_hg[:, lo - COL_HG:hi - COL_HG] = y
            else:
                pm_ref[:, lo - COL_GATE + 2 * BRANCH_WIDTH:hi - COL_GATE + 2 * BRANCH_WIDTH] = y

        chunks = iter(range(IN_COLS // MIX_TN if with_projection else 0))
        ticks = [0]

        def tick():
            ticks[0] += 1
            if ticks[0] % MIX_TICKS_PER_CHUNK == 0:
                c = next(chunks, None)
                if c is not None:
                    project(c)

        if not with_mixers:
            for c in chunks:
                project(c)
            return

        fresh = lag_pos == 0
        heads = [slice(h * LANES, (h + 1) * LANES) for h in range(HGRN_HEADS)]
        states = [jnp.where(fresh, 0.0, st_scr[h]) for h in range(HGRN_HEADS)]
        pending = []
        for k in range(MIX_TM // HG_T):
            rows = slice(k * HG_T, (k + 1) * HG_T)
            front, middle, back = _hgrn_stages(lag_hg, rows, lb_ref, ng_ref, tri_ref, pair_ref,
                                               sign_ref, oc_ref, tick)
            fronts = [front(c) for c in heads]
            mids = [middle(c, st, *fr) for c, st, fr in zip(heads, states, fronts)]
            states = [md[0] for md in mids]
            for c, md in zip(heads, mids):
                back(c, *md[1:])

            pc = slice(k * LANES, (k + 1) * LANES)
            split, near, live, far = _sb_stages(
                lambda j, pc=pc: lag_kv[pl.ds(pl.multiple_of(j * SB_T, SB_T), SB_T), pc],
                lambda j, pc=pc: lag_kv[pl.ds(pl.multiple_of(j * SB_T, SB_T), SB_T),
                                        BRANCH_WIDTH + pc.start:BRANCH_WIDTH + pc.stop],
                tick)
            for u in range(MIX_TM // SB_T):
                qr = slice(u * SB_T, (u + 1) * SB_T)
                qa, qb = split(lag_qz[qr, pc])
                i = lag_pos * (MIX_TM // SB_T) + u
                carry = near(i, qa, qb)
                gate = _silu(lag_qz[qr, BRANCH_WIDTH + pc.start:BRANCH_WIDTH + pc.stop].astype(F32))
                pending.append((far, i, qa, qb, carry, live(carry), gate, qr, pc))
        for c in chunks:
            project(c)
        for h in range(HGRN_HEADS):
            st_scr[h] = states[h]
        outs = [far(i, qa, qb, carry, alive) for far, i, qa, qb, carry, alive, _, _, _ in pending]
        for o, (_, _, _, _, _, _, gate, qr, pc) in zip(outs, pending):
            oa_ref[qr, pc] = (o * gate).astype(oa_ref.dtype)

    step(True, True)


def _mix(h_res, g, w_bf, layer, lb, norm_g, seq):
    n = h_res.shape[0]
    nb = n // MIX_TM
    col = lax.broadcasted_iota(jnp.int32, (1, IN_COLS), 1)
    col_scale = jnp.where(col < BRANCH_WIDTH, math.log2(math.e) / math.sqrt(SB_HEAD_DIM), 1.0)
    tri, pair, sign = _hgrn_tables()
    whole = lambda arr: pl.BlockSpec(arr.shape, lambda i: (0,) * arr.ndim,
                                     pipeline_mode=pl.Buffered(1))
    cur = lambda i: (jnp.minimum(i, nb - 1), 0)
    lagged = lambda i: (jnp.maximum(i - 1, 0), 0)
    return pl.pallas_call(
        functools.partial(_mix_kernel, blocks_per_seq=seq // MIX_TM),
        grid=(nb + 1,),
        in_specs=[
            pl.BlockSpec((MIX_TM, D_MODEL), cur),
            pl.BlockSpec((1, D_MODEL), lambda i: (0, 0)),
            pl.BlockSpec((None, D_MODEL, IN_COLS), lambda i: (layer, 0, 0),
                         pipeline_mode=pl.Buffered(1)),
            pl.BlockSpec((1, IN_COLS), lambda i: (0, 0)),
            pl.BlockSpec((1, BRANCH_WIDTH), lambda i: (0, 0)),
            pl.BlockSpec((1, BRANCH_WIDTH), lambda i: (0, 0)),
            whole(tri), whole(pair), whole(sign),
        ],
        out_specs=[
            pl.BlockSpec((MIX_TM, PM_COLS), cur),
            pl.BlockSpec((MIX_TM, BRANCH_WIDTH), lagged),
            pl.BlockSpec((MIX_TM, BRANCH_WIDTH), lagged),
        ],
        out_shape=[jax.ShapeDtypeStruct((n, PM_COLS), BF16),
                   jax.ShapeDtypeStruct((n, BRANCH_WIDTH), BF16),
                   jax.ShapeDtypeStruct((n, BRANCH_WIDTH), BF16)],
        scratch_shapes=[pltpu.VMEM((2, MIX_TM, 2 * BRANCH_WIDTH), BF16),
                        pltpu.VMEM((2, seq, 2 * BRANCH_WIDTH), BF16),
                        pltpu.VMEM((2, MIX_TM, 4 * BRANCH_WIDTH), BF16),
                        pltpu.VMEM((HGRN_HEADS, LANES, LANES), F32)],
        compiler_params=pltpu.CompilerParams(dimension_semantics=("arbitrary",),
                                             vmem_limit_bytes=VMEM_LIMIT_MIX),
        name="mix",
    )(h_res, g, w_bf, col_scale.astype(F32), lb.reshape(1, BRANCH_WIDTH),
      norm_g.reshape(1, BRANCH_WIDTH), tri, pair, sign)


def _merge_kernel(oa_ref, oc_ref, pu_ref, pz_ref, ga_ref, gb_ref, gc_ref, pw32_ref, ps_ref,
                  wb32_ref, wo32_ref, res_ref, fg_ref, out_ref, halo_scr, pw_ref, wb_ref, wo_ref, *,
                  final, tiles_per_seq):
    ti = pl.program_id(0) % tiles_per_seq

    @pl.when(pl.program_id(0) == 0)
    def _():
        pw_ref[...] = pw32_ref[...].astype(BF16)
        wb_ref[...] = wb32_ref[...].astype(BF16)
        wo_ref[...] = wo32_ref[...].astype(BF16)

    @pl.when(ti == 0)
    def _():
        halo_scr[...] = jnp.zeros_like(halo_scr)

    for r0 in range(0, MERGE_TM, MERGE_ROWS):
        rows = slice(r0, r0 + MERGE_ROWS)
        o_b = _pool_tile(pu_ref, pz_ref, pw_ref, ps_ref, halo_scr, rows, ti * MERGE_TM)
        merged = (ga_ref[rows, :].astype(F32) * _dot(oa_ref[rows, :], wb_ref[0])
                  + gb_ref[rows, :].astype(F32) * _dot(o_b, wb_ref[1])
                  + gc_ref[rows, :].astype(F32) * _dot(oc_ref[rows, :], wb_ref[2]))
        h = res_ref[rows, :] + _dot(merged.astype(BF16), wo_ref[...])
        if final:
            ms = jnp.mean(h * h, axis=-1, keepdims=True)
            h = h * lax.rsqrt(ms + EPS) * fg_ref[...]
        out_ref[rows, :] = h


def _merge(o_a, o_c, pm, pool_w_bf, pool_scale, wb_bf, wo_bf, h_res, final_g, seq, layer):
    n = h_res.shape[0]
    groups = len(POOL_WINDOWS)
    final = layer == DEPTH - 1
    row = lambda width, blk=0: pl.BlockSpec((MERGE_TM, width), lambda i: (i, blk))
    whole = lambda *shape: pl.BlockSpec((None,) + shape, lambda i: (layer,) + (0,) * len(shape),
                                        pipeline_mode=pl.Buffered(1))
    return pl.pallas_call(
        functools.partial(_merge_kernel, final=final, tiles_per_seq=seq // MERGE_TM),
        grid=(n // MERGE_TM,),
        in_specs=[
            row(BRANCH_WIDTH), row(BRANCH_WIDTH),
            row(BRANCH_WIDTH, 0), row(BRANCH_WIDTH, 1),
            row(D_MODEL, 1), row(D_MODEL, 2), row(D_MODEL, 3),
            whole(groups, LANES, LANES), whole(1, BRANCH_WIDTH),
            whole(3, BRANCH_WIDTH, D_MODEL), whole(D_MODEL, D_MODEL),
            row(D_MODEL), pl.BlockSpec((1, D_MODEL), lambda i: (0, 0)),
        ],
        out_specs=row(D_MODEL),
        out_shape=jax.ShapeDtypeStruct((n, D_MODEL), F32),
        scratch_shapes=[pltpu.VMEM((POOL_HALO, BRANCH_WIDTH), F32),
                        pltpu.VMEM((groups, LANES, LANES), BF16),
                        pltpu.VMEM((3, BRANCH_WIDTH, D_MODEL), BF16),
                        pltpu.VMEM((D_MODEL, D_MODEL), BF16)],
        compiler_params=pltpu.CompilerParams(dimension_semantics=("arbitrary",),
                                             vmem_limit_bytes=VMEM_LIMIT_MERGE),
        name="merge_final" if final else "merge",
    )(o_a, o_c, pm, pm, pm, pm, pm, pool_w_bf,
      pool_scale.reshape(DEPTH, 1, BRANCH_WIDTH), wb_bf, wo_bf, h_res, final_g)


def kernel(x, norm_g, w_in, pool_w, pool_scale, hgrn_lb, hgrn_norm_g, w_branch, w_out, final_g):
    bsz, seq, _ = x.shape
    h_res = x.astype(F32).reshape(bsz * seq, D_MODEL)
    lb_all = jnp.cumsum(jax.nn.softmax(hgrn_lb.astype(F32), axis=0), axis=0)
    lb_all = lb_all - lb_all[:1]
    w_in_bf = w_in.astype(BF16)
    pool_w32, wb32, wo32 = pool_w.astype(F32), w_branch.astype(F32), w_out.astype(F32)
    fg = final_g.astype(F32).reshape(1, D_MODEL)
    for layer in range(DEPTH):
        pm, o_a, o_c = _mix(h_res, norm_g[layer].astype(F32).reshape(1, D_MODEL), w_in_bf, layer,
                            lb_all[layer], hgrn_norm_g[layer].astype(F32), seq)
        h_res = _merge(o_a, o_c, pm, pool_w32, pool_scale.astype(F32),
                       wb32, wo32, h_res, fg, seq, layer)
    return h_res.reshape(bsz, seq, D_MODEL).astype(x.dtype)
```

```python
import functools
import math

import jax
import jax.numpy as jnp
import numpy as np
from jax import lax
from jax.experimental import pallas as pl
from jax.experimental.pallas import tpu as pltpu

D_MODEL = 1024
DEPTH = 2
BRANCH_WIDTH = D_MODEL // 2
SB_HEAD_DIM = 64
POOL_WINDOWS = (2, 4, 8, 16)
POOL_HALO = 16
HGRN_HEADS = 4
EPS = 1e-6
IN_COLS = 10 * BRANCH_WIDTH + 3 * D_MODEL

LANES = 128
SUBLANES = 8
VMEM_LIMIT_MIX = 62 * 1024 * 1024
VMEM_LIMIT_MERGE = 48 * 1024 * 1024

COL_SB_Q, COL_SB_K, COL_SB_V, COL_SB_Z = 0, 512, 1024, 1536
COL_POOL, COL_HG, COL_GATE = 2048, 3072, 5120
PM_COLS = 2 * BRANCH_WIDTH + 3 * D_MODEL

MIX_TM, MIX_TN = 512, 256
MIX_TICKS_PER_CHUNK = 6
SB_T = 256
SB_DEAD_LOG2 = -152.0
SB_OFF_LOG2 = -1e30
SB_Z_CAP = 126.0
HG_T = 64
MERGE_TM = 512
MERGE_ROWS = 512

F32 = jnp.float32
BF16 = jnp.bfloat16


def _dot(a, b):
    return jnp.dot(a, b, preferred_element_type=F32)


def _dot_nt(a, b):
    return lax.dot_general(a, b, (((1,), (1,)), ((), ())), preferred_element_type=F32)


def _dot_tn(a, b):
    return lax.dot_general(a, b, (((0,), (0,)), ((), ())), preferred_element_type=F32)


def _split3(x):
    hi = x.astype(BF16)
    r = x - hi.astype(F32)
    mid = r.astype(BF16)
    lo = (r - mid.astype(F32)).astype(BF16)
    return hi, mid, lo


def _silu(x):
    return x * jax.nn.sigmoid(x)


def _sb_stages(ktile, vtile, tick):
    t = SB_T
    lane = lax.broadcasted_iota(jnp.int32, (1, LANES), 1)
    rows = lax.broadcasted_iota(jnp.int32, (t, t), 0)
    cols = lax.broadcasted_iota(jnp.int32, (t, t), 1)
    causal = cols < rows
    neg_upper = jnp.where(rows > cols, -1.0, 0.0).astype(BF16)

    def chain(z, masked):
        nl = jnp.maximum(z, jnp.log2(1.0 + jnp.exp2(jnp.minimum(z, SB_Z_CAP))))
        lz = z - nl
        if masked:
            nl = jnp.where(causal, nl, 0.0)
        out = lz, _dot(nl.astype(BF16), neg_upper), jnp.sum(nl, axis=-1, keepdims=True)
        tick()
        return out

    def weights(lz, rem, c, masked):
        a = jnp.exp2(lz + rem if c is None else lz + rem + c)
        if masked:
            a = jnp.where(causal, a, 0.0)
        tick()
        return a.astype(BF16)

    def live(carry):
        return jnp.max(jnp.maximum(carry[1], carry[3])) > SB_DEAD_LOG2

    def split(q):
        return (jnp.where(lane < SB_HEAD_DIM, q, jnp.zeros_like(q)),
                jnp.where(lane < SB_HEAD_DIM, jnp.zeros_like(q), q))

    def near(i, qa, qb):
        k0, v0 = ktile(i), vtile(i)
        prev = jnp.maximum(i - 1, 0)
        k1, v1 = ktile(prev), vtile(prev)
        z0a, z0b = _dot_nt(qa, k0), _dot_nt(qb, k0)
        z1a, z1b = _dot_nt(qa, k1), _dot_nt(qb, k1)
        lz0a, rem0a, rs0a = chain(z0a, True)
        lz0b, rem0b, rs0b = chain(z0b, True)
        acc_a = _dot(weights(lz0a, rem0a, None, True), v0)
        lz1a, rem1a, rs1a = chain(z1a, False)
        acc_b = _dot(weights(lz0b, rem0b, None, True), v0)
        lz1b, rem1b, rs1b = chain(z1b, False)
        off = jnp.where(i > 0, 0.0, SB_OFF_LOG2)
        c1a = off - rs0a
        c1b = off - rs0b
        acc_a = acc_a + _dot(weights(lz1a, rem1a, c1a, False), v1)
        acc_b = acc_b + _dot(weights(lz1b, rem1b, c1b, False), v1)
        return acc_a, c1a - rs1a, acc_b, c1b - rs1b

    def far(i, qa, qb, carry, alive):
        def body(state):
            j, _, (acc_a, c_a, acc_b, c_b) = state
            k, v = ktile(j), vtile(j)
            lz_a, rem_a, rs_a = chain(_dot_nt(qa, k), False)
            lz_b, rem_b, rs_b = chain(_dot_nt(qb, k), False)
            acc_a = acc_a + _dot(weights(lz_a, rem_a, c_a, False), v)
            acc_b = acc_b + _dot(weights(lz_b, rem_b, c_b, False), v)
            new = (acc_a, c_a - rs_a, acc_b, c_b - rs_b)
            return j - 1, live(new), new

        state = lax.while_loop(lambda st: (st[0] >= 0) & st[1], body, (i - 2, alive, carry))
        carry = state[2]
        return jnp.where(lane < SB_HEAD_DIM, carry[0], carry[2])

    return split, near, live, far


def _pool_tile(u_ref, z_ref, w_ref, sc_ref, halo_scr, rows, first_pos):
    tm = rows.stop - rows.start
    pos = first_pos + rows.start + lax.broadcasted_iota(jnp.int32, (tm, 1), 0)
    outs = []
    for g, w in enumerate(POOL_WINDOWS):
        cols = slice(g * LANES, (g + 1) * LANES)
        u = u_ref[rows, cols].astype(F32)
        s = jnp.concatenate([halo_scr[:, cols], u], axis=0)
        k = 1
        while k < w:
            s = s + pltpu.roll(s, k, axis=0)
            k *= 2
        cnt = jnp.minimum(pos + 1, w).astype(F32)
        d = s[POOL_HALO:, :] / cnt - u
        outs.append(_dot(d.astype(BF16), w_ref[g]))
        halo_scr[:, cols] = u[tm - POOL_HALO:, :]
    y = jnp.concatenate(outs, axis=1)
    return (y * sc_ref[...] * z_ref[rows, :].astype(F32)).astype(BF16)


HG_LEVELS = HG_T.bit_length() - 1


def _hgrn_tables():
    r = np.arange(HG_T)
    tri = (r[None, :] <= r[:, None]).astype(np.float32)
    pair = np.zeros((HG_LEVELS, HG_T, HG_T), np.float32)
    sign = np.zeros((HG_LEVELS, HG_T, LANES), np.float32)
    for l in range(HG_LEVELS):
        m = 1 << l
        second = (r % (2 * m)) >= m
        same = (r[:, None] // (2 * m)) == (r[None, :] // (2 * m))
        pair[l] = same & second[:, None] & ~second[None, :]
        sign[l] = np.where(second, 1.0, -1.0)[:, None]
    return jnp.asarray(tri, BF16), jnp.asarray(pair), jnp.asarray(sign)


def _hgrn_edge(b, level):
    t = b.shape[0]
    m = 1 << level
    if m >= SUBLANES:
        blk = 2 * m
        e = b.reshape(t // blk, blk, LANES)[:, m - 1:m, :]
        return jnp.broadcast_to(e, (t // blk, blk, LANES)).reshape(t, LANES)
    b3 = b.reshape(t // SUBLANES, SUBLANES, LANES)
    if m == 4:
        e = jnp.broadcast_to(b3[:, 3:4, :], b3.shape)
    else:
        sub = lax.broadcasted_iota(jnp.int32, (1, SUBLANES, 1), 1)
        e = jnp.where(sub < 4, b3[:, 1:2, :], b3[:, 5:6, :])
    return e.reshape(t, LANES)


def _hgrn_stages(blk, rows, lb_ref, g_ref, tri_ref, pair_ref, sign_ref, o_ref, tick):
    t = HG_T

    def col(part, cols):
        return blk[rows, part * BRANCH_WIDTH + cols.start:part * BRANCH_WIDTH + cols.stop]

    def front(cols):
        lb = lb_ref[:, cols]
        sig = jax.nn.sigmoid(col(1, cols).astype(F32))
        f = lb + (1.0 - lb) * sig
        kk = (1.0 - lb) * (1.0 - sig)
        hi, mid, lo = _split3(jnp.log2(f))
        tri = tri_ref[...]
        b = _dot(tri, hi) + _dot(tri, mid) + _dot(tri, lo)
        tick()
        return f, kk, b

    def middle(cols, st, f, kk, b):
        q = col(0, cols).astype(F32)
        v_bf = col(2, cols)
        b_last = b[t - 1:t, :]
        o = _dot_nt((q * jnp.exp2(b)).astype(BF16), st.astype(BF16))
        kd = (kk * jnp.exp2(b_last - b)).astype(BF16)
        st = st * jnp.exp2(b_last) + _dot_tn(v_bf, kd)
        o = o + jnp.sum(q * kk, axis=-1, keepdims=True) * v_bf.astype(F32)
        amat = jnp.zeros((t, t), F32)
        for level in range(HG_LEVELS):
            if level == 0:
                dec = jnp.where(sign_ref[0] > 0.0, f, 1.0)
            else:
                dec = jnp.exp2((b - _hgrn_edge(b, level)) * sign_ref[level])
            p = _dot_nt((q * dec).astype(BF16), (kk * dec).astype(BF16))
            amat = amat + p * pair_ref[level]
            tick()
        return st, o, amat.astype(BF16), v_bf

    def back(cols, o, amat, v_bf):
        o = o + _dot(amat, v_bf)
        ms = jnp.mean(o * o, axis=-1, keepdims=True)
        o = o * lax.rsqrt(ms + EPS) * g_ref[:, cols]
        o_ref[rows, cols] = (o * _silu(col(3, cols).astype(F32))).astype(o_ref.dtype)
        tick()

    return front, middle, back


def _mix_kernel(x_ref, g_ref, w_ref, cs_ref, lb_ref, ng_ref, tri_ref, pair_ref, sign_ref,
                pm_ref, oa_ref, oc_ref, qz_scr, kv_scr, hg_scr, st_scr, *, blocks_per_seq):
    n = pl.program_id(0)
    slot = n % 2
    bps = blocks_per_seq

    @pl.when(n == 0)
    def _():
        qz_scr[...] = jnp.zeros_like(qz_scr)
        kv_scr[...] = jnp.zeros_like(kv_scr)
        hg_scr[...] = jnp.zeros_like(hg_scr)
        st_scr[...] = jnp.zeros_like(st_scr)

    lag_pos = (n + bps - 1) % bps
    lag_kv = kv_scr.at[((n + 2 * bps - 1) // bps) % 2]
    lag_qz = qz_scr.at[1 - slot]
    lag_hg = hg_scr.at[1 - slot]
    cur_qz = qz_scr.at[slot]
    cur_hg = hg_scr.at[slot]
    cur_kv = kv_scr.at[(n // bps) % 2]
    cur_rows = pl.ds(pl.multiple_of((n % bps) * MIX_TM, MIX_TM), MIX_TM)

    def step(with_projection, with_mixers):
        if with_projection:
            x = x_ref[...]
            ms = jnp.mean(x * x, axis=-1, keepdims=True)
            hn = (x * lax.rsqrt(ms + EPS) * g_ref[...]).astype(BF16)

        def project(c):
            lo, hi = c * MIX_TN, (c + 1) * MIX_TN
            y = _dot(hn, w_ref[:, lo:hi])
            if lo < BRANCH_WIDTH:
                y = y * cs_ref[:, lo:hi]
            elif lo >= COL_GATE:
                y = jax.nn.sigmoid(y)
            elif COL_POOL + BRANCH_WIDTH <= lo < COL_HG:
                y = _silu(y)
            y = y.astype(BF16)
            if lo < COL_SB_K:
                cur_qz[:, lo:hi] = y
            elif lo < COL_SB_Z:
                cur_kv[cur_rows, lo - COL_SB_K:hi - COL_SB_K] = y
            elif lo < COL_POOL:
                cur_qz[:, lo - COL_SB_Z + BRANCH_WIDTH:hi - COL_SB_Z + BRANCH_WIDTH] = y
            elif lo < COL_HG:
                pm_ref[:, lo - COL_POOL:hi - COL_POOL] = y
            elif lo < COL_GATE:
                cur_hg[:, lo - COL_HG:hi - COL_HG] = y
            else:
                pm_ref[:, lo - COL_GATE + 2 * BRANCH_WIDTH:hi - COL_GATE + 2 * BRANCH_WIDTH] = y

        chunks = iter(range(IN_COLS // MIX_TN if with_projection else 0))
        ticks = [0]

        def tick():
            ticks[0] += 1
            if ticks[0] % MIX_TICKS_PER_CHUNK == 0:
                c = next(chunks, None)
                if c is not None:
                    project(c)

        if not with_mixers:
            for c in chunks:
                project(c)
            return

        fresh = lag_pos == 0
        heads = [slice(h * LANES, (h + 1) * LANES) for h in range(HGRN_HEADS)]
        states = [jnp.where(fresh, 0.0, st_scr[h]) for h in range(HGRN_HEADS)]
        pending = []
        n_tiles = MIX_TM // HG_T
        n_pairs = BRANCH_WIDTH // LANES
        for k in range(n_tiles):
            rows = slice(k * HG_T, (k + 1) * HG_T)
            front, middle, back = _hgrn_stages(lag_hg, rows, lb_ref, ng_ref, tri_ref, pair_ref,
                                               sign_ref, oc_ref, tick)
            fronts = [front(c) for c in heads]
            mids = [middle(c, st, *fr) for c, st, fr in zip(heads, states, fronts)]
            states = [md[0] for md in mids]
            for c, md in zip(heads, mids):
                back(c, *md[1:])

            for p in range(k * n_pairs // n_tiles, (k + 1) * n_pairs // n_tiles):
                pc = slice(p * LANES, (p + 1) * LANES)
                split, near, live, far = _sb_stages(
                    lambda j, pc=pc: lag_kv[pl.ds(pl.multiple_of(j * SB_T, SB_T), SB_T), pc],
                    lambda j, pc=pc: lag_kv[pl.ds(pl.multiple_of(j * SB_T, SB_T), SB_T),
                                            BRANCH_WIDTH + pc.start:BRANCH_WIDTH + pc.stop],
                    tick)
                for u in range(MIX_TM // SB_T):
                    qr = slice(u * SB_T, (u + 1) * SB_T)
                    qa, qb = split(lag_qz[qr, pc])
                    i = lag_pos * (MIX_TM // SB_T) + u
                    carry = near(i, qa, qb)
                    gate = _silu(
                        lag_qz[qr, BRANCH_WIDTH + pc.start:BRANCH_WIDTH + pc.stop].astype(F32))
                    pending.append((far, i, qa, qb, carry, live(carry), gate, qr, pc))
        for c in chunks:
            project(c)
        for h in range(HGRN_HEADS):
            st_scr[h] = states[h]
        outs = [far(i, qa, qb, carry, alive) for far, i, qa, qb, carry, alive, _, _, _ in pending]
        for o, (_, _, _, _, _, _, gate, qr, pc) in zip(outs, pending):
            oa_ref[qr, pc] = (o * gate).astype(oa_ref.dtype)

    step(True, True)


def _mix(h_res, g, w_bf, layer, lb, norm_g, seq):
    n = h_res.shape[0]
    nb = n // MIX_TM
    col = lax.broadcasted_iota(jnp.int32, (1, IN_COLS), 1)
    col_scale = jnp.where(col < BRANCH_WIDTH, math.log2(math.e) / math.sqrt(SB_HEAD_DIM), 1.0)
    tri, pair, sign = _hgrn_tables()
    whole = lambda arr: pl.BlockSpec(arr.shape, lambda i: (0,) * arr.ndim,
                                     pipeline_mode=pl.Buffered(1))
    cur = lambda i: (jnp.minimum(i, nb - 1), 0)
    lagged = lambda i: (jnp.maximum(i - 1, 0), 0)
    return pl.pallas_call(
        functools.partial(_mix_kernel, blocks_per_seq=seq // MIX_TM),
        grid=(nb + 1,),
        in_specs=[
            pl.BlockSpec((MIX_TM, D_MODEL), cur),
            pl.BlockSpec((1, D_MODEL), lambda i: (0, 0)),
            pl.BlockSpec((None, D_MODEL, IN_COLS), lambda i: (layer, 0, 0),
                         pipeline_mode=pl.Buffered(1)),
            pl.BlockSpec((1, IN_COLS), lambda i: (0, 0)),
            pl.BlockSpec((1, BRANCH_WIDTH), lambda i: (0, 0)),
            pl.BlockSpec((1, BRANCH_WIDTH), lambda i: (0, 0)),
            whole(tri), whole(pair), whole(sign),
        ],
        out_specs=[
            pl.BlockSpec((MIX_TM, PM_COLS), cur),
            pl.BlockSpec((MIX_TM, BRANCH_WIDTH), lagged),
            pl.BlockSpec((MIX_TM, BRANCH_WIDTH), lagged),
        ],
        out_shape=[jax.ShapeDtypeStruct((n, PM_COLS), BF16),
                   jax.ShapeDtypeStruct((n, BRANCH_WIDTH), BF16),
                   jax.ShapeDtypeStruct((n, BRANCH_WIDTH), BF16)],
        scratch_shapes=[pltpu.VMEM((2, MIX_TM, 2 * BRANCH_WIDTH), BF16),
                        pltpu.VMEM((2, seq, 2 * BRANCH_WIDTH), BF16),
                        pltpu.VMEM((2, MIX_TM, 4 * BRANCH_WIDTH), BF16),
                        pltpu.VMEM((HGRN_HEADS, LANES, LANES), F32)],
        compiler_params=pltpu.CompilerParams(dimension_semantics=("arbitrary",),
                                             vmem_limit_bytes=VMEM_LIMIT_MIX),
        name="mix",
    )(h_res, g, w_bf, col_scale.astype(F32), lb.reshape(1, BRANCH_WIDTH),
      norm_g.reshape(1, BRANCH_WIDTH), tri, pair, sign)


def _merge_kernel(oa_ref, oc_ref, pu_ref, pz_ref, ga_ref, gb_ref, gc_ref, pw32_ref, ps_ref,
                  wb32_ref, wo32_ref, res_ref, fg_ref, out_ref, halo_scr, pw_ref, wb_ref, wo_ref, *,
                  final, tiles_per_seq):
    ti = pl.program_id(0) % tiles_per_seq

    @pl.when(pl.program_id(0) == 0)
    def _():
        pw_ref[...] = pw32_ref[...].astype(BF16)
        wb_ref[...] = wb32_ref[...].astype(BF16)
        wo_ref[...] = wo32_ref[...].astype(BF16)

    @pl.when(ti == 0)
    def _():
        halo_scr[...] = jnp.zeros_like(halo_scr)

    for r0 in range(0, MERGE_TM, MERGE_ROWS):
        rows = slice(r0, r0 + MERGE_ROWS)
        o_b = _pool_tile(pu_ref, pz_ref, pw_ref, ps_ref, halo_scr, rows, ti * MERGE_TM)
        merged = (ga_ref[rows, :].astype(F32) * _dot(oa_ref[rows, :], wb_ref[0])
                  + gb_ref[rows, :].astype(F32) * _dot(o_b, wb_ref[1])
                  + gc_ref[rows, :].astype(F32) * _dot(oc_ref[rows, :], wb_ref[2]))
        h = res_ref[rows, :] + _dot(merged.astype(BF16), wo_ref[...])
        if final:
            ms = jnp.mean(h * h, axis=-1, keepdims=True)
            h = h * lax.rsqrt(ms + EPS) * fg_ref[...]
        out_ref[rows, :] = h


def _merge(o_a, o_c, pm, pool_w_bf, pool_scale, wb_bf, wo_bf, h_res, final_g, seq, layer):
    n = h_res.shape[0]
    groups = len(POOL_WINDOWS)
    final = layer == DEPTH - 1
    row = lambda width, blk=0: pl.BlockSpec((MERGE_TM, width), lambda i: (i, blk))
    whole = lambda *shape: pl.BlockSpec((None,) + shape, lambda i: (layer,) + (0,) * len(shape),
                                        pipeline_mode=pl.Buffered(1))
    return pl.pallas_call(
        functools.partial(_merge_kernel, final=final, tiles_per_seq=seq // MERGE_TM),
        grid=(n // MERGE_TM,),
        in_specs=[
            row(BRANCH_WIDTH), row(BRANCH_WIDTH),
            row(BRANCH_WIDTH, 0), row(BRANCH_WIDTH, 1),
            row(D_MODEL, 1), row(D_MODEL, 2), row(D_MODEL, 3),
            whole(groups, LANES, LANES), whole(1, BRANCH_WIDTH),
            whole(3, BRANCH_WIDTH, D_MODEL), whole(D_MODEL, D_MODEL),
            row(D_MODEL), pl.BlockSpec((1, D_MODEL), lambda i: (0, 0)),
        ],
        out_specs=row(D_MODEL),
        out_shape=jax.ShapeDtypeStruct((n, D_MODEL), F32),
        scratch_shapes=[pltpu.VMEM((POOL_HALO, BRANCH_WIDTH), F32),
                        pltpu.VMEM((groups, LANES, LANES), BF16),
                        pltpu.VMEM((3, BRANCH_WIDTH, D_MODEL), BF16),
                        pltpu.VMEM((D_MODEL, D_MODEL), BF16)],
        compiler_params=pltpu.CompilerParams(dimension_semantics=("arbitrary",),
                                             vmem_limit_bytes=VMEM_LIMIT_MERGE),
        name="merge_final" if final else "merge",
    )(o_a, o_c, pm, pm, pm, pm, pm, pool_w_bf,
      pool_scale.reshape(DEPTH, 1, BRANCH_WIDTH), wb_bf, wo_bf, h_res, final_g)


def kernel(x, norm_g, w_in, pool_w, pool_scale, hgrn_lb, hgrn_norm_g, w_branch, w_out, final_g):
    bsz, seq, _ = x.shape
    h_res = x.astype(F32).reshape(bsz * seq, D_MODEL)
    lb_all = jnp.cumsum(jax.nn.softmax(hgrn_lb.astype(F32), axis=0), axis=0)
    lb_all = lb_all - lb_all[:1]
    w_in_bf = w_in.astype(BF16)
    pool_w32, wb32, wo32 = pool_w.astype(F32), w_branch.astype(F32), w_out.astype(F32)
    fg = final_g.astype(F32).reshape(1, D_MODEL)
    for layer in range(DEPTH):
        pm, o_a, o_c = _mix(h_res, norm_g[layer].astype(F32).reshape(1, D_MODEL), w_in_bf, layer,
                            lb_all[layer], hgrn_norm_g[layer].astype(F32), seq)
        h_res = _merge(o_a, o_c, pm, pool_w32, pool_scale.astype(F32),
                       wb32, wo32, h_res, fg, seq, layer)
    return h_res.reshape(bsz, seq, D_MODEL).astype(x.dtype)
```

```python
import functools
import math

import jax
import jax.numpy as jnp
import numpy as np
from jax import lax
from jax.experimental import pallas as pl
from jax.experimental.pallas import tpu as pltpu

D_MODEL = 1024
DEPTH = 2
BRANCH_WIDTH = D_MODEL // 2
SB_HEAD_DIM = 64
POOL_WINDOWS = (2, 4, 8, 16)
POOL_HALO = 16
HGRN_HEADS = 4
EPS = 1e-6
IN_COLS = 10 * BRANCH_WIDTH + 3 * D_MODEL

LANES = 128
SUBLANES = 8
VMEM_LIMIT_MIX = 62 * 1024 * 1024
VMEM_LIMIT_MERGE = 48 * 1024 * 1024

COL_SB_Q, COL_SB_K, COL_SB_V, COL_SB_Z = 0, 512, 1024, 1536
COL_POOL, COL_HG, COL_GATE = 2048, 3072, 5120
PM_COLS = 2 * BRANCH_WIDTH + 3 * D_MODEL

MIX_TM, MIX_TN = 512, 256
MIX_TICKS_PER_CHUNK = 6
SB_T = 256
SB_DEAD_LOG2 = -152.0
SB_OFF_LOG2 = -1e30
SB_Z_CAP = 126.0
HG_T = 32
MERGE_TM = 512
MERGE_ROWS = 512

F32 = jnp.float32
BF16 = jnp.bfloat16


def _dot(a, b):
    return jnp.dot(a, b, preferred_element_type=F32)


def _dot_nt(a, b):
    return lax.dot_general(a, b, (((1,), (1,)), ((), ())), preferred_element_type=F32)


def _dot_tn(a, b):
    return lax.dot_general(a, b, (((0,), (0,)), ((), ())), preferred_element_type=F32)


def _split3(x):
    hi = x.astype(BF16)
    r = x - hi.astype(F32)
    mid = r.astype(BF16)
    lo = (r - mid.astype(F32)).astype(BF16)
    return hi, mid, lo


def _silu(x):
    return x * jax.nn.sigmoid(x)


def _sb_stages(ktile, vtile, tick):
    t = SB_T
    lane = lax.broadcasted_iota(jnp.int32, (1, LANES), 1)
    rows = lax.broadcasted_iota(jnp.int32, (t, t), 0)
    cols = lax.broadcasted_iota(jnp.int32, (t, t), 1)
    causal = cols < rows
    neg_upper = jnp.where(rows > cols, -1.0, 0.0).astype(BF16)

    def chain(z, masked):
        nl = jnp.maximum(z, jnp.log2(1.0 + jnp.exp2(jnp.minimum(z, SB_Z_CAP))))
        lz = z - nl
        if masked:
            nl = jnp.where(causal, nl, 0.0)
        out = lz, _dot(nl.astype(BF16), neg_upper), jnp.sum(nl, axis=-1, keepdims=True)
        tick()
        return out

    def weights(lz, rem, c, masked):
        a = jnp.exp2(lz + rem if c is None else lz + rem + c)
        if masked:
            a = jnp.where(causal, a, 0.0)
        tick()
        return a.astype(BF16)

    def live(carry):
        return jnp.max(jnp.maximum(carry[1], carry[3])) > SB_DEAD_LOG2

    def split(q):
        return (jnp.where(lane < SB_HEAD_DIM, q, jnp.zeros_like(q)),
                jnp.where(lane < SB_HEAD_DIM, jnp.zeros_like(q), q))

    def near(i, qa, qb):
        k0, v0 = ktile(i), vtile(i)
        prev = jnp.maximum(i - 1, 0)
        k1, v1 = ktile(prev), vtile(prev)
        z0a, z0b = _dot_nt(qa, k0), _dot_nt(qb, k0)
        z1a, z1b = _dot_nt(qa, k1), _dot_nt(qb, k1)
        lz0a, rem0a, rs0a = chain(z0a, True)
        lz0b, rem0b, rs0b = chain(z0b, True)
        acc_a = _dot(weights(lz0a, rem0a, None, True), v0)
        lz1a, rem1a, rs1a = chain(z1a, False)
        acc_b = _dot(weights(lz0b, rem0b, None, True), v0)
        lz1b, rem1b, rs1b = chain(z1b, False)
        off = jnp.where(i > 0, 0.0, SB_OFF_LOG2)
        c1a = off - rs0a
        c1b = off - rs0b
        acc_a = acc_a + _dot(weights(lz1a, rem1a, c1a, False), v1)
        acc_b = acc_b + _dot(weights(lz1b, rem1b, c1b, False), v1)
        return acc_a, c1a - rs1a, acc_b, c1b - rs1b

    def far(i, qa, qb, carry, alive):
        def body(state):
            j, _, (acc_a, c_a, acc_b, c_b) = state
            k, v = ktile(j), vtile(j)
            lz_a, rem_a, rs_a = chain(_dot_nt(qa, k), False)
            lz_b, rem_b, rs_b = chain(_dot_nt(qb, k), False)
            acc_a = acc_a + _dot(weights(lz_a, rem_a, c_a, False), v)
            acc_b = acc_b + _dot(weights(lz_b, rem_b, c_b, False), v)
            new = (acc_a, c_a - rs_a, acc_b, c_b - rs_b)
            return j - 1, live(new), new

        state = lax.while_loop(lambda st: (st[0] >= 0) & st[1], body, (i - 2, alive, carry))
        carry = state[2]
        return jnp.where(lane < SB_HEAD_DIM, carry[0], carry[2])

    return split, near, live, far


def _pool_tile(u_ref, z_ref, w_ref, sc_ref, halo_scr, rows, first_pos):
    tm = rows.stop - rows.start
    pos = first_pos + rows.start + lax.broadcasted_iota(jnp.int32, (tm, 1), 0)
    outs = []
    for g, w in enumerate(POOL_WINDOWS):
        cols = slice(g * LANES, (g + 1) * LANES)
        u = u_ref[rows, cols].astype(F32)
        s = jnp.concatenate([halo_scr[:, cols], u], axis=0)
        k = 1
        while k < w:
            s = s + pltpu.roll(s, k, axis=0)
            k *= 2
        cnt = jnp.minimum(pos + 1, w).astype(F32)
        d = s[POOL_HALO:, :] / cnt - u
        outs.append(_dot(d.astype(BF16), w_ref[g]))
        halo_scr[:, cols] = u[tm - POOL_HALO:, :]
    y = jnp.concatenate(outs, axis=1)
    return (y * sc_ref[...] * z_ref[rows, :].astype(F32)).astype(BF16)


HG_LEVELS = HG_T.bit_length() - 1


def _hgrn_tables():
    r = np.arange(HG_T)
    tri = (r[None, :] <= r[:, None]).astype(np.float32)
    pair = np.zeros((HG_LEVELS, HG_T, HG_T), np.float32)
    sign = np.zeros((HG_LEVELS, HG_T, LANES), np.float32)
    for l in range(HG_LEVELS):
        m = 1 << l
        second = (r % (2 * m)) >= m
        same = (r[:, None] // (2 * m)) == (r[None, :] // (2 * m))
        pair[l] = same & second[:, None] & ~second[None, :]
        sign[l] = np.where(second, 1.0, -1.0)[:, None]
    return jnp.asarray(tri, BF16), jnp.asarray(pair), jnp.asarray(sign)


def _hgrn_edge(b, level):
    t = b.shape[0]
    m = 1 << level
    if m >= SUBLANES:
        blk = 2 * m
        e = b.reshape(t // blk, blk, LANES)[:, m - 1:m, :]
        return jnp.broadcast_to(e, (t // blk, blk, LANES)).reshape(t, LANES)
    b3 = b.reshape(t // SUBLANES, SUBLANES, LANES)
    if m == 4:
        e = jnp.broadcast_to(b3[:, 3:4, :], b3.shape)
    else:
        sub = lax.broadcasted_iota(jnp.int32, (1, SUBLANES, 1), 1)
        e = jnp.where(sub < 4, b3[:, 1:2, :], b3[:, 5:6, :])
    return e.reshape(t, LANES)


def _hgrn_stages(blk, rows, lb_ref, g_ref, tri_ref, pair_ref, sign_ref, o_ref, tick):
    t = HG_T

    def col(part, cols):
        return blk[rows, part * BRANCH_WIDTH + cols.start:part * BRANCH_WIDTH + cols.stop]

    def front(cols):
        lb = lb_ref[:, cols]
        sig = jax.nn.sigmoid(col(1, cols).astype(F32))
        f = lb + (1.0 - lb) * sig
        kk = (1.0 - lb) * (1.0 - sig)
        hi, mid, lo = _split3(jnp.log2(f))
        tri = tri_ref[...]
        b = _dot(tri, hi) + _dot(tri, mid) + _dot(tri, lo)
        tick()
        return f, kk, b

    def middle(cols, st, f, kk, b):
        q = col(0, cols).astype(F32)
        v_bf = col(2, cols)
        b_last = b[t - 1:t, :]
        o = _dot_nt((q * jnp.exp2(b)).astype(BF16), st.astype(BF16))
        kd = (kk * jnp.exp2(b_last - b)).astype(BF16)
        st = st * jnp.exp2(b_last) + _dot_tn(v_bf, kd)
        o = o + jnp.sum(q * kk, axis=-1, keepdims=True) * v_bf.astype(F32)
        amat = jnp.zeros((t, t), F32)
        for level in range(HG_LEVELS):
            if level == 0:
                dec = jnp.where(sign_ref[0] > 0.0, f, 1.0)
            else:
                dec = jnp.exp2((b - _hgrn_edge(b, level)) * sign_ref[level])
            p = _dot_nt((q * dec).astype(BF16), (kk * dec).astype(BF16))
            amat = amat + p * pair_ref[level]
            tick()
        return st, o, amat.astype(BF16), v_bf

    def back(cols, o, amat, v_bf):
        o = o + _dot(amat, v_bf)
        ms = jnp.mean(o * o, axis=-1, keepdims=True)
        o = o * lax.rsqrt(ms + EPS) * g_ref[:, cols]
        o_ref[rows, cols] = (o * _silu(col(3, cols).astype(F32))).astype(o_ref.dtype)
        tick()

    return front, middle, back


def _mix_kernel(x_ref, g_ref, w_ref, cs_ref, lb_ref, ng_ref, tri_ref, pair_ref, sign_ref,
                pm_ref, oa_ref, oc_ref, qz_scr, kv_scr, hg_scr, st_scr, *, blocks_per_seq):
    n = pl.program_id(0)
    slot = n % 2
    bps = blocks_per_seq

    @pl.when(n == 0)
    def _():
        qz_scr[...] = jnp.zeros_like(qz_scr)
        kv_scr[...] = jnp.zeros_like(kv_scr)
        hg_scr[...] = jnp.zeros_like(hg_scr)
        st_scr[...] = jnp.zeros_like(st_scr)

    lag_pos = (n + bps - 1) % bps
    lag_kv = kv_scr.at[((n + 2 * bps - 1) // bps) % 2]
    lag_qz = qz_scr.at[1 - slot]
    lag_hg = hg_scr.at[1 - slot]
    cur_qz = qz_scr.at[slot]
    cur_hg = hg_scr.at[slot]
    cur_kv = kv_scr.at[(n // bps) % 2]
    cur_rows = pl.ds(pl.multiple_of((n % bps) * MIX_TM, MIX_TM), MIX_TM)

    def step(with_projection, with_mixers):
        if with_projection:
            x = x_ref[...]
            ms = jnp.mean(x * x, axis=-1, keepdims=True)
            hn = (x * lax.rsqrt(ms + EPS) * g_ref[...]).astype(BF16)

        def project(c):
            lo, hi = c * MIX_TN, (c + 1) * MIX_TN
            y = _dot(hn, w_ref[:, lo:hi])
            if lo < BRANCH_WIDTH:
                y = y * cs_ref[:, lo:hi]
            elif lo >= COL_GATE:
                y = jax.nn.sigmoid(y)
            elif COL_POOL + BRANCH_WIDTH <= lo < COL_HG:
                y = _silu(y)
            y = y.astype(BF16)
            if lo < COL_SB_K:
                cur_qz[:, lo:hi] = y
            elif lo < COL_SB_Z:
                cur_kv[cur_rows, lo - COL_SB_K:hi - COL_SB_K] = y
            elif lo < COL_POOL:
                cur_qz[:, lo - COL_SB_Z + BRANCH_WIDTH:hi - COL_SB_Z + BRANCH_WIDTH] = y
            elif lo < COL_HG:
                pm_ref[:, lo - COL_POOL:hi - COL_POOL] = y
            elif lo < COL_GATE:
                cur_hg[:, lo - COL_HG:hi - COL_HG] = y
            else:
                pm_ref[:, lo - COL_GATE + 2 * BRANCH_WIDTH:hi - COL_GATE + 2 * BRANCH_WIDTH] = y

        chunks = iter(range(IN_COLS // MIX_TN if with_projection else 0))
        ticks = [0]

        def tick():
            ticks[0] += 1
            if ticks[0] % MIX_TICKS_PER_CHUNK == 0:
                c = next(chunks, None)
                if c is not None:
                    project(c)

        if not with_mixers:
            for c in chunks:
                project(c)
            return

        fresh = lag_pos == 0
        heads = [slice(h * LANES, (h + 1) * LANES) for h in range(HGRN_HEADS)]
        states = [jnp.where(fresh, 0.0, st_scr[h]) for h in range(HGRN_HEADS)]
        pending = []
        n_tiles = MIX_TM // HG_T
        n_pairs = BRANCH_WIDTH // LANES
        for k in range(n_tiles):
            rows = slice(k * HG_T, (k + 1) * HG_T)
            front, middle, back = _hgrn_stages(lag_hg, rows, lb_ref, ng_ref, tri_ref, pair_ref,
                                               sign_ref, oc_ref, tick)
            fronts = [front(c) for c in heads]
            mids = [middle(c, st, *fr) for c, st, fr in zip(heads, states, fronts)]
            states = [md[0] for md in mids]
            for c, md in zip(heads, mids):
                back(c, *md[1:])

            for p in range(k * n_pairs // n_tiles, (k + 1) * n_pairs // n_tiles):
                pc = slice(p * LANES, (p + 1) * LANES)
                split, near, live, far = _sb_stages(
                    lambda j, pc=pc: lag_kv[pl.ds(pl.multiple_of(j * SB_T, SB_T), SB_T), pc],
                    lambda j, pc=pc: lag_kv[pl.ds(pl.multiple_of(j * SB_T, SB_T), SB_T),
                                            BRANCH_WIDTH + pc.start:BRANCH_WIDTH + pc.stop],
                    tick)
                for u in range(MIX_TM // SB_T):
                    qr = slice(u * SB_T, (u + 1) * SB_T)
                    qa, qb = split(lag_qz[qr, pc])
                    i = lag_pos * (MIX_TM // SB_T) + u
                    carry = near(i, qa, qb)
                    gate = _silu(
                        lag_qz[qr, BRANCH_WIDTH + pc.start:BRANCH_WIDTH + pc.stop].astype(F32))
                    pending.append((far, i, qa, qb, carry, live(carry), gate, qr, pc))
        for c in chunks:
            project(c)
        for h in range(HGRN_HEADS):
            st_scr[h] = states[h]
        outs = [far(i, qa, qb, carry, alive) for far, i, qa, qb, carry, alive, _, _, _ in pending]
        for o, (_, _, _, _, _, _, gate, qr, pc) in zip(outs, pending):
            oa_ref[qr, pc] = (o * gate).astype(oa_ref.dtype)

    step(True, True)


def _mix(h_res, g, w_bf, layer, lb, norm_g, seq):
    n = h_res.shape[0]
    nb = n // MIX_TM
    col = lax.broadcasted_iota(jnp.int32, (1, IN_COLS), 1)
    col_scale = jnp.where(col < BRANCH_WIDTH, math.log2(math.e) / math.sqrt(SB_HEAD_DIM), 1.0)
    tri, pair, sign = _hgrn_tables()
    whole = lambda arr: pl.BlockSpec(arr.shape, lambda i: (0,) * arr.ndim,
                                     pipeline_mode=pl.Buffered(1))
    cur = lambda i: (jnp.minimum(i, nb - 1), 0)
    lagged = lambda i: (jnp.maximum(i - 1, 0), 0)
    return pl.pallas_call(
        functools.partial(_mix_kernel, blocks_per_seq=seq // MIX_TM),
        grid=(nb + 1,),
        in_specs=[
            pl.BlockSpec((MIX_TM, D_MODEL), cur),
            pl.BlockSpec((1, D_MODEL), lambda i: (0, 0)),
            pl.BlockSpec((None, D_MODEL, IN_COLS), lambda i: (layer, 0, 0),
                         pipeline_mode=pl.Buffered(1)),
            pl.BlockSpec((1, IN_COLS), lambda i: (0, 0)),
            pl.BlockSpec((1, BRANCH_WIDTH), lambda i: (0, 0)),
            pl.BlockSpec((1, BRANCH_WIDTH), lambda i: (0, 0)),
            whole(tri), whole(pair), whole(sign),
        ],
        out_specs=[
            pl.BlockSpec((MIX_TM, PM_COLS), cur),
            pl.BlockSpec((MIX_TM, BRANCH_WIDTH), lagged),
            pl.BlockSpec((MIX_TM, BRANCH_WIDTH), lagged),
        ],
        out_shape=[jax.ShapeDtypeStruct((n, PM_COLS), BF16),
                   jax.ShapeDtypeStruct((n, BRANCH_WIDTH), BF16),
                   jax.ShapeDtypeStruct((n, BRANCH_WIDTH), BF16)],
        scratch_shapes=[pltpu.VMEM((2, MIX_TM, 2 * BRANCH_WIDTH), BF16),
                        pltpu.VMEM((2, seq, 2 * BRANCH_WIDTH), BF16),
                        pltpu.VMEM((2, MIX_TM, 4 * BRANCH_WIDTH), BF16),
                        pltpu.VMEM((HGRN_HEADS, LANES, LANES), F32)],
        compiler_params=pltpu.CompilerParams(dimension_semantics=("arbitrary",),
                                             vmem_limit_bytes=VMEM_LIMIT_MIX),
        name="mix",
    )(h_res, g, w_bf, col_scale.astype(F32), lb.reshape(1, BRANCH_WIDTH),
      norm_g.reshape(1, BRANCH_WIDTH), tri, pair, sign)


def _merge_kernel(oa_ref, oc_ref, pu_ref, pz_ref, ga_ref, gb_ref, gc_ref, pw32_ref, ps_ref,
                  wb32_ref, wo32_ref, res_ref, fg_ref, out_ref, halo_scr, pw_ref, wb_ref, wo_ref, *,
                  final, tiles_per_seq):
    ti = pl.program_id(0) % tiles_per_seq

    @pl.when(pl.program_id(0) == 0)
    def _():
        pw_ref[...] = pw32_ref[...].astype(BF16)
        wb_ref[...] = wb32_ref[...].astype(BF16)
        wo_ref[...] = wo32_ref[...].astype(BF16)

    @pl.when(ti == 0)
    def _():
        halo_scr[...] = jnp.zeros_like(halo_scr)

    for r0 in range(0, MERGE_TM, MERGE_ROWS):
        rows = slice(r0, r0 + MERGE_ROWS)
        o_b = _pool_tile(pu_ref, pz_ref, pw_ref, ps_ref, halo_scr, rows, ti * MERGE_TM)
        merged = (ga_ref[rows, :].astype(F32) * _dot(oa_ref[rows, :], wb_ref[0])
                  + gb_ref[rows, :].astype(F32) * _dot(o_b, wb_ref[1])
                  + gc_ref[rows, :].astype(F32) * _dot(oc_ref[rows, :], wb_ref[2]))
        h = res_ref[rows, :] + _dot(merged.astype(BF16), wo_ref[...])
        if final:
            ms = jnp.mean(h * h, axis=-1, keepdims=True)
            h = h * lax.rsqrt(ms + EPS) * fg_ref[...]
        out_ref[rows, :] = h


def _merge(o_a, o_c, pm, pool_w_bf, pool_scale, wb_bf, wo_bf, h_res, final_g, seq, layer):
    n = h_res.shape[0]
    groups = len(POOL_WINDOWS)
    final = layer == DEPTH - 1
    row = lambda width, blk=0: pl.BlockSpec((MERGE_TM, width), lambda i: (i, blk))
    whole = lambda *shape: pl.BlockSpec((None,) + shape, lambda i: (layer,) + (0,) * len(shape),
                                        pipeline_mode=pl.Buffered(1))
    return pl.pallas_call(
        functools.partial(_merge_kernel, final=final, tiles_per_seq=seq // MERGE_TM),
        grid=(n // MERGE_TM,),
        in_specs=[
            row(BRANCH_WIDTH), row(BRANCH_WIDTH),
            row(BRANCH_WIDTH, 0), row(BRANCH_WIDTH, 1),
            row(D_MODEL, 1), row(D_MODEL, 2), row(D_MODEL, 3),
            whole(groups, LANES, LANES), whole(1, BRANCH_WIDTH),
            whole(3, BRANCH_WIDTH, D_MODEL), whole(D_MODEL, D_MODEL),
            row(D_MODEL), pl.BlockSpec((1, D_MODEL), lambda i: (0, 0)),
        ],
        out_specs=row(D_MODEL),
        out_shape=jax.ShapeDtypeStruct((n, D_MODEL), F32),
        scratch_shapes=[pltpu.VMEM((POOL_HALO, BRANCH_WIDTH), F32),
                        pltpu.VMEM((groups, LANES, LANES), BF16),
                        pltpu.VMEM((3, BRANCH_WIDTH, D_MODEL), BF16),
                        pltpu.VMEM((D_MODEL, D_MODEL), BF16)],
        compiler_params=pltpu.CompilerParams(dimension_semantics=("arbitrary",),
                                             vmem_limit_bytes=VMEM_LIMIT_MERGE),
        name="merge_final" if final else "merge",
    )(o_a, o_c, pm, pm, pm, pm, pm, pool_w_bf,
      pool_scale.reshape(DEPTH, 1, BRANCH_WIDTH), wb_bf, wo_bf, h_res, final_g)


def kernel(x, norm_g, w_in, pool_w, pool_scale, hgrn_lb, hgrn_norm_g, w_branch, w_out, final_g):
    bsz, seq, _ = x.shape
    h_res = x.astype(F32).reshape(bsz * seq, D_MODEL)
    lb_all = jnp.cumsum(jax.nn.softmax(hgrn_lb.astype(F32), axis=0), axis=0)
    lb_all = lb_all - lb_all[:1]
    w_in_bf = w_in.astype(BF16)
    pool_w32, wb32, wo32 = pool_w.astype(F32), w_branch.astype(F32), w_out.astype(F32)
    fg = final_g.astype(F32).reshape(1, D_MODEL)
    for layer in range(DEPTH):
        pm, o_a, o_c = _mix(h_res, norm_g[layer].astype(F32).reshape(1, D_MODEL), w_in_bf, layer,
                            lb_all[layer], hgrn_norm_g[layer].astype(F32), seq)
        h_res = _merge(o_a, o_c, pm, pool_w32, pool_scale.astype(F32),
                       wb32, wo32, h_res, fg, seq, layer)
    return h_res.reshape(bsz, seq, D_MODEL).astype(x.dtype)
```
